```python
import math
import jax, jax.numpy as jnp
from jax import lax
import numpy as np

D_MODEL = 1024
BATCH = 4
SEQ = 8192
DEPTH = 1

HG_HEADS = 4
HG_HEAD_K = 128
HG_HEAD_V = 128
HG_KEY = HG_HEADS * HG_HEAD_K
HG_VAL = HG_HEADS * HG_HEAD_V
GLA_HEADS = 4
GLA_HEAD_K = 64
GLA_HEAD_V = 128
GLA_KEY = GLA_HEADS * GLA_HEAD_K
GLA_VAL = GLA_HEADS * GLA_HEAD_V
GLA_GATE_RANK = 16
GLA_GATE_NORMALIZER = 16.0
MIX_WIDTH = HG_VAL + GLA_VAL
IN_SPLITS = (HG_KEY, HG_KEY, HG_VAL, HG_VAL, GLA_KEY, GLA_KEY, GLA_VAL, GLA_GATE_RANK, GLA_VAL)
IN_WIDTH = sum(IN_SPLITS)
CHUNK = 64
PEER_HEADS = 8
PEER_N_KEYS = 128
PEER_N_EXPERTS = PEER_N_KEYS * PEER_N_KEYS
PEER_QUERY_DIM = 256
PEER_HALF = PEER_QUERY_DIM // 2
PEER_TOPK = 16
PEER_BLOCK = 128
EPS = 1e-6

kernel_name = "hymba_hgrn2_gla_peer_block"


def rms_norm(x, g):
    xf = x.astype(jnp.float32)
    y = xf * lax.rsqrt(jnp.mean(xf * xf, axis=-1, keepdims=True) + EPS)
    return (y * g.astype(jnp.float32)).astype(x.dtype)


def split_heads(a, n_heads):
    b, t, w = a.shape
    return a.reshape(b, t, n_heads, w // n_heads).transpose(0, 2, 1, 3)


def merge_heads(a):
    b, n, t, d = a.shape
    return a.transpose(0, 2, 1, 3).reshape(b, t, n * d)


def chunk_gated_linear_attention(q, k, v, log_g):
    b, h, t, dk = q.shape
    dv = v.shape[-1]
    n_chunks = t // CHUNK

    def to_chunks(a):
        return a.astype(jnp.float32).reshape(b, h, n_chunks, CHUNK, a.shape[-1]).transpose(2, 0, 1, 3, 4)

    qc, kc, vc, gc = to_chunks(q), to_chunks(k), to_chunks(v), to_chunks(log_g)
    causal = jnp.tril(jnp.ones((CHUNK, CHUNK), dtype=bool))[:, :, None]

    def step(state, inp):
        qb, kb, vb, gb = inp
        cum = jnp.cumsum(gb, axis=2)
        rel = cum[:, :, :, None, :] - cum[:, :, None, :, :]
        decay = jnp.exp(jnp.where(causal, rel, -jnp.inf))
        scores = jnp.einsum('bhid,bhjd,bhijd->bhij', qb, kb, decay)
        out = (jnp.einsum('bhij,bhjv->bhiv', scores, vb)
               + jnp.einsum('bhid,bhdv->bhiv', qb * jnp.exp(cum), state))
        last = cum[:, :, -1:, :]
        state = (jnp.exp(last[:, :, 0, :])[..., None] * state
                 + jnp.einsum('bhjd,bhjv->bhdv', kb * jnp.exp(last - cum), vb))
        return state, out

    s0 = jnp.zeros((b, h, dk, dv), jnp.float32)
    _, o = lax.scan(step, s0, (qc, kc, vc, gc))
    return o.transpose(1, 2, 0, 3, 4).reshape(b, h, t, dv)


def hgrn2_mixer(q, f, i, gate, lower_bound, norm_g):
    q = jax.nn.silu(q)
    forget = lower_bound + (1.0 - lower_bound) * jax.nn.sigmoid(f.astype(jnp.float32))
    key = 1.0 - forget
    log_f = jnp.log(forget)
    o = chunk_gated_linear_attention(split_heads(q, HG_HEADS), split_heads(key, HG_HEADS),
                                     split_heads(i, HG_HEADS), split_heads(log_f, HG_HEADS))
    o = merge_heads(rms_norm(o, norm_g))
    return (o * jax.nn.silu(gate.astype(jnp.float32))).astype(gate.dtype)


def gla_mixer(q, k, v, gate_low, gate, w_gate_up, b_gate, norm_g):
    log_g = jax.nn.log_sigmoid((gate_low @ w_gate_up + b_gate).astype(jnp.float32)) / GLA_GATE_NORMALIZER
    q = q * (GLA_HEAD_K ** -0.5)
    o = chunk_gated_linear_attention(split_heads(q, GLA_HEADS), split_heads(k, GLA_HEADS),
                                     split_heads(v, GLA_HEADS), split_heads(log_g, GLA_HEADS))
    o = merge_heads(rms_norm(o, norm_g))
    return (o * jax.nn.silu(gate.astype(jnp.float32))).astype(gate.dtype)


def peer_ffn(xn, w_q, sub_keys, u, v):
    b, t, d = xn.shape
    blocks = xn.reshape(-1, PEER_BLOCK, d)

    def block(xb):
        q = (xb @ w_q).reshape(PEER_BLOCK, PEER_HEADS, 2, PEER_HALF)
        s = jnp.einsum('thpd,hpnd->thpn', q, sub_keys).astype(jnp.float32)
        s_top, i_top = lax.top_k(s, PEER_TOPK)
        cand = s_top[:, :, 0, :, None] + s_top[:, :, 1, None, :]
        cand_idx = i_top[:, :, 0, :, None] * PEER_N_KEYS + i_top[:, :, 1, None, :]
        cand = cand.reshape(PEER_BLOCK, PEER_HEADS, PEER_TOPK * PEER_TOPK)
        cand_idx = cand_idx.reshape(PEER_BLOCK, PEER_HEADS, PEER_TOPK * PEER_TOPK)
        best, pos = lax.top_k(cand, PEER_TOPK)
        expert = jnp.take_along_axis(cand_idx, pos, axis=-1)
        g = jax.nn.softmax(best, axis=-1)
        act = jax.nn.gelu(jnp.einsum('td,thkd->thk', xb, u[expert]), approximate=False)
        return jnp.einsum('thk,thkd->td', (g * act).astype(xb.dtype), v[expert])

    return lax.map(block, blocks).reshape(b, t, d)


def setup_inputs(seed: int = 0) -> dict:
    key = jax.random.key(seed)
    ks = jax.random.split(key, 16)
    nrm = jax.random.normal
    L = DEPTH
    return {
        "x": nrm(ks[0], (BATCH, SEQ, D_MODEL), jnp.float32),
        "norm1_g": 1.0 + 0.02 * nrm(ks[1], (L, D_MODEL), jnp.float32),
        "w_in": nrm(ks[2], (L, D_MODEL, IN_WIDTH), jnp.float32) * D_MODEL ** -0.5,
        "hg_lower_logits": 0.1 * nrm(ks[3], (L + 1, HG_KEY), jnp.float32),
        "hg_norm_g": 1.0 + 0.02 * nrm(ks[4], (L, HG_HEAD_V), jnp.float32),
        "gla_w_gate_up": nrm(ks[5], (L, GLA_GATE_RANK, GLA_KEY), jnp.float32) * GLA_GATE_RANK ** -0.5,
        "gla_b_gate": 0.1 * nrm(ks[6], (L, GLA_KEY), jnp.float32),
        "gla_norm_g": 1.0 + 0.02 * nrm(ks[7], (L, GLA_HEAD_V), jnp.float32),
        "w_out": nrm(ks[8], (L, MIX_WIDTH, D_MODEL), jnp.float32) * MIX_WIDTH ** -0.5,
        "norm2_g": 1.0 + 0.02 * nrm(ks[9], (L, D_MODEL), jnp.float32),
        "peer_w_q": nrm(ks[10], (L, D_MODEL, PEER_HEADS * PEER_QUERY_DIM), jnp.float32) * D_MODEL ** -0.5,
        "peer_sub_keys": nrm(ks[11], (L, PEER_HEADS, 2, PEER_N_KEYS, PEER_HALF), jnp.float32) * PEER_HALF ** -0.5,
        "peer_u": nrm(ks[12], (L, PEER_N_EXPERTS, D_MODEL), jnp.float32) * D_MODEL ** -0.5,
        "peer_v": nrm(ks[13], (L, PEER_N_EXPERTS, D_MODEL), jnp.float32) * 0.3,
        "norm_f_g": 1.0 + 0.02 * nrm(ks[14], (D_MODEL,), jnp.float32),
    }


def reference(x, norm1_g, w_in, hg_lower_logits, hg_norm_g, gla_w_gate_up, gla_b_gate, gla_norm_g,
              w_out, norm2_g, peer_w_q, peer_sub_keys, peer_u, peer_v, norm_f_g):
    lower_bounds = jnp.cumsum(jax.nn.softmax(hg_lower_logits.astype(jnp.float32), axis=0), axis=0)
    offsets = [int(o) for o in np.cumsum(IN_SPLITS)[:-1]]
    h = x
    for layer in range(DEPTH):
        xn = rms_norm(h, norm1_g[layer])
        proj = xn @ w_in[layer]
        hq, hf, hi, hgate, gq, gk, gv, glow, ggate = jnp.split(proj, offsets, axis=-1)
        o_hgrn = hgrn2_mixer(hq, hf, hi, hgate, lower_bounds[layer], hg_norm_g[layer])
        o_gla = gla_mixer(gq, gk, gv, glow, ggate, gla_w_gate_up[layer], gla_b_gate[layer], gla_norm_g[layer])
        mixed = jnp.concatenate([o_hgrn, o_gla], axis=-1)
        h = h + (mixed @ w_out[layer]).astype(h.dtype)
        hn = rms_norm(h, norm2_g[layer])
        h = h + peer_ffn(hn, peer_w_q[layer], peer_sub_keys[layer], peer_u[layer], peer_v[layer]).astype(h.dtype)
    return rms_norm(h, norm_f_g)
```

```python
import functools

import jax
import jax.numpy as jnp
import numpy as np
from jax import lax
from jax.experimental import pallas as pl
from jax.experimental.pallas import tpu as pltpu
from jax.experimental.pallas import tpu_sc as plsc

F32 = jnp.float32
BF16 = jnp.bfloat16
EPS = 1e-6

D_MODEL = 1024
HG_HEADS = 4
GLA_HEADS = 4
HEAD_V = 128
HG_KEY = 512
GLA_KEY = 256
GLA_HEAD_K = 64
GLA_GATE_RANK = 16
GLA_GATE_NORMALIZER = 16.0
CHUNK = 64
LANES = 128
PEER_HEADS = 8
PEER_N_KEYS = 128
PEER_HALF = 128
PEER_TOPK = 16

OFF_HQ, OFF_HF, OFF_HI, OFF_HGATE = 0, 512, 1024, 1536
OFF_GQ, OFF_GK, OFF_GV, OFF_GLOW, OFF_GGATE = 2048, 2304, 2560, 3072, 3200
IN_WIDTH_PADDED = 3712

VMEM_LIMIT = 48 * 1024 * 1024

SC_LANES = 16
SC_WORKERS = 32


def _nt(a, b):
    return lax.dot_general(a, b, (((1,), (1,)), ((), ())), preferred_element_type=F32)


def _tn(a, b):
    return lax.dot_general(a, b, (((0,), (0,)), ((), ())), preferred_element_type=F32)


def _inproj_body(x_ref, g_ref, w_ref, o_ref):
    x = x_ref[...]
    xn = x * lax.rsqrt(jnp.mean(x * x, axis=-1, keepdims=True) + EPS) * g_ref[...]
    o_ref[...] = jnp.dot(xn.astype(BF16), w_ref[...], preferred_element_type=F32)


def _inproj(x2, g, w, tm=256):
    n, d = x2.shape
    wd = w.shape[1]
    return pl.pallas_call(
        _inproj_body,
        grid=(n // tm,),
        in_specs=[pl.BlockSpec((tm, d), lambda i: (i, 0)),
                  pl.BlockSpec((1, d), lambda i: (0, 0)),
                  pl.BlockSpec((d, wd), lambda i: (0, 0))],
        out_specs=pl.BlockSpec((tm, wd), lambda i: (i, 0)),
        out_shape=jax.ShapeDtypeStruct((n, wd), F32),
        compiler_params=pltpu.CompilerParams(dimension_semantics=("parallel",),
                                             vmem_limit_bytes=VMEM_LIMIT),
        name="inproj",
    )(x2, g, w)


def _level_constants():
    c = CHUNK
    mats = [np.tril(np.ones((c, c), np.float32))]
    level = np.full((c, c), -1, np.int32)
    b, lvl = c // 2, 0
    while b >= 1:
        m = np.zeros((c, c), np.float32)
        for s in range(0, c, 2 * b):
            mid = s + b
            for i in range(mid, s + 2 * b):
                m[i, mid:i + 1] = 1.0
                level[i, s:mid] = lvl
            for j in range(s, mid):
                m[j, j + 1:mid] = 1.0
        mats.append(m)
        b //= 2
        lvl += 1
    level[np.arange(c), np.arange(c)] = lvl
    return np.concatenate(mats, axis=0), level, lvl


_SEG_MATS, _LEVEL_MAP, _N_LEVELS = _level_constants()


def _split3(a):
    hi = a.astype(BF16)
    r = a - hi.astype(F32)
    mid = r.astype(BF16)
    lo = (r - mid.astype(F32)).astype(BF16)
    return jnp.concatenate([hi, mid, lo], axis=1)


def _gla_chunk(q, k, g, v, st, seg, level):
    c = CHUNK
    ex3 = jnp.dot(seg, _split3(g), preferred_element_type=F32)
    ex = ex3[:, 0:LANES] + ex3[:, LANES:2 * LANES] + ex3[:, 2 * LANES:3 * LANES]
    cum = ex[0:c]
    scores = jnp.where(level == _N_LEVELS, _nt(q.astype(BF16), k.astype(BF16)), 0.0)
    for l in range(_N_LEVELS):
        e = jnp.exp(ex[c * (l + 1):c * (l + 2)])
        p = _nt((q * e).astype(BF16), (k * e).astype(BF16))
        scores = jnp.where(level == l, p, scores)
    last = cum[c - 1:c, :]
    qd = (q * jnp.exp(cum)).astype(BF16)
    kd = (k * jnp.exp(last - cum)).astype(BF16)
    vb = v.astype(BF16)
    o = jnp.dot(scores.astype(BF16), vb, preferred_element_type=F32) + _nt(qd, st.astype(BF16))
    st_new = st * jnp.exp(last) + _tn(vb, kd)
    return o, st_new


def _head_out(o, gain, gate):
    on = o * lax.rsqrt(jnp.mean(o * o, axis=-1, keepdims=True) + EPS) * gain
    return on * (gate * jax.nn.sigmoid(gate))


def _mixer_body(p_ref, seg_ref, level_ref, lbl_ref, hgn_ref, wgu_ref, bg_ref, ggn_ref,
                o_ref, st_ref, *, chunks):
    @pl.when(pl.program_id(1) == 0)
    def _():
        st_ref[...] = jnp.zeros_like(st_ref)

    seg = seg_ref[...]
    level = level_ref[...]
    logits = lbl_ref[...]
    ez = jnp.exp(logits - jnp.max(logits, axis=0, keepdims=True))
    lb = ez[0:1, :] / jnp.sum(ez, axis=0, keepdims=True)
    lane = lax.broadcasted_iota(jnp.int32, (CHUNK, LANES), 1)

    def chunk_body(ci, carry):
        rows = pl.ds(pl.multiple_of(ci * CHUNK, CHUNK), CHUNK)

        def col(off):
            return p_ref[rows, pl.ds(off, LANES)]

        for h in range(HG_HEADS):
            hq = col(OFF_HQ + h * LANES)
            q = hq * jax.nn.sigmoid(hq)
            lbh = lb[:, h * LANES:(h + 1) * LANES]
            forget = lbh + (1.0 - lbh) * jax.nn.sigmoid(col(OFF_HF + h * LANES))
            o, st = _gla_chunk(q, 1.0 - forget, jnp.log(forget), col(OFF_HI + h * LANES),
                               st_ref[h], seg, level)
            st_ref[h] = st
            o_ref[rows, pl.ds(h * HEAD_V, HEAD_V)] = _head_out(
                o, hgn_ref[...], col(OFF_HGATE + h * HEAD_V)).astype(o_ref.dtype)

        zg = jnp.dot(col(OFF_GLOW).astype(BF16), wgu_ref[...], preferred_element_type=F32) + bg_ref[...]
        log_g = (jnp.minimum(zg, 0.0) - jnp.log(1.0 + jnp.exp(-jnp.abs(zg)))) / GLA_GATE_NORMALIZER
        for h in range(GLA_HEADS):
            pair, half = h // 2, h % 2
            q = col(OFF_GQ + pair * LANES) * (GLA_HEAD_K ** -0.5)
            in_head = (lane >= half * GLA_HEAD_K) & (lane < (half + 1) * GLA_HEAD_K)
            k = jnp.where(in_head, col(OFF_GK + pair * LANES), 0.0)
            g = log_g[:, pair * LANES:(pair + 1) * LANES]
            o, st = _gla_chunk(q, k, g, col(OFF_GV + h * HEAD_V), st_ref[HG_HEADS + h], seg, level)
            st_ref[HG_HEADS + h] = st
            o_ref[rows, pl.ds((HG_HEADS + h) * HEAD_V, HEAD_V)] = _head_out(
                o, ggn_ref[...], col(OFF_GGATE + h * HEAD_V)).astype(o_ref.dtype)
        return carry

    lax.fori_loop(0, chunks, chunk_body, 0)


def _mixers(proj, batch, seq, lb_logits, hg_norm_g, wgu, bg, gla_norm_g, tt=512):
    n = batch * seq
    steps = seq // tt
    heads = HG_HEADS + GLA_HEADS
    const = lambda shape: pl.BlockSpec(shape, lambda b, t: (0,) * len(shape))
    return pl.pallas_call(
        functools.partial(_mixer_body, chunks=tt // CHUNK),
        grid=(batch, steps),
        in_specs=[pl.BlockSpec((tt, IN_WIDTH_PADDED), lambda b, t: (b * steps + t, 0)),
                  const(_SEG_MATS.shape), const(_LEVEL_MAP.shape),
                  const(lb_logits.shape), const((1, HEAD_V)),
                  const(wgu.shape), const(bg.shape), const((1, HEAD_V))],
        out_specs=pl.BlockSpec((tt, heads * HEAD_V), lambda b, t: (b * steps + t, 0)),
        out_shape=jax.ShapeDtypeStruct((n, heads * HEAD_V), BF16),
        scratch_shapes=[pltpu.VMEM((heads, HEAD_V, LANES), F32)],
        compiler_params=pltpu.CompilerParams(dimension_semantics=("parallel", "arbitrary"),
                                             vmem_limit_bytes=VMEM_LIMIT),
        name="mixers",
    )(proj, jnp.asarray(_SEG_MATS, BF16), jnp.asarray(_LEVEL_MAP), lb_logits, hg_norm_g,
      wgu, bg, gla_norm_g)


def _outproj_body(x_ref, m_ref, wo_ref, g_ref, wq_ref, h_ref, hn_ref, q_ref):
    h = x_ref[...] + jnp.dot(m_ref[...], wo_ref[...], preferred_element_type=F32)
    h_ref[...] = h
    hn = h * lax.rsqrt(jnp.mean(h * h, axis=-1, keepdims=True) + EPS) * g_ref[...]
    hn_ref[...] = hn
    q = jnp.dot(hn.astype(BF16), wq_ref[...], preferred_element_type=F32)
    for j in range(q_ref.shape[0]):
        q_ref[j] = q[:, j * PEER_HALF:(j + 1) * PEER_HALF]


def _outproj(x2, mixed, wo, g2, wq, tm=256):
    n, d = x2.shape
    nq = wq.shape[1] // PEER_HALF
    return pl.pallas_call(
        _outproj_body,
        grid=(n // tm,),
        in_specs=[pl.BlockSpec((tm, d), lambda i: (i, 0)),
                  pl.BlockSpec((tm, mixed.shape[1]), lambda i: (i, 0)),
                  pl.BlockSpec(wo.shape, lambda i: (0, 0)),
                  pl.BlockSpec((1, d), lambda i: (0, 0)),
                  pl.BlockSpec(wq.shape, lambda i: (0, 0))],
        out_specs=[pl.BlockSpec((tm, d), lambda i: (i, 0)),
                   pl.BlockSpec((tm, d), lambda i: (i, 0)),
                   pl.BlockSpec((nq, tm, PEER_HALF), lambda i: (0, i, 0))],
        out_shape=[jax.ShapeDtypeStruct((n, d), F32),
                   jax.ShapeDtypeStruct((n, d), F32),
                   jax.ShapeDtypeStruct((nq, n, PEER_HALF), F32)],
        compiler_params=pltpu.CompilerParams(dimension_semantics=("parallel",),
                                             vmem_limit_bytes=VMEM_LIMIT),
        name="outproj",
    )(x2, mixed, wo, g2, wq)


def _candidate_constants():
    k = PEER_TOPK
    pos, valid = [], []
    for j in range(k):
        pos.append(j); valid.append(True)
    for i in range(1, 8):
        for j in range(8):
            pos.append(i * k + j); valid.append((i + 1) * (j + 1) <= k)
    for i in range(8, k):
        pos.append(i * k); valid.append(True)
    return np.asarray(pos, np.float32), np.asarray(valid, bool)


_CAND_POS, _CAND_VALID = _candidate_constants()
_N_CAND = _CAND_POS.shape[0]


def _topk_rows(s, pos, extra, k):
    t = s.shape[1]
    slot = lax.broadcasted_iota(jnp.int32, (k, t), 0)

    def body(it, carry):
        s, vals, poss, exts = carry
        m = jnp.max(s, axis=0, keepdims=True)
        p = jnp.min(jnp.where(s == m, pos, 1e9), axis=0, keepdims=True)
        hit = pos == p
        vals = jnp.where(slot == it, m, vals)
        poss = jnp.where(slot == it, p, poss)
        if extra is not None:
            x = jnp.max(jnp.where(hit, extra, -1.0), axis=0, keepdims=True)
            exts = jnp.where(slot == it, x, exts)
        return jnp.where(hit, -jnp.inf, s), vals, poss, exts

    z = jnp.zeros((k, t), F32)
    _, vals, poss, exts = lax.fori_loop(0, k, body, (s, z, z, z))
    return vals, poss, exts


def _route_body(q_ref, keys_ref, cpos_ref, cvalid_ref, idx_ref, gate_ref):
    k = PEER_TOPK
    tb = q_ref.shape[1]
    key_pos = lax.broadcasted_iota(jnp.int32, (PEER_N_KEYS, LANES), 0).astype(F32)
    cpos = cpos_ref[...]
    cvalid = cvalid_ref[...] > 0.5
    for lt in range(tb // LANES):
        cols = pl.ds(lt * LANES, LANES)
        tops = []
        for p in range(2):
            s = _nt(keys_ref[p], q_ref[p, cols, :].astype(BF16))
            v, i, _ = _topk_rows(s, key_pos, None, k)
            tops.append((v, i))
        (v0, i0), (v1, i1) = tops
        vals = [v0[0:1] + v1]
        exps = [i0[0:1] * PEER_N_KEYS + i1]
        for i in range(1, 8):
            vals.append(v0[i:i + 1] + v1[0:8])
            exps.append(i0[i:i + 1] * PEER_N_KEYS + i1[0:8])
        vals.append(v0[8:k] + v1[0:1])
        exps.append(i0[8:k] * PEER_N_KEYS + i1[0:1])
        cand = jnp.where(cvalid, jnp.concatenate(vals, axis=0), -jnp.inf)
        best, _, expert = _topk_rows(cand, cpos, jnp.concatenate(exps, axis=0), k)
        e = jnp.exp(best - jnp.max(best, axis=0, keepdims=True))
        gate_ref[:, cols] = e / jnp.sum(e, axis=0, keepdims=True)
        idx_ref[:, cols] = expert.astype(jnp.int32)


def _route(q3, keys, tb=512):
    nq, n, _ = q3.shape
    heads = nq // 2
    cpos = jnp.broadcast_to(jnp.asarray(_CAND_POS)[:, None], (_N_CAND, LANES))
    cvalid = jnp.broadcast_to(jnp.asarray(_CAND_VALID, F32)[:, None], (_N_CAND, LANES))
    return pl.pallas_call(
        _route_body,
        grid=(n // tb, heads),
        in_specs=[pl.BlockSpec((2, tb, PEER_HALF), lambda i, h: (h, i, 0)),
                  pl.BlockSpec((2, PEER_N_KEYS, PEER_HALF), lambda i, h: (h, 0, 0)),
                  pl.BlockSpec((_N_CAND, LANES), lambda i, h: (0, 0)),
                  pl.BlockSpec((_N_CAND, LANES), lambda i, h: (0, 0))],
        out_specs=[pl.BlockSpec((PEER_TOPK, tb), lambda i, h: (h, i)),
                   pl.BlockSpec((PEER_TOPK, tb), lambda i, h: (h, i))],
        out_shape=[jax.ShapeDtypeStruct((heads * PEER_TOPK, n), jnp.int32),
                   jax.ShapeDtypeStruct((heads * PEER_TOPK, n), F32)],
        compiler_params=pltpu.CompilerParams(dimension_semantics=("parallel", "parallel"),
                                             vmem_limit_bytes=VMEM_LIMIT),
        name="route",
    )(q3, keys, cpos, cvalid)


def _sc_pipeline(idx_v, table_hbm, rows, sems, n_units, units_per_token, compute):
    def start(unit, slot):
        tl = unit // units_per_token
        g = unit % units_per_token
        iv = idx_v[tl, pl.ds(g * SC_LANES, SC_LANES)]
        pltpu.async_copy(table_hbm.at[iv], rows[slot], sems[slot])

    def wait(slot):
        iv = idx_v[0, pl.ds(0, SC_LANES)]
        pltpu.make_async_copy(table_hbm.at[iv], rows[slot], sems[slot]).wait()

    start(0, 0)

    def pair(i, carry):
        u0 = 2 * i
        start(u0 + 1, 1)
        wait(0)
        compute(u0 // units_per_token, u0 % units_per_token, rows[0])

        @pl.when(u0 + 2 < n_units)
        def _():
            start(u0 + 2, 0)

        wait(1)
        compute((u0 + 1) // units_per_token, (u0 + 1) % units_per_token, rows[1])
        return carry

    lax.fori_loop(0, n_units // 2, pair, 0)


def _peer_act(hn, idx, u, tb=8):
    n, d = hn.shape
    kk = idx.shape[1]
    tpw = n // SC_WORKERS
    upt = kk // SC_LANES
    nchunk = d // SC_LANES
    mesh = plsc.VectorSubcoreMesh(core_axis_name="c", subcore_axis_name="s")

    def body(x_hbm, idx_hbm, u_hbm, out_hbm, x_v, idx_v, act_v, rows0, rows1, tmp_v, sem0, sem1):
        wid = lax.axis_index("s") * 2 + lax.axis_index("c")
        lane_row = lax.iota(jnp.int32, SC_LANES) * SC_LANES

        def compute(tl, g, r_ref):
            def cbody(c, accs):
                xv = x_v[tl, pl.ds(c * SC_LANES, SC_LANES)]
                return tuple(accs[r] + r_ref[r, pl.ds(c * SC_LANES, SC_LANES)] * xv
                             for r in range(SC_LANES))
            accs = lax.fori_loop(0, nchunk, cbody,
                                 tuple(jnp.zeros((SC_LANES,), F32) for _ in range(SC_LANES)))
            for r in range(SC_LANES):
                tmp_v[pl.ds(r * SC_LANES, SC_LANES)] = accs[r]
            tot = plsc.load_gather(tmp_v, [lane_row])
            for j in range(1, SC_LANES):
                tot = tot + plsc.load_gather(tmp_v, [lane_row + j])
            act_v[tl, pl.ds(g * SC_LANES, SC_LANES)] = tot

        def batch(b, carry):
            t0 = wid * tpw + b * tb
            pltpu.sync_copy(x_hbm.at[pl.ds(t0, tb)], x_v)
            pltpu.sync_copy(idx_hbm.at[pl.ds(t0, tb)], idx_v)
            _sc_pipeline(idx_v, u_hbm, (rows0, rows1), (sem0, sem1), tb * upt, upt, compute)
            pltpu.sync_copy(act_v, out_hbm.at[pl.ds(t0, tb)])
            return carry

        lax.fori_loop(0, tpw // tb, batch, 0)

    return pl.kernel(
        body, mesh=mesh,
        compiler_params=pltpu.CompilerParams(needs_layout_passes=False),
        out_type=jax.ShapeDtypeStruct((n, kk), F32),
        scratch_types=[pltpu.VMEM((tb, d), F32), pltpu.VMEM((tb, kk), jnp.int32),
                       pltpu.VMEM((tb, kk), F32),
                       pltpu.VMEM((SC_LANES, d), F32), pltpu.VMEM((SC_LANES, d), F32),
                       pltpu.VMEM((SC_LANES * SC_LANES,), F32),
                       pltpu.SemaphoreType.DMA, pltpu.SemaphoreType.DMA],
        name="peer_act",
    )(hn, idx, u)


def _peer_out(w, idx, v, tb=8, cb=16):
    n, kk = w.shape
    d = v.shape[1]
    tpw = n // SC_WORKERS
    upt = kk // SC_LANES
    nchunk = d // SC_LANES
    mesh = plsc.VectorSubcoreMesh(core_axis_name="c", subcore_axis_name="s")

    def body(w_hbm, idx_hbm, v_hbm, out_hbm, w_v, idx_v, out_v, rows0, rows1, sem0, sem1):
        wid = lax.axis_index("s") * 2 + lax.axis_index("c")
        zero = jnp.zeros((SC_LANES,), F32)

        def compute(tl, g, r_ref):
            tls = jnp.full((SC_LANES,), tl, jnp.int32)
            ws = [plsc.load_gather(w_v, [tls, jnp.full((SC_LANES,), g * SC_LANES + r, jnp.int32)])
                  for r in range(SC_LANES)]

            def cb_body(ci, carry):
                c0 = ci * (cb * SC_LANES)
                accs = [out_v[tl, pl.ds(c0 + j * SC_LANES, SC_LANES)] for j in range(cb)]
                for r in range(SC_LANES):
                    for j in range(cb):
                        accs[j] = accs[j] + ws[r] * r_ref[r, pl.ds(c0 + j * SC_LANES, SC_LANES)]
                for j in range(cb):
                    out_v[tl, pl.ds(c0 + j * SC_LANES, SC_LANES)] = accs[j]
                return carry

            lax.fori_loop(0, nchunk // cb, cb_body, 0)

        def batch(b, carry):
            t0 = wid * tpw + b * tb
            pltpu.sync_copy(w_hbm.at[pl.ds(t0, tb)], w_v)
            pltpu.sync_copy(idx_hbm.at[pl.ds(t0, tb)], idx_v)

            def zbody(i, carry):
                out_v[i // nchunk, pl.ds((i % nchunk) * SC_LANES, SC_LANES)] = zero
                return carry

            lax.fori_loop(0, tb * nchunk, zbody, 0)
            _sc_pipeline(idx_v, v_hbm, (rows0, rows1), (sem0, sem1), tb * upt, upt, compute)
            pltpu.sync_copy(out_v, out_hbm.at[pl.ds(t0, tb)])
            return carry

        lax.fori_loop(0, tpw // tb, batch, 0)

    return pl.kernel(
        body, mesh=mesh,
        compiler_params=pltpu.CompilerParams(needs_layout_passes=False),
        out_type=jax.ShapeDtypeStruct((n, d), F32),
        scratch_types=[pltpu.VMEM((tb, kk), F32), pltpu.VMEM((tb, kk), jnp.int32),
                       pltpu.VMEM((tb, d), F32),
                       pltpu.VMEM((SC_LANES, d), F32), pltpu.VMEM((SC_LANES, d), F32),
                       pltpu.SemaphoreType.DMA, pltpu.SemaphoreType.DMA],
        name="peer_out",
    )(w, idx, v)


def _gelu_gate_body(a_ref, g_ref, o_ref):
    a = a_ref[...]
    o_ref[...] = g_ref[...] * (a * (lax.erf(a / np.sqrt(2.0).astype(np.float32)) + 1.0) / 2.0)


def _gelu_gate(act, gate, tm=1024):
    n, kk = act.shape
    assert n % tm == 0
    spec = pl.BlockSpec((tm, kk), lambda i: (i, 0))
    return pl.pallas_call(
        _gelu_gate_body, grid=(n // tm,), in_specs=[spec, spec], out_specs=spec,
        out_shape=jax.ShapeDtypeStruct((n, kk), F32),
        compiler_params=pltpu.CompilerParams(dimension_semantics=("parallel",)),
        name="gelu_gate",
    )(act, gate)


def _final_body(h_ref, p_ref, g_ref, o_ref):
    h = h_ref[...] + p_ref[...]
    o_ref[...] = h * lax.rsqrt(jnp.mean(h * h, axis=-1, keepdims=True) + EPS) * g_ref[...]


def _final(h1, peer, g, tm=512):
    n, d = h1.shape
    spec = pl.BlockSpec((tm, d), lambda i: (i, 0))
    return pl.pallas_call(
        _final_body, grid=(n // tm,),
        in_specs=[spec, spec, pl.BlockSpec((1, d), lambda i: (0, 0))], out_specs=spec,
        out_shape=jax.ShapeDtypeStruct((n, d), F32),
        compiler_params=pltpu.CompilerParams(dimension_semantics=("parallel",)),
        name="final_norm",
    )(h1, peer, g)


def kernel(x, norm1_g, w_in, hg_lower_logits, hg_norm_g, gla_w_gate_up, gla_b_gate, gla_norm_g,
           w_out, norm2_g, peer_w_q, peer_sub_keys, peer_u, peer_v, norm_f_g):
    batch, seq, d = x.shape
    depth = w_in.shape[0]
    assert depth == 1, "single-layer block"
    n = batch * seq
    layer = 0
    h = x.reshape(n, d)

    w = w_in[layer]
    pad = LANES - GLA_GATE_RANK
    w_r = jnp.concatenate(
        [w[:, :3072], w[:, 3072:3072 + GLA_GATE_RANK], jnp.zeros((d, pad), w.dtype),
         w[:, 3072 + GLA_GATE_RANK:]], axis=1).astype(BF16)
    wgu = jnp.concatenate([gla_w_gate_up[layer], jnp.zeros((pad, GLA_KEY), F32)], axis=0).astype(BF16)

    proj = _inproj(h, norm1_g[layer].reshape(1, d), w_r)
    mixed = _mixers(proj, batch, seq, hg_lower_logits, hg_norm_g[layer].reshape(1, HEAD_V), wgu,
                    gla_b_gate[layer].reshape(1, GLA_KEY), gla_norm_g[layer].reshape(1, HEAD_V))
    h1, hn, q3 = _outproj(h, mixed, w_out[layer].astype(BF16), norm2_g[layer].reshape(1, d),
                          peer_w_q[layer].astype(BF16))
    keys = peer_sub_keys[layer].reshape(PEER_HEADS * 2, PEER_N_KEYS, PEER_HALF).astype(BF16)
    idx_t, gate_t = _route(q3, keys)
    idx = idx_t.T
    act = _peer_act(hn, idx, peer_u[layer])
    wts = _gelu_gate(act, gate_t.T)
    peer = _peer_out(wts, idx, peer_v[layer])
    out = _final(h1, peer, norm_f_g.reshape(1, d))
    return out.reshape(batch, seq, d)
```

```python
import functools

import jax
import jax.numpy as jnp
import numpy as np
from jax import lax
from jax.experimental import pallas as pl
from jax.experimental.pallas import tpu as pltpu
from jax.experimental.pallas import tpu_sc as plsc

F32 = jnp.float32
BF16 = jnp.bfloat16
EPS = 1e-6

D_MODEL = 1024
HG_HEADS = 4
GLA_HEADS = 4
HEAD_V = 128
HG_KEY = 512
GLA_KEY = 256
GLA_HEAD_K = 64
GLA_GATE_RANK = 16
GLA_GATE_NORMALIZER = 16.0
CHUNK = 64
LANES = 128
PEER_HEADS = 8
PEER_N_KEYS = 128
PEER_HALF = 128
PEER_TOPK = 16

OFF_HQ, OFF_HF, OFF_HI, OFF_HGATE = 0, 512, 1024, 1536
OFF_GQ, OFF_GK, OFF_GV, OFF_GLOW, OFF_GGATE = 2048, 2304, 2560, 3072, 3200
IN_WIDTH_PADDED = 3712

VMEM_LIMIT = 48 * 1024 * 1024

SC_LANES = 16
SC_WORKERS = 32
PIPELINE_GROUPS = 4


def _nt(a, b):
    return lax.dot_general(a, b, (((1,), (1,)), ((), ())), preferred_element_type=F32)


def _tn(a, b):
    return lax.dot_general(a, b, (((0,), (0,)), ((), ())), preferred_element_type=F32)


def _inproj_body(x_ref, g_ref, w_ref, o_ref):
    x = x_ref[...]
    xn = x * lax.rsqrt(jnp.mean(x * x, axis=-1, keepdims=True) + EPS) * g_ref[...]
    o_ref[...] = jnp.dot(xn.astype(BF16), w_ref[...], preferred_element_type=F32)


def _inproj(x2, g, w, tm=256):
    n, d = x2.shape
    wd = w.shape[1]
    return pl.pallas_call(
        _inproj_body,
        grid=(n // tm,),
        in_specs=[pl.BlockSpec((tm, d), lambda i: (i, 0)),
                  pl.BlockSpec((1, d), lambda i: (0, 0)),
                  pl.BlockSpec((d, wd), lambda i: (0, 0))],
        out_specs=pl.BlockSpec((tm, wd), lambda i: (i, 0)),
        out_shape=jax.ShapeDtypeStruct((n, wd), F32),
        compiler_params=pltpu.CompilerParams(dimension_semantics=("parallel",),
                                             vmem_limit_bytes=VMEM_LIMIT),
        name="inproj",
    )(x2, g, w)


def _level_constants():
    c = CHUNK
    mats = [np.tril(np.ones((c, c), np.float32))]
    level = np.full((c, c), -1, np.int32)
    b, lvl = c // 2, 0
    while b >= 1:
        m = np.zeros((c, c), np.float32)
        for s in range(0, c, 2 * b):
            mid = s + b
            for i in range(mid, s + 2 * b):
                m[i, mid:i + 1] = 1.0
                level[i, s:mid] = lvl
            for j in range(s, mid):
                m[j, j + 1:mid] = 1.0
        mats.append(m)
        b //= 2
        lvl += 1
    level[np.arange(c), np.arange(c)] = lvl
    return np.concatenate(mats, axis=0), level, lvl


_SEG_MATS, _LEVEL_MAP, _N_LEVELS = _level_constants()


def _split3(a):
    hi = a.astype(BF16)
    r = a - hi.astype(F32)
    mid = r.astype(BF16)
    lo = (r - mid.astype(F32)).astype(BF16)
    return jnp.concatenate([hi, mid, lo], axis=1)


def _gla_chunk(q, k, g, v, st, seg, level):
    c = CHUNK
    ex3 = jnp.dot(seg, _split3(g), preferred_element_type=F32)
    ex = ex3[:, 0:LANES] + ex3[:, LANES:2 * LANES] + ex3[:, 2 * LANES:3 * LANES]
    cum = ex[0:c]
    scores = jnp.where(level == _N_LEVELS, _nt(q.astype(BF16), k.astype(BF16)), 0.0)
    for l in range(_N_LEVELS):
        e = jnp.exp(ex[c * (l + 1):c * (l + 2)])
        p = _nt((q * e).astype(BF16), (k * e).astype(BF16))
        scores = jnp.where(level == l, p, scores)
    last = cum[c - 1:c, :]
    qd = (q * jnp.exp(cum)).astype(BF16)
    kd = (k * jnp.exp(last - cum)).astype(BF16)
    vb = v.astype(BF16)
    o = jnp.dot(scores.astype(BF16), vb, preferred_element_type=F32) + _nt(qd, st.astype(BF16))
    st_new = st * jnp.exp(last) + _tn(vb, kd)
    return o, st_new


def _head_out(o, gain, gate):
    on = o * lax.rsqrt(jnp.mean(o * o, axis=-1, keepdims=True) + EPS) * gain
    return on * (gate * jax.nn.sigmoid(gate))


def _mixer_body(p_ref, seg_ref, level_ref, lbl_ref, hgn_ref, wgu_ref, bg_ref, ggn_ref,
                o_ref, st_ref, *, chunks):
    @pl.when(pl.program_id(1) == 0)
    def _():
        st_ref[...] = jnp.zeros_like(st_ref)

    seg = seg_ref[...]
    level = level_ref[...]
    logits = lbl_ref[...]
    ez = jnp.exp(logits - jnp.max(logits, axis=0, keepdims=True))
    lb = ez[0:1, :] / jnp.sum(ez, axis=0, keepdims=True)
    lane = lax.broadcasted_iota(jnp.int32, (CHUNK, LANES), 1)

    def chunk_body(ci, carry):
        rows = pl.ds(pl.multiple_of(ci * CHUNK, CHUNK), CHUNK)

        def col(off):
            return p_ref[rows, pl.ds(off, LANES)]

        for h in range(HG_HEADS):
            hq = col(OFF_HQ + h * LANES)
            q = hq * jax.nn.sigmoid(hq)
            lbh = lb[:, h * LANES:(h + 1) * LANES]
            forget = lbh + (1.0 - lbh) * jax.nn.sigmoid(col(OFF_HF + h * LANES))
            o, st = _gla_chunk(q, 1.0 - forget, jnp.log(forget), col(OFF_HI + h * LANES),
                               st_ref[h], seg, level)
            st_ref[h] = st
            o_ref[rows, pl.ds(h * HEAD_V, HEAD_V)] = _head_out(
                o, hgn_ref[...], col(OFF_HGATE + h * HEAD_V)).astype(o_ref.dtype)

        zg = jnp.dot(col(OFF_GLOW).astype(BF16), wgu_ref[...], preferred_element_type=F32) + bg_ref[...]
        log_g = (jnp.minimum(zg, 0.0) - jnp.log(1.0 + jnp.exp(-jnp.abs(zg)))) / GLA_GATE_NORMALIZER
        for h in range(GLA_HEADS):
            pair, half = h // 2, h % 2
            q = col(OFF_GQ + pair * LANES) * (GLA_HEAD_K ** -0.5)
            in_head = (lane >= half * GLA_HEAD_K) & (lane < (half + 1) * GLA_HEAD_K)
            k = jnp.where(in_head, col(OFF_GK + pair * LANES), 0.0)
            g = log_g[:, pair * LANES:(pair + 1) * LANES]
            o, st = _gla_chunk(q, k, g, col(OFF_GV + h * HEAD_V), st_ref[HG_HEADS + h], seg, level)
            st_ref[HG_HEADS + h] = st
            o_ref[rows, pl.ds((HG_HEADS + h) * HEAD_V, HEAD_V)] = _head_out(
                o, ggn_ref[...], col(OFF_GGATE + h * HEAD_V)).astype(o_ref.dtype)
        return carry

    lax.fori_loop(0, chunks, chunk_body, 0)


def _mixers(proj, batch, seq, lb_logits, hg_norm_g, wgu, bg, gla_norm_g, tt=512):
    n = batch * seq
    steps = seq // tt
    heads = HG_HEADS + GLA_HEADS
    const = lambda shape: pl.BlockSpec(shape, lambda b, t: (0,) * len(shape))
    return pl.pallas_call(
        functools.partial(_mixer_body, chunks=tt // CHUNK),
        grid=(batch, steps),
        in_specs=[pl.BlockSpec((tt, IN_WIDTH_PADDED), lambda b, t: (b * steps + t, 0)),
                  const(_SEG_MATS.shape), const(_LEVEL_MAP.shape),
                  const(lb_logits.shape), const((1, HEAD_V)),
                  const(wgu.shape), const(bg.shape), const((1, HEAD_V))],
        out_specs=pl.BlockSpec((tt, heads * HEAD_V), lambda b, t: (b * steps + t, 0)),
        out_shape=jax.ShapeDtypeStruct((n, heads * HEAD_V), BF16),
        scratch_shapes=[pltpu.VMEM((heads, HEAD_V, LANES), F32)],
        compiler_params=pltpu.CompilerParams(dimension_semantics=("parallel", "arbitrary"),
                                             vmem_limit_bytes=VMEM_LIMIT),
        name="mixers",
    )(proj, jnp.asarray(_SEG_MATS, BF16), jnp.asarray(_LEVEL_MAP), lb_logits, hg_norm_g,
      wgu, bg, gla_norm_g)


def _outproj_body(x_ref, m_ref, wo_ref, g_ref, wq_ref, h_ref, hn_ref, q_ref):
    h = x_ref[...] + jnp.dot(m_ref[...], wo_ref[...], preferred_element_type=F32)
    h_ref[...] = h
    hn = h * lax.rsqrt(jnp.mean(h * h, axis=-1, keepdims=True) + EPS) * g_ref[...]
    hn_ref[...] = hn
    q = jnp.dot(hn.astype(BF16), wq_ref[...], preferred_element_type=F32)
    for j in range(q_ref.shape[0]):
        q_ref[j] = q[:, j * PEER_HALF:(j + 1) * PEER_HALF]


def _outproj(x2, mixed, wo, g2, wq, tm=256):
    n, d = x2.shape
    nq = wq.shape[1] // PEER_HALF
    return pl.pallas_call(
        _outproj_body,
        grid=(n // tm,),
        in_specs=[pl.BlockSpec((tm, d), lambda i: (i, 0)),
                  pl.BlockSpec((tm, mixed.shape[1]), lambda i: (i, 0)),
                  pl.BlockSpec(wo.shape, lambda i: (0, 0)),
                  pl.BlockSpec((1, d), lambda i: (0, 0)),
                  pl.BlockSpec(wq.shape, lambda i: (0, 0))],
        out_specs=[pl.BlockSpec((tm, d), lambda i: (i, 0)),
                   pl.BlockSpec((tm, d), lambda i: (i, 0)),
                   pl.BlockSpec((nq, tm, PEER_HALF), lambda i: (0, i, 0))],
        out_shape=[jax.ShapeDtypeStruct((n, d), F32),
                   jax.ShapeDtypeStruct((n, d), F32),
                   jax.ShapeDtypeStruct((nq, n, PEER_HALF), F32)],
        compiler_params=pltpu.CompilerParams(dimension_semantics=("parallel",),
                                             vmem_limit_bytes=VMEM_LIMIT),
        name="outproj",
    )(x2, mixed, wo, g2, wq)


def _candidate_constants():
    k = PEER_TOPK
    pos, valid = [], []
    for j in range(k):
        pos.append(j); valid.append(True)
    for i in range(1, 8):
        for j in range(8):
            pos.append(i * k + j); valid.append((i + 1) * (j + 1) <= k)
    for i in range(8, k):
        pos.append(i * k); valid.append(True)
    return np.asarray(pos, np.float32), np.asarray(valid, bool)


_CAND_POS, _CAND_VALID = _candidate_constants()
_N_CAND = _CAND_POS.shape[0]


def _topk_rows(ss, pos, extras, k):
    t = ss[0].shape[1]
    slot = lax.broadcasted_iota(jnp.int32, (k, t), 0)

    def body(it, carry):
        out = []
        for (s, vals, poss, exts), extra in zip(carry, extras):
            m = jnp.max(s, axis=0, keepdims=True)
            p = jnp.min(jnp.where(s == m, pos, 1e9), axis=0, keepdims=True)
            hit = pos == p
            vals = jnp.where(slot == it, m, vals)
            poss = jnp.where(slot == it, p, poss)
            if extra is not None:
                x = jnp.max(jnp.where(hit, extra, -1.0), axis=0, keepdims=True)
                exts = jnp.where(slot == it, x, exts)
            out.append((jnp.where(hit, -jnp.inf, s), vals, poss, exts))
        return tuple(out)

    z = jnp.zeros((k, t), F32)
    res = lax.fori_loop(0, k, body, tuple((s, z, z, z) for s in ss))
    return [(vals, poss, exts) for _, vals, poss, exts in res]


def _route_body(q_ref, keys_ref, cpos_ref, cvalid_ref, idx_ref, gate_ref):
    k = PEER_TOPK
    n_tiles = q_ref.shape[1] // LANES
    key_pos = lax.broadcasted_iota(jnp.int32, (PEER_N_KEYS, LANES), 0).astype(F32)
    cpos = cpos_ref[...]
    cvalid = cvalid_ref[...] > 0.5
    scores = [_nt(keys_ref[p], q_ref[p, pl.ds(lt * LANES, LANES), :].astype(BF16))
              for lt in range(n_tiles) for p in range(2)]
    tops = _topk_rows(scores, key_pos, [None] * len(scores), k)
    cands, experts = [], []
    for lt in range(n_tiles):
        (v0, i0, _), (v1, i1, _) = tops[2 * lt], tops[2 * lt + 1]
        vals = [v0[0:1] + v1]
        exps = [i0[0:1] * PEER_N_KEYS + i1]
        for i in range(1, 8):
            vals.append(v0[i:i + 1] + v1[0:8])
            exps.append(i0[i:i + 1] * PEER_N_KEYS + i1[0:8])
        vals.append(v0[8:k] + v1[0:1])
        exps.append(i0[8:k] * PEER_N_KEYS + i1[0:1])
        cands.append(jnp.where(cvalid, jnp.concatenate(vals, axis=0), -jnp.inf))
        experts.append(jnp.concatenate(exps, axis=0))
    picks = _topk_rows(cands, cpos, experts, k)
    for lt in range(n_tiles):
        best, _, expert = picks[lt]
        cols = pl.ds(lt * LANES, LANES)
        e = jnp.exp(best - jnp.max(best, axis=0, keepdims=True))
        gate_ref[:, cols] = e / jnp.sum(e, axis=0, keepdims=True)
        idx_ref[:, cols] = expert.astype(jnp.int32)


def _route(q3, keys, tb=512):
    nq, n, _ = q3.shape
    heads = nq // 2
    cpos = jnp.broadcast_to(jnp.asarray(_CAND_POS)[:, None], (_N_CAND, LANES))
    cvalid = jnp.broadcast_to(jnp.asarray(_CAND_VALID, F32)[:, None], (_N_CAND, LANES))
    return pl.pallas_call(
        _route_body,
        grid=(n // tb, heads),
        in_specs=[pl.BlockSpec((2, tb, PEER_HALF), lambda i, h: (h, i, 0)),
                  pl.BlockSpec((2, PEER_N_KEYS, PEER_HALF), lambda i, h: (h, 0, 0)),
                  pl.BlockSpec((_N_CAND, LANES), lambda i, h: (0, 0)),
                  pl.BlockSpec((_N_CAND, LANES), lambda i, h: (0, 0))],
        out_specs=[pl.BlockSpec((PEER_TOPK, tb), lambda i, h: (h, i)),
                   pl.BlockSpec((PEER_TOPK, tb), lambda i, h: (h, i))],
        out_shape=[jax.ShapeDtypeStruct((heads * PEER_TOPK, n), jnp.int32),
                   jax.ShapeDtypeStruct((heads * PEER_TOPK, n), F32)],
        compiler_params=pltpu.CompilerParams(dimension_semantics=("parallel", "parallel"),
                                             vmem_limit_bytes=VMEM_LIMIT),
        name="route",
    )(q3, keys, cpos, cvalid)


def _sc_pipeline(idx_v, table_hbm, rows, sems, n_units, units_per_token, compute):
    def start(unit, slot):
        tl = unit // units_per_token
        g = unit % units_per_token
        iv = idx_v[tl, pl.ds(g * SC_LANES, SC_LANES)]
        pltpu.async_copy(table_hbm.at[iv], rows[slot], sems[slot])

    def wait(slot):
        iv = idx_v[0, pl.ds(0, SC_LANES)]
        pltpu.make_async_copy(table_hbm.at[iv], rows[slot], sems[slot]).wait()

    start(0, 0)

    def pair(i, carry):
        u0 = 2 * i
        start(u0 + 1, 1)
        wait(0)
        compute(u0 // units_per_token, u0 % units_per_token, rows[0])

        @pl.when(u0 + 2 < n_units)
        def _():
            start(u0 + 2, 0)

        wait(1)
        compute((u0 + 1) // units_per_token, (u0 + 1) % units_per_token, rows[1])
        return carry

    lax.fori_loop(0, n_units // 2, pair, 0)


def _peer_act(hn, idx, u, tb=8):
    n, d = hn.shape
    kk = idx.shape[1]
    tpw = n // SC_WORKERS
    upt = kk // SC_LANES
    nchunk = d // SC_LANES
    mesh = plsc.VectorSubcoreMesh(core_axis_name="c", subcore_axis_name="s")

    def body(x_hbm, idx_hbm, u_hbm, out_hbm, x_v, idx_v, act_v, rows0, rows1, tmp_v, sem0, sem1):
        wid = lax.axis_index("s") * 2 + lax.axis_index("c")
        lane_row = lax.iota(jnp.int32, SC_LANES) * SC_LANES

        def compute(tl, g, r_ref):
            def cbody(c, accs):
                xv = x_v[tl, pl.ds(c * SC_LANES, SC_LANES)]
                return tuple(accs[r] + r_ref[r, pl.ds(c * SC_LANES, SC_LANES)] * xv
                             for r in range(SC_LANES))
            accs = lax.fori_loop(0, nchunk, cbody,
                                 tuple(jnp.zeros((SC_LANES,), F32) for _ in range(SC_LANES)))
            for r in range(SC_LANES):
                tmp_v[pl.ds(r * SC_LANES, SC_LANES)] = accs[r]
            tot = plsc.load_gather(tmp_v, [lane_row])
            for j in range(1, SC_LANES):
                tot = tot + plsc.load_gather(tmp_v, [lane_row + j])
            act_v[tl, pl.ds(g * SC_LANES, SC_LANES)] = tot

        def batch(b, carry):
            t0 = wid * tpw + b * tb
            pltpu.sync_copy(x_hbm.at[pl.ds(t0, tb)], x_v)
            pltpu.sync_copy(idx_hbm.at[pl.ds(t0, tb)], idx_v)
            _sc_pipeline(idx_v, u_hbm, (rows0, rows1), (sem0, sem1), tb * upt, upt, compute)
            pltpu.sync_copy(act_v, out_hbm.at[pl.ds(t0, tb)])
            return carry

        lax.fori_loop(0, tpw // tb, batch, 0)

    return pl.kernel(
        body, mesh=mesh,
        compiler_params=pltpu.CompilerParams(needs_layout_passes=False),
        out_type=jax.ShapeDtypeStruct((n, kk), F32),
        scratch_types=[pltpu.VMEM((tb, d), F32), pltpu.VMEM((tb, kk), jnp.int32),
                       pltpu.VMEM((tb, kk), F32),
                       pltpu.VMEM((SC_LANES, d), F32), pltpu.VMEM((SC_LANES, d), F32),
                       pltpu.VMEM((SC_LANES * SC_LANES,), F32),
                       pltpu.SemaphoreType.DMA, pltpu.SemaphoreType.DMA],
        name="peer_act",
    )(hn, idx, u)


def _peer_out(w, idx, v, tb=8, cb=16):
    n, kk = w.shape
    d = v.shape[1]
    tpw = n // SC_WORKERS
    upt = kk // SC_LANES
    nchunk = d // SC_LANES
    mesh = plsc.VectorSubcoreMesh(core_axis_name="c", subcore_axis_name="s")

    def body(w_hbm, idx_hbm, v_hbm, out_hbm, w_v, idx_v, out_v, rows0, rows1, sem0, sem1):
        wid = lax.axis_index("s") * 2 + lax.axis_index("c")
        zero = jnp.zeros((SC_LANES,), F32)

        def compute(tl, g, r_ref):
            tls = jnp.full((SC_LANES,), tl, jnp.int32)

            def cb_body(ci, carry):
                c0 = ci * (cb * SC_LANES)
                accs = tuple(out_v[tl, pl.ds(c0 + j * SC_LANES, SC_LANES)] for j in range(cb))

                def rbody(r, accs):
                    wr = plsc.load_gather(w_v, [tls, jnp.full((SC_LANES,), g * SC_LANES + r, jnp.int32)])
                    return tuple(accs[j] + wr * r_ref[r, pl.ds(c0 + j * SC_LANES, SC_LANES)]
                                 for j in range(cb))

                accs = lax.fori_loop(0, SC_LANES, rbody, accs)
                for j in range(cb):
                    out_v[tl, pl.ds(c0 + j * SC_LANES, SC_LANES)] = accs[j]
                return carry

            lax.fori_loop(0, nchunk // cb, cb_body, 0)

        def batch(b, carry):
            t0 = wid * tpw + b * tb
            pltpu.sync_copy(w_hbm.at[pl.ds(t0, tb)], w_v)
            pltpu.sync_copy(idx_hbm.at[pl.ds(t0, tb)], idx_v)

            def zbody(i, carry):
                out_v[i // nchunk, pl.ds((i % nchunk) * SC_LANES, SC_LANES)] = zero
                return carry

            lax.fori_loop(0, tb * nchunk, zbody, 0)
            _sc_pipeline(idx_v, v_hbm, (rows0, rows1), (sem0, sem1), tb * upt, upt, compute)
            pltpu.sync_copy(out_v, out_hbm.at[pl.ds(t0, tb)])
            return carry

        lax.fori_loop(0, tpw // tb, batch, 0)

    return pl.kernel(
        body, mesh=mesh,
        compiler_params=pltpu.CompilerParams(needs_layout_passes=False),
        out_type=jax.ShapeDtypeStruct((n, d), F32),
        scratch_types=[pltpu.VMEM((tb, kk), F32), pltpu.VMEM((tb, kk), jnp.int32),
                       pltpu.VMEM((tb, d), F32),
                       pltpu.VMEM((SC_LANES, d), F32), pltpu.VMEM((SC_LANES, d), F32),
                       pltpu.SemaphoreType.DMA, pltpu.SemaphoreType.DMA],
        name="peer_out",
    )(w, idx, v)


def _gelu_gate_body(a_ref, g_ref, o_ref):
    a = a_ref[...]
    o_ref[...] = g_ref[...] * (a * (lax.erf(a / np.sqrt(2.0).astype(np.float32)) + 1.0) / 2.0)


def _gelu_gate(act, gate, tm=1024):
    n, kk = act.shape
    tm = min(tm, n)
    assert n % tm == 0
    spec = pl.BlockSpec((tm, kk), lambda i: (i, 0))
    return pl.pallas_call(
        _gelu_gate_body, grid=(n // tm,), in_specs=[spec, spec], out_specs=spec,
        out_shape=jax.ShapeDtypeStruct((n, kk), F32),
        compiler_params=pltpu.CompilerParams(dimension_semantics=("parallel",)),
        name="gelu_gate",
    )(act, gate)


def _final_body(h_ref, p_ref, g_ref, o_ref):
    h = h_ref[...] + p_ref[...]
    o_ref[...] = h * lax.rsqrt(jnp.mean(h * h, axis=-1, keepdims=True) + EPS) * g_ref[...]


def _final(h1, peer, g, tm=512):
    n, d = h1.shape
    spec = pl.BlockSpec((tm, d), lambda i: (i, 0))
    return pl.pallas_call(
        _final_body, grid=(n // tm,),
        in_specs=[spec, spec, pl.BlockSpec((1, d), lambda i: (0, 0))], out_specs=spec,
        out_shape=jax.ShapeDtypeStruct((n, d), F32),
        compiler_params=pltpu.CompilerParams(dimension_semantics=("parallel",)),
        name="final_norm",
    )(h1, peer, g)


def kernel(x, norm1_g, w_in, hg_lower_logits, hg_norm_g, gla_w_gate_up, gla_b_gate, gla_norm_g,
           w_out, norm2_g, peer_w_q, peer_sub_keys, peer_u, peer_v, norm_f_g):
    batch, seq, d = x.shape
    depth = w_in.shape[0]
    assert depth == 1, "single-layer block"
    layer = 0

    w = w_in[layer]
    pad = LANES - GLA_GATE_RANK
    w_r = jnp.concatenate(
        [w[:, :3072], w[:, 3072:3072 + GLA_GATE_RANK], jnp.zeros((d, pad), w.dtype),
         w[:, 3072 + GLA_GATE_RANK:]], axis=1).astype(BF16)
    wgu = jnp.concatenate([gla_w_gate_up[layer], jnp.zeros((pad, GLA_KEY), F32)], axis=0).astype(BF16)

    wo = w_out[layer].astype(BF16)
    wq = peer_w_q[layer].astype(BF16)
    keys = peer_sub_keys[layer].reshape(PEER_HEADS * 2, PEER_N_KEYS, PEER_HALF).astype(BF16)

    groups = PIPELINE_GROUPS if batch % PIPELINE_GROUPS == 0 else 1
    bpg = batch // groups
    outs = []
    for gi in range(groups):
        h = x[gi * bpg:(gi + 1) * bpg].reshape(bpg * seq, d)
        proj = _inproj(h, norm1_g[layer].reshape(1, d), w_r)
        mixed = _mixers(proj, bpg, seq, hg_lower_logits, hg_norm_g[layer].reshape(1, HEAD_V), wgu,
                        gla_b_gate[layer].reshape(1, GLA_KEY), gla_norm_g[layer].reshape(1, HEAD_V))
        h1, hn, q3 = _outproj(h, mixed, wo, norm2_g[layer].reshape(1, d), wq)
        idx_t, gate_t = _route(q3, keys)
        idx = idx_t.T
        act = _peer_act(hn, idx, peer_u[layer])
        wts = _gelu_gate(act, gate_t.T)
        peer = _peer_out(wts, idx, peer_v[layer])
        outs.append(_final(h1, peer, norm_f_g.reshape(1, d)))
    return jnp.concatenate(outs, axis=0).reshape(batch, seq, d)
```

```python
import functools

import jax
import jax.numpy as jnp
import numpy as np
from jax import lax
from jax.experimental import pallas as pl
from jax.experimental.pallas import tpu as pltpu
from jax.experimental.pallas import tpu_sc as plsc

F32 = jnp.float32
BF16 = jnp.bfloat16
EPS = 1e-6

D_MODEL = 1024
HG_HEADS = 4
GLA_HEADS = 4
HEAD_V = 128
HG_KEY = 512
GLA_KEY = 256
GLA_HEAD_K = 64
GLA_GATE_RANK = 16
GLA_GATE_NORMALIZER = 16.0
CHUNK = 64
LANES = 128
PEER_HEADS = 8
PEER_N_KEYS = 128
PEER_HALF = 128
PEER_TOPK = 16

OFF_HQ, OFF_HF, OFF_HI, OFF_HGATE = 0, 512, 1024, 1536
OFF_GQ, OFF_GK, OFF_GV, OFF_GLOW, OFF_GGATE = 2048, 2304, 2560, 3072, 3200
IN_WIDTH_PADDED = 3712

VMEM_LIMIT = 48 * 1024 * 1024

SC_LANES = 16
SC_WORKERS = 32
PIPELINE_GROUPS = 4
TC_SHARE = 0.1875
SHARE_GRANULE = 256


def _nt(a, b):
    return lax.dot_general(a, b, (((1,), (1,)), ((), ())), preferred_element_type=F32)


def _tn(a, b):
    return lax.dot_general(a, b, (((0,), (0,)), ((), ())), preferred_element_type=F32)


def _inproj_body(x_ref, g_ref, w_ref, o_ref):
    x = x_ref[...]
    xn = x * lax.rsqrt(jnp.mean(x * x, axis=-1, keepdims=True) + EPS) * g_ref[...]
    o_ref[...] = jnp.dot(xn.astype(BF16), w_ref[...], preferred_element_type=F32)


def _inproj(x2, g, w, tm=256):
    n, d = x2.shape
    wd = w.shape[1]
    return pl.pallas_call(
        _inproj_body,
        grid=(n // tm,),
        in_specs=[pl.BlockSpec((tm, d), lambda i: (i, 0)),
                  pl.BlockSpec((1, d), lambda i: (0, 0)),
                  pl.BlockSpec((d, wd), lambda i: (0, 0))],
        out_specs=pl.BlockSpec((tm, wd), lambda i: (i, 0)),
        out_shape=jax.ShapeDtypeStruct((n, wd), F32),
        compiler_params=pltpu.CompilerParams(dimension_semantics=("parallel",),
                                             vmem_limit_bytes=VMEM_LIMIT),
        name="inproj",
    )(x2, g, w)


def _level_constants():
    c = CHUNK
    mats = [np.tril(np.ones((c, c), np.float32))]
    level = np.full((c, c), -1, np.int32)
    b, lvl = c // 2, 0
    while b >= 1:
        m = np.zeros((c, c), np.float32)
        for s in range(0, c, 2 * b):
            mid = s + b
            for i in range(mid, s + 2 * b):
                m[i, mid:i + 1] = 1.0
                level[i, s:mid] = lvl
            for j in range(s, mid):
                m[j, j + 1:mid] = 1.0
        mats.append(m)
        b //= 2
        lvl += 1
    level[np.arange(c), np.arange(c)] = lvl
    return np.concatenate(mats, axis=0), level, lvl


_SEG_MATS, _LEVEL_MAP, _N_LEVELS = _level_constants()


def _split3(a):
    hi = a.astype(BF16)
    r = a - hi.astype(F32)
    mid = r.astype(BF16)
    lo = (r - mid.astype(F32)).astype(BF16)
    return jnp.concatenate([hi, mid, lo], axis=1)


def _gla_chunk(q, k, g, v, st, seg, level):
    c = CHUNK
    ex3 = jnp.dot(seg, _split3(g), preferred_element_type=F32)
    ex = ex3[:, 0:LANES] + ex3[:, LANES:2 * LANES] + ex3[:, 2 * LANES:3 * LANES]
    cum = ex[0:c]
    scores = jnp.where(level == _N_LEVELS, _nt(q.astype(BF16), k.astype(BF16)), 0.0)
    for l in range(_N_LEVELS):
        e = jnp.exp(ex[c * (l + 1):c * (l + 2)])
        p = _nt((q * e).astype(BF16), (k * e).astype(BF16))
        scores = jnp.where(level == l, p, scores)
    last = cum[c - 1:c, :]
    qd = (q * jnp.exp(cum)).astype(BF16)
    kd = (k * jnp.exp(last - cum)).astype(BF16)
    vb = v.astype(BF16)
    o = jnp.dot(scores.astype(BF16), vb, preferred_element_type=F32) + _nt(qd, st.astype(BF16))
    st_new = st * jnp.exp(last) + _tn(vb, kd)
    return o, st_new


def _head_out(o, gain, gate):
    on = o * lax.rsqrt(jnp.mean(o * o, axis=-1, keepdims=True) + EPS) * gain
    return on * (gate * jax.nn.sigmoid(gate))


def _mixer_body(p_ref, seg_ref, level_ref, lbl_ref, hgn_ref, wgu_ref, bg_ref, ggn_ref,
                o_ref, st_ref, *, chunks):
    @pl.when(pl.program_id(1) == 0)
    def _():
        st_ref[...] = jnp.zeros_like(st_ref)

    seg = seg_ref[...]
    level = level_ref[...]
    logits = lbl_ref[...]
    ez = jnp.exp(logits - jnp.max(logits, axis=0, keepdims=True))
    lb = ez[0:1, :] / jnp.sum(ez, axis=0, keepdims=True)
    lane = lax.broadcasted_iota(jnp.int32, (CHUNK, LANES), 1)

    def chunk_body(ci, carry):
        rows = pl.ds(pl.multiple_of(ci * CHUNK, CHUNK), CHUNK)

        def col(off):
            return p_ref[rows, pl.ds(off, LANES)]

        for h in range(HG_HEADS):
            hq = col(OFF_HQ + h * LANES)
            q = hq * jax.nn.sigmoid(hq)
            lbh = lb[:, h * LANES:(h + 1) * LANES]
            forget = lbh + (1.0 - lbh) * jax.nn.sigmoid(col(OFF_HF + h * LANES))
            o, st = _gla_chunk(q, 1.0 - forget, jnp.log(forget), col(OFF_HI + h * LANES),
                               st_ref[h], seg, level)
            st_ref[h] = st
            o_ref[rows, pl.ds(h * HEAD_V, HEAD_V)] = _head_out(
                o, hgn_ref[...], col(OFF_HGATE + h * HEAD_V)).astype(o_ref.dtype)

        zg = jnp.dot(col(OFF_GLOW).astype(BF16), wgu_ref[...], preferred_element_type=F32) + bg_ref[...]
        log_g = (jnp.minimum(zg, 0.0) - jnp.log(1.0 + jnp.exp(-jnp.abs(zg)))) / GLA_GATE_NORMALIZER
        for h in range(GLA_HEADS):
            pair, half = h // 2, h % 2
            q = col(OFF_GQ + pair * LANES) * (GLA_HEAD_K ** -0.5)
            in_head = (lane >= half * GLA_HEAD_K) & (lane < (half + 1) * GLA_HEAD_K)
            k = jnp.where(in_head, col(OFF_GK + pair * LANES), 0.0)
            g = log_g[:, pair * LANES:(pair + 1) * LANES]
            o, st = _gla_chunk(q, k, g, col(OFF_GV + h * HEAD_V), st_ref[HG_HEADS + h], seg, level)
            st_ref[HG_HEADS + h] = st
            o_ref[rows, pl.ds((HG_HEADS + h) * HEAD_V, HEAD_V)] = _head_out(
                o, ggn_ref[...], col(OFF_GGATE + h * HEAD_V)).astype(o_ref.dtype)
        return carry

    lax.fori_loop(0, chunks, chunk_body, 0)


def _mixers(proj, batch, seq, lb_logits, hg_norm_g, wgu, bg, gla_norm_g, tt=512):
    n = batch * seq
    steps = seq // tt
    heads = HG_HEADS + GLA_HEADS
    const = lambda shape: pl.BlockSpec(shape, lambda b, t: (0,) * len(shape))
    return pl.pallas_call(
        functools.partial(_mixer_body, chunks=tt // CHUNK),
        grid=(batch, steps),
        in_specs=[pl.BlockSpec((tt, IN_WIDTH_PADDED), lambda b, t: (b * steps + t, 0)),
                  const(_SEG_MATS.shape), const(_LEVEL_MAP.shape),
                  const(lb_logits.shape), const((1, HEAD_V)),
                  const(wgu.shape), const(bg.shape), const((1, HEAD_V))],
        out_specs=pl.BlockSpec((tt, heads * HEAD_V), lambda b, t: (b * steps + t, 0)),
        out_shape=jax.ShapeDtypeStruct((n, heads * HEAD_V), BF16),
        scratch_shapes=[pltpu.VMEM((heads, HEAD_V, LANES), F32)],
        compiler_params=pltpu.CompilerParams(dimension_semantics=("parallel", "arbitrary"),
                                             vmem_limit_bytes=VMEM_LIMIT),
        name="mixers",
    )(proj, jnp.asarray(_SEG_MATS, BF16), jnp.asarray(_LEVEL_MAP), lb_logits, hg_norm_g,
      wgu, bg, gla_norm_g)


def _outproj_body(x_ref, m_ref, wo_ref, g_ref, wq_ref, h_ref, hn_ref, q_ref):
    h = x_ref[...] + jnp.dot(m_ref[...], wo_ref[...], preferred_element_type=F32)
    h_ref[...] = h
    hn = h * lax.rsqrt(jnp.mean(h * h, axis=-1, keepdims=True) + EPS) * g_ref[...]
    hn_ref[...] = hn
    q = jnp.dot(hn.astype(BF16), wq_ref[...], preferred_element_type=F32)
    for j in range(q_ref.shape[0]):
        q_ref[j] = q[:, j * PEER_HALF:(j + 1) * PEER_HALF]


def _outproj(x2, mixed, wo, g2, wq, tm=256):
    n, d = x2.shape
    nq = wq.shape[1] // PEER_HALF
    return pl.pallas_call(
        _outproj_body,
        grid=(n // tm,),
        in_specs=[pl.BlockSpec((tm, d), lambda i: (i, 0)),
                  pl.BlockSpec((tm, mixed.shape[1]), lambda i: (i, 0)),
                  pl.BlockSpec(wo.shape, lambda i: (0, 0)),
                  pl.BlockSpec((1, d), lambda i: (0, 0)),
                  pl.BlockSpec(wq.shape, lambda i: (0, 0))],
        out_specs=[pl.BlockSpec((tm, d), lambda i: (i, 0)),
                   pl.BlockSpec((tm, d), lambda i: (i, 0)),
                   pl.BlockSpec((nq, tm, PEER_HALF), lambda i: (0, i, 0))],
        out_shape=[jax.ShapeDtypeStruct((n, d), F32),
                   jax.ShapeDtypeStruct((n, d), F32),
                   jax.ShapeDtypeStruct((nq, n, PEER_HALF), F32)],
        compiler_params=pltpu.CompilerParams(dimension_semantics=("parallel",),
                                             vmem_limit_bytes=VMEM_LIMIT),
        name="outproj",
    )(x2, mixed, wo, g2, wq)


def _candidate_constants():
    k = PEER_TOPK
    pos, valid = [], []
    for j in range(k):
        pos.append(j); valid.append(True)
    for i in range(1, 8):
        for j in range(8):
            pos.append(i * k + j); valid.append((i + 1) * (j + 1) <= k)
    for i in range(8, k):
        pos.append(i * k); valid.append(True)
    return np.asarray(pos, np.float32), np.asarray(valid, bool)


_CAND_POS, _CAND_VALID = _candidate_constants()
_N_CAND = _CAND_POS.shape[0]


def _topk_rows(ss, pos, extras, k):
    t = ss[0].shape[1]
    slot = lax.broadcasted_iota(jnp.int32, (k, t), 0)

    def body(it, carry):
        out = []
        for (s, vals, poss, exts), extra in zip(carry, extras):
            m = jnp.max(s, axis=0, keepdims=True)
            p = jnp.min(jnp.where(s == m, pos, 1e9), axis=0, keepdims=True)
            hit = pos == p
            vals = jnp.where(slot == it, m, vals)
            poss = jnp.where(slot == it, p, poss)
            if extra is not None:
                x = jnp.max(jnp.where(hit, extra, -1.0), axis=0, keepdims=True)
                exts = jnp.where(slot == it, x, exts)
            out.append((jnp.where(hit, -jnp.inf, s), vals, poss, exts))
        return tuple(out)

    z = jnp.zeros((k, t), F32)
    res = lax.fori_loop(0, k, body, tuple((s, z, z, z) for s in ss))
    return [(vals, poss, exts) for _, vals, poss, exts in res]


def _route_body(q_ref, keys_ref, cpos_ref, cvalid_ref, idx_ref, gate_ref):
    k = PEER_TOPK
    n_tiles = q_ref.shape[1] // LANES
    key_pos = lax.broadcasted_iota(jnp.int32, (PEER_N_KEYS, LANES), 0).astype(F32)
    cpos = cpos_ref[...]
    cvalid = cvalid_ref[...] > 0.5
    scores = [_nt(keys_ref[p], q_ref[p, pl.ds(lt * LANES, LANES), :].astype(BF16))
              for lt in range(n_tiles) for p in range(2)]
    tops = _topk_rows(scores, key_pos, [None] * len(scores), k)
    cands, experts = [], []
    for lt in range(n_tiles):
        (v0, i0, _), (v1, i1, _) = tops[2 * lt], tops[2 * lt + 1]
        vals = [v0[0:1] + v1]
        exps = [i0[0:1] * PEER_N_KEYS + i1]
        for i in range(1, 8):
            vals.append(v0[i:i + 1] + v1[0:8])
            exps.append(i0[i:i + 1] * PEER_N_KEYS + i1[0:8])
        vals.append(v0[8:k] + v1[0:1])
        exps.append(i0[8:k] * PEER_N_KEYS + i1[0:1])
        cands.append(jnp.where(cvalid, jnp.concatenate(vals, axis=0), -jnp.inf))
        experts.append(jnp.concatenate(exps, axis=0))
    picks = _topk_rows(cands, cpos, experts, k)
    for lt in range(n_tiles):
        best, _, expert = picks[lt]
        cols = pl.ds(lt * LANES, LANES)
        e = jnp.exp(best - jnp.max(best, axis=0, keepdims=True))
        gate_ref[:, cols] = e / jnp.sum(e, axis=0, keepdims=True)
        idx_ref[:, cols] = expert.astype(jnp.int32)


def _route(q3, keys, tb=512):
    nq, n, _ = q3.shape
    heads = nq // 2
    cpos = jnp.broadcast_to(jnp.asarray(_CAND_POS)[:, None], (_N_CAND, LANES))
    cvalid = jnp.broadcast_to(jnp.asarray(_CAND_VALID, F32)[:, None], (_N_CAND, LANES))
    return pl.pallas_call(
        _route_body,
        grid=(n // tb, heads),
        in_specs=[pl.BlockSpec((2, tb, PEER_HALF), lambda i, h: (h, i, 0)),
                  pl.BlockSpec((2, PEER_N_KEYS, PEER_HALF), lambda i, h: (h, 0, 0)),
                  pl.BlockSpec((_N_CAND, LANES), lambda i, h: (0, 0)),
                  pl.BlockSpec((_N_CAND, LANES), lambda i, h: (0, 0))],
        out_specs=[pl.BlockSpec((PEER_TOPK, tb), lambda i, h: (h, i)),
                   pl.BlockSpec((PEER_TOPK, tb), lambda i, h: (h, i))],
        out_shape=[jax.ShapeDtypeStruct((heads * PEER_TOPK, n), jnp.int32),
                   jax.ShapeDtypeStruct((heads * PEER_TOPK, n), F32)],
        compiler_params=pltpu.CompilerParams(dimension_semantics=("parallel", "parallel"),
                                             vmem_limit_bytes=VMEM_LIMIT),
        name="route",
    )(q3, keys, cpos, cvalid)


def _sc_pipeline(idx_v, table_hbm, rows, sems, n_units, units_per_token, compute):
    def start(unit, slot):
        tl = unit // units_per_token
        g = unit % units_per_token
        iv = idx_v[tl, pl.ds(g * SC_LANES, SC_LANES)]
        pltpu.async_copy(table_hbm.at[iv], rows[slot], sems[slot])

    def wait(slot):
        iv = idx_v[0, pl.ds(0, SC_LANES)]
        pltpu.make_async_copy(table_hbm.at[iv], rows[slot], sems[slot]).wait()

    start(0, 0)

    def pair(i, carry):
        u0 = 2 * i
        start(u0 + 1, 1)
        wait(0)
        compute(u0 // units_per_token, u0 % units_per_token, rows[0])

        @pl.when(u0 + 2 < n_units)
        def _():
            start(u0 + 2, 0)

        wait(1)
        compute((u0 + 1) // units_per_token, (u0 + 1) % units_per_token, rows[1])
        return carry

    lax.fori_loop(0, n_units // 2, pair, 0)


def _peer_act(hn, idx, u, tb=8):
    n, d = hn.shape
    kk = idx.shape[1]
    tpw = n // SC_WORKERS
    upt = kk // SC_LANES
    nchunk = d // SC_LANES
    mesh = plsc.VectorSubcoreMesh(core_axis_name="c", subcore_axis_name="s")

    def body(x_hbm, idx_hbm, u_hbm, out_hbm, x_v, idx_v, act_v, rows0, rows1, tmp_v, sem0, sem1):
        wid = lax.axis_index("s") * 2 + lax.axis_index("c")
        lane_row = lax.iota(jnp.int32, SC_LANES) * SC_LANES

        def compute(tl, g, r_ref):
            def cbody(c, accs):
                xv = x_v[tl, pl.ds(c * SC_LANES, SC_LANES)]
                return tuple(accs[r] + r_ref[r, pl.ds(c * SC_LANES, SC_LANES)] * xv
                             for r in range(SC_LANES))
            accs = lax.fori_loop(0, nchunk, cbody,
                                 tuple(jnp.zeros((SC_LANES,), F32) for _ in range(SC_LANES)))
            for r in range(SC_LANES):
                tmp_v[pl.ds(r * SC_LANES, SC_LANES)] = accs[r]
            tot = plsc.load_gather(tmp_v, [lane_row])
            for j in range(1, SC_LANES):
                tot = tot + plsc.load_gather(tmp_v, [lane_row + j])
            act_v[tl, pl.ds(g * SC_LANES, SC_LANES)] = tot

        def batch(b, carry):
            t0 = wid * tpw + b * tb
            pltpu.sync_copy(x_hbm.at[pl.ds(t0, tb)], x_v)
            pltpu.sync_copy(idx_hbm.at[pl.ds(t0, tb)], idx_v)
            _sc_pipeline(idx_v, u_hbm, (rows0, rows1), (sem0, sem1), tb * upt, upt, compute)
            pltpu.sync_copy(act_v, out_hbm.at[pl.ds(t0, tb)])
            return carry

        lax.fori_loop(0, tpw // tb, batch, 0)

    return pl.kernel(
        body, mesh=mesh,
        compiler_params=pltpu.CompilerParams(needs_layout_passes=False),
        out_type=jax.ShapeDtypeStruct((n, kk), F32),
        scratch_types=[pltpu.VMEM((tb, d), F32), pltpu.VMEM((tb, kk), jnp.int32),
                       pltpu.VMEM((tb, kk), F32),
                       pltpu.VMEM((SC_LANES, d), F32), pltpu.VMEM((SC_LANES, d), F32),
                       pltpu.VMEM((SC_LANES * SC_LANES,), F32),
                       pltpu.SemaphoreType.DMA, pltpu.SemaphoreType.DMA],
        name="peer_act",
    )(hn, idx, u)


def _peer_out(w, idx, v, tb=8, cb=16):
    n, kk = w.shape
    d = v.shape[1]
    tpw = n // SC_WORKERS
    upt = kk // SC_LANES
    nchunk = d // SC_LANES
    mesh = plsc.VectorSubcoreMesh(core_axis_name="c", subcore_axis_name="s")

    def body(w_hbm, idx_hbm, v_hbm, out_hbm, w_v, idx_v, out_v, rows0, rows1, sem0, sem1):
        wid = lax.axis_index("s") * 2 + lax.axis_index("c")
        zero = jnp.zeros((SC_LANES,), F32)

        def compute(tl, g, r_ref):
            tls = jnp.full((SC_LANES,), tl, jnp.int32)

            def cb_body(ci, carry):
                c0 = ci * (cb * SC_LANES)
                accs = tuple(out_v[tl, pl.ds(c0 + j * SC_LANES, SC_LANES)] for j in range(cb))

                def rbody(r, accs):
                    wr = plsc.load_gather(w_v, [tls, jnp.full((SC_LANES,), g * SC_LANES + r, jnp.int32)])
                    return tuple(accs[j] + wr * r_ref[r, pl.ds(c0 + j * SC_LANES, SC_LANES)]
                                 for j in range(cb))

                accs = lax.fori_loop(0, SC_LANES, rbody, accs)
                for j in range(cb):
                    out_v[tl, pl.ds(c0 + j * SC_LANES, SC_LANES)] = accs[j]
                return carry

            lax.fori_loop(0, nchunk // cb, cb_body, 0)

        def batch(b, carry):
            t0 = wid * tpw + b * tb
            pltpu.sync_copy(w_hbm.at[pl.ds(t0, tb)], w_v)
            pltpu.sync_copy(idx_hbm.at[pl.ds(t0, tb)], idx_v)

            def zbody(i, carry):
                out_v[i // nchunk, pl.ds((i % nchunk) * SC_LANES, SC_LANES)] = zero
                return carry

            lax.fori_loop(0, tb * nchunk, zbody, 0)
            _sc_pipeline(idx_v, v_hbm, (rows0, rows1), (sem0, sem1), tb * upt, upt, compute)
            pltpu.sync_copy(out_v, out_hbm.at[pl.ds(t0, tb)])
            return carry

        lax.fori_loop(0, tpw // tb, batch, 0)

    return pl.kernel(
        body, mesh=mesh,
        compiler_params=pltpu.CompilerParams(needs_layout_passes=False),
        out_type=jax.ShapeDtypeStruct((n, d), F32),
        scratch_types=[pltpu.VMEM((tb, kk), F32), pltpu.VMEM((tb, kk), jnp.int32),
                       pltpu.VMEM((tb, d), F32),
                       pltpu.VMEM((SC_LANES, d), F32), pltpu.VMEM((SC_LANES, d), F32),
                       pltpu.SemaphoreType.DMA, pltpu.SemaphoreType.DMA],
        name="peer_out",
    )(w, idx, v)


SUBLANES = 8
ROW_TILE = (SUBLANES, LANES)
TC_PEER_TOKENS = 8
ISSUE_UNROLL = 8


def _gelu(a):
    return a * (lax.erf(a / np.sqrt(2.0).astype(np.float32)) + 1.0) / 2.0


def _peer_tc_body(idx_ref, x_ref, g_ref, u_hbm, v_hbm, o_ref, ubuf, vbuf, sem_u, sem_v):
    tokens = x_ref.shape[0]
    kk = g_ref.shape[1]
    n_rows = tokens * kk

    def row_copy(table, buf, sem, expert, j):
        src = table.at[pl.ds(pl.multiple_of(expert * SUBLANES, SUBLANES), SUBLANES)]
        return pltpu.make_async_copy(src, buf.at[pl.ds(j * SUBLANES, SUBLANES)], sem)

    def issue(i, carry):
        for r in range(ISSUE_UNROLL):
            j = i * ISSUE_UNROLL + r
            expert = idx_ref[j]
            row_copy(u_hbm, ubuf, sem_u, expert, j).start()
            row_copy(v_hbm, vbuf, sem_v, expert, j).start()
        return carry

    lax.fori_loop(0, n_rows // ISSUE_UNROLL, issue, 0)

    def wait_all(table, buf, sem):
        pltpu.make_async_copy(table.at[pl.ds(0, n_rows * SUBLANES)], buf, sem).wait()

    def rows_of(buf, t, s):
        return buf[pl.ds(t * kk * SUBLANES + s, kk, stride=SUBLANES), :]

    wait_all(u_hbm, ubuf, sem_u)
    acts = []
    for t in range(tokens):
        xt = x_ref[t]
        acc = rows_of(ubuf, t, 0) * xt[0:1, :]
        for s in range(1, SUBLANES):
            acc = acc + rows_of(ubuf, t, s) * xt[s:s + 1, :]
        acts.append(jnp.sum(acc, axis=1, keepdims=True))
    w = g_ref[0] * _gelu(jnp.concatenate(acts, axis=1))

    wait_all(v_hbm, vbuf, sem_v)
    for t in range(tokens):
        wt = w[:, t:t + 1]
        o_ref[t] = jnp.concatenate(
            [jnp.sum(rows_of(vbuf, t, s) * wt, axis=0, keepdims=True) for s in range(SUBLANES)], axis=0)


def _peer_tc(hn, idx, gate_t, u3, v3):
    m, d = hn.shape
    kk = idx.shape[1]
    tb = TC_PEER_TOKENS
    x3 = hn.reshape(m, *ROW_TILE)
    g3 = gate_t.reshape(kk, m // tb, tb).transpose(1, 0, 2)
    rows = tb * kk * SUBLANES
    out = pl.pallas_call(
        _peer_tc_body,
        grid=(m // tb,),
        in_specs=[pl.BlockSpec((tb * kk,), lambda i: (i,), memory_space=pltpu.SMEM),
                  pl.BlockSpec((tb, *ROW_TILE), lambda i: (i, 0, 0)),
                  pl.BlockSpec((1, kk, tb), lambda i: (i, 0, 0)),
                  pl.BlockSpec(memory_space=pl.ANY),
                  pl.BlockSpec(memory_space=pl.ANY)],
        out_specs=pl.BlockSpec((tb, *ROW_TILE), lambda i: (i, 0, 0)),
        out_shape=jax.ShapeDtypeStruct((m, *ROW_TILE), F32),
        scratch_shapes=[pltpu.VMEM((rows, LANES), F32), pltpu.VMEM((rows, LANES), F32),
                        pltpu.SemaphoreType.DMA, pltpu.SemaphoreType.DMA],
        compiler_params=pltpu.CompilerParams(dimension_semantics=("arbitrary",),
                                             vmem_limit_bytes=VMEM_LIMIT),
        name="peer_tc",
    )(idx.reshape(m * kk), x3, g3, u3, v3)
    return out.reshape(m, d)


def _gelu_gate_body(a_ref, g_ref, o_ref):
    o_ref[...] = g_ref[...] * _gelu(a_ref[...])


def _gelu_gate(act, gate, tm=SHARE_GRANULE):
    n, kk = act.shape
    tm = min(tm, n)
    assert n % tm == 0
    spec = pl.BlockSpec((tm, kk), lambda i: (i, 0))
    return pl.pallas_call(
        _gelu_gate_body, grid=(n // tm,), in_specs=[spec, spec], out_specs=spec,
        out_shape=jax.ShapeDtypeStruct((n, kk), F32),
        compiler_params=pltpu.CompilerParams(dimension_semantics=("parallel",)),
        name="gelu_gate",
    )(act, gate)


def _final_body(h_ref, pa_ref, pb_ref, g_ref, o_ref, *, a_blocks):
    peer = jnp.where(pl.program_id(0) < a_blocks, pa_ref[...], pb_ref[...])
    h = h_ref[...] + peer
    o_ref[...] = h * lax.rsqrt(jnp.mean(h * h, axis=-1, keepdims=True) + EPS) * g_ref[...]


def _final(h1, peer_a, peer_b, g, tm=256):
    n, d = h1.shape
    ma, mb = peer_a.shape[0], peer_b.shape[0]
    assert ma + mb == n and ma % tm == 0 and mb % tm == 0 and ma > 0 and mb > 0
    a_blocks = ma // tm
    spec = pl.BlockSpec((tm, d), lambda i: (i, 0))
    return pl.pallas_call(
        functools.partial(_final_body, a_blocks=a_blocks), grid=(n // tm,),
        in_specs=[spec,
                  pl.BlockSpec((tm, d), lambda i: (jnp.minimum(i, a_blocks - 1), 0)),
                  pl.BlockSpec((tm, d), lambda i: (jnp.maximum(i - a_blocks, 0), 0)),
                  pl.BlockSpec((1, d), lambda i: (0, 0))],
        out_specs=spec,
        out_shape=jax.ShapeDtypeStruct((n, d), F32),
        compiler_params=pltpu.CompilerParams(dimension_semantics=("parallel",)),
        name="final_norm",
    )(h1, peer_a, peer_b, g)


def _tc_share(n_tokens):
    m = int(n_tokens * TC_SHARE) // SHARE_GRANULE * SHARE_GRANULE
    return min(max(m, SHARE_GRANULE), n_tokens - SHARE_GRANULE)


def kernel(x, norm1_g, w_in, hg_lower_logits, hg_norm_g, gla_w_gate_up, gla_b_gate, gla_norm_g,
           w_out, norm2_g, peer_w_q, peer_sub_keys, peer_u, peer_v, norm_f_g):
    batch, seq, d = x.shape
    depth = w_in.shape[0]
    assert depth == 1, "single-layer block"
    layer = 0

    w = w_in[layer]
    pad = LANES - GLA_GATE_RANK
    w_r = jnp.concatenate(
        [w[:, :3072], w[:, 3072:3072 + GLA_GATE_RANK], jnp.zeros((d, pad), w.dtype),
         w[:, 3072 + GLA_GATE_RANK:]], axis=1).astype(BF16)
    wgu = jnp.concatenate([gla_w_gate_up[layer], jnp.zeros((pad, GLA_KEY), F32)], axis=0).astype(BF16)

    wo = w_out[layer].astype(BF16)
    wq = peer_w_q[layer].astype(BF16)
    keys = peer_sub_keys[layer].reshape(PEER_HEADS * 2, PEER_N_KEYS, PEER_HALF).astype(BF16)
    n_experts = peer_u.shape[1]
    u_tiles = peer_u[layer].reshape(n_experts * SUBLANES, LANES)
    v_tiles = peer_v[layer].reshape(n_experts * SUBLANES, LANES)

    groups = PIPELINE_GROUPS if batch % PIPELINE_GROUPS == 0 else 1
    bpg = batch // groups
    outs = []
    for gi in range(groups):
        h = x[gi * bpg:(gi + 1) * bpg].reshape(bpg * seq, d)
        proj = _inproj(h, norm1_g[layer].reshape(1, d), w_r)
        mixed = _mixers(proj, bpg, seq, hg_lower_logits, hg_norm_g[layer].reshape(1, HEAD_V), wgu,
                        gla_b_gate[layer].reshape(1, GLA_KEY), gla_norm_g[layer].reshape(1, HEAD_V))
        h1, hn, q3 = _outproj(h, mixed, wo, norm2_g[layer].reshape(1, d), wq)
        idx_t, gate_t = _route(q3, keys)
        idx = idx_t.T
        m_tc = _tc_share(bpg * seq)
        peer_tc = _peer_tc(hn[:m_tc], idx[:m_tc], gate_t[:, :m_tc], u_tiles, v_tiles)
        act = _peer_act(hn[m_tc:], idx[m_tc:], peer_u[layer])
        wts = _gelu_gate(act, gate_t[:, m_tc:].T)
        peer_sc = _peer_out(wts, idx[m_tc:], peer_v[layer])
        outs.append(_final(h1, peer_tc, peer_sc, norm_f_g.reshape(1, d)))
    return jnp.concatenate(outs, axis=0).reshape(batch, seq, d)
```

```python
import functools

import jax
import jax.numpy as jnp
import numpy as np
from jax import lax
from jax.experimental import pallas as pl
from jax.experimental.pallas import tpu as pltpu
from jax.experimental.pallas import tpu_sc as plsc

F32 = jnp.float32
BF16 = jnp.bfloat16
EPS = 1e-6

D_MODEL = 1024
HG_HEADS = 4
GLA_HEADS = 4
HEAD_V = 128
HG_KEY = 512
GLA_KEY = 256
GLA_HEAD_K = 64
GLA_GATE_RANK = 16
GLA_GATE_NORMALIZER = 16.0
CHUNK = 64
LANES = 128
PEER_HEADS = 8
PEER_N_KEYS = 128
PEER_HALF = 128
PEER_TOPK = 16

OFF_HQ, OFF_HF, OFF_HI, OFF_HGATE = 0, 512, 1024, 1536
OFF_GQ, OFF_GK, OFF_GV, OFF_GLOW, OFF_GGATE = 2048, 2304, 2560, 3072, 3200
IN_WIDTH_PADDED = 3712

VMEM_LIMIT = 48 * 1024 * 1024

SC_LANES = 16
SC_WORKERS = 32
PIPELINE_GROUPS = 4
TC_SHARE = 0.21875
SHARE_GRANULE = 256


def _nt(a, b):
    return lax.dot_general(a, b, (((1,), (1,)), ((), ())), preferred_element_type=F32)


def _tn(a, b):
    return lax.dot_general(a, b, (((0,), (0,)), ((), ())), preferred_element_type=F32)


def _inproj_body(x_ref, g_ref, w_ref, o_ref):
    x = x_ref[...]
    xn = x * lax.rsqrt(jnp.mean(x * x, axis=-1, keepdims=True) + EPS) * g_ref[...]
    o_ref[...] = jnp.dot(xn.astype(BF16), w_ref[...], preferred_element_type=F32)


def _inproj(x2, g, w, tm=256):
    n, d = x2.shape
    wd = w.shape[1]
    return pl.pallas_call(
        _inproj_body,
        grid=(n // tm,),
        in_specs=[pl.BlockSpec((tm, d), lambda i: (i, 0)),
                  pl.BlockSpec((1, d), lambda i: (0, 0)),
                  pl.BlockSpec((d, wd), lambda i: (0, 0))],
        out_specs=pl.BlockSpec((tm, wd), lambda i: (i, 0)),
        out_shape=jax.ShapeDtypeStruct((n, wd), F32),
        compiler_params=pltpu.CompilerParams(dimension_semantics=("parallel",),
                                             vmem_limit_bytes=VMEM_LIMIT),
        name="inproj",
    )(x2, g, w)


def _level_constants():
    c = CHUNK
    mats = [np.tril(np.ones((c, c), np.float32))]
    level = np.full((c, c), -1, np.int32)
    b, lvl = c // 2, 0
    while b >= 1:
        m = np.zeros((c, c), np.float32)
        for s in range(0, c, 2 * b):
            mid = s + b
            for i in range(mid, s + 2 * b):
                m[i, mid:i + 1] = 1.0
                level[i, s:mid] = lvl
            for j in range(s, mid):
                m[j, j + 1:mid] = 1.0
        mats.append(m)
        b //= 2
        lvl += 1
    level[np.arange(c), np.arange(c)] = lvl
    return np.concatenate(mats, axis=0), level, lvl


_SEG_MATS, _LEVEL_MAP, _N_LEVELS = _level_constants()


def _split3(a):
    hi = a.astype(BF16)
    r = a - hi.astype(F32)
    mid = r.astype(BF16)
    lo = (r - mid.astype(F32)).astype(BF16)
    return jnp.concatenate([hi, mid, lo], axis=1)


def _gla_chunk(q, k, g, v, st, seg, level):
    c = CHUNK
    ex3 = jnp.dot(seg, _split3(g), preferred_element_type=F32)
    ex = ex3[:, 0:LANES] + ex3[:, LANES:2 * LANES] + ex3[:, 2 * LANES:3 * LANES]
    cum = ex[0:c]
    scores = jnp.where(level == _N_LEVELS, _nt(q.astype(BF16), k.astype(BF16)), 0.0)
    for l in range(_N_LEVELS):
        e = jnp.exp(ex[c * (l + 1):c * (l + 2)])
        p = _nt((q * e).astype(BF16), (k * e).astype(BF16))
        scores = jnp.where(level == l, p, scores)
    last = cum[c - 1:c, :]
    qd = (q * jnp.exp(cum)).astype(BF16)
    kd = (k * jnp.exp(last - cum)).astype(BF16)
    vb = v.astype(BF16)
    o = jnp.dot(scores.astype(BF16), vb, preferred_element_type=F32) + _nt(qd, st.astype(BF16))
    st_new = st * jnp.exp(last) + _tn(vb, kd)
    return o, st_new


def _head_out(o, gain, gate):
    on = o * lax.rsqrt(jnp.mean(o * o, axis=-1, keepdims=True) + EPS) * gain
    return on * (gate * jax.nn.sigmoid(gate))


def _mixer_body(p_ref, seg_ref, level_ref, lbl_ref, hgn_ref, wgu_ref, bg_ref, ggn_ref,
                o_ref, st_ref, *, chunks):
    @pl.when(pl.program_id(1) == 0)
    def _():
        st_ref[...] = jnp.zeros_like(st_ref)

    seg = seg_ref[...]
    level = level_ref[...]
    logits = lbl_ref[...]
    ez = jnp.exp(logits - jnp.max(logits, axis=0, keepdims=True))
    lb = ez[0:1, :] / jnp.sum(ez, axis=0, keepdims=True)
    lane = lax.broadcasted_iota(jnp.int32, (CHUNK, LANES), 1)

    def chunk_body(ci, carry):
        rows = pl.ds(pl.multiple_of(ci * CHUNK, CHUNK), CHUNK)

        def col(off):
            return p_ref[rows, pl.ds(off, LANES)]

        for h in range(HG_HEADS):
            hq = col(OFF_HQ + h * LANES)
            q = hq * jax.nn.sigmoid(hq)
            lbh = lb[:, h * LANES:(h + 1) * LANES]
            forget = lbh + (1.0 - lbh) * jax.nn.sigmoid(col(OFF_HF + h * LANES))
            o, st = _gla_chunk(q, 1.0 - forget, jnp.log(forget), col(OFF_HI + h * LANES),
                               st_ref[h], seg, level)
            st_ref[h] = st
            o_ref[rows, pl.ds(h * HEAD_V, HEAD_V)] = _head_out(
                o, hgn_ref[...], col(OFF_HGATE + h * HEAD_V)).astype(o_ref.dtype)

        zg = jnp.dot(col(OFF_GLOW).astype(BF16), wgu_ref[...], preferred_element_type=F32) + bg_ref[...]
        log_g = (jnp.minimum(zg, 0.0) - jnp.log(1.0 + jnp.exp(-jnp.abs(zg)))) / GLA_GATE_NORMALIZER
        for h in range(GLA_HEADS):
            pair, half = h // 2, h % 2
            q = col(OFF_GQ + pair * LANES) * (GLA_HEAD_K ** -0.5)
            in_head = (lane >= half * GLA_HEAD_K) & (lane < (half + 1) * GLA_HEAD_K)
            k = jnp.where(in_head, col(OFF_GK + pair * LANES), 0.0)
            g = log_g[:, pair * LANES:(pair + 1) * LANES]
            o, st = _gla_chunk(q, k, g, col(OFF_GV + h * HEAD_V), st_ref[HG_HEADS + h], seg, level)
            st_ref[HG_HEADS + h] = st
            o_ref[rows, pl.ds((HG_HEADS + h) * HEAD_V, HEAD_V)] = _head_out(
                o, ggn_ref[...], col(OFF_GGATE + h * HEAD_V)).astype(o_ref.dtype)
        return carry

    lax.fori_loop(0, chunks, chunk_body, 0)


def _mixers(proj, batch, seq, lb_logits, hg_norm_g, wgu, bg, gla_norm_g, tt=512):
    n = batch * seq
    steps = seq // tt
    heads = HG_HEADS + GLA_HEADS
    const = lambda shape: pl.BlockSpec(shape, lambda b, t: (0,) * len(shape))
    return pl.pallas_call(
        functools.partial(_mixer_body, chunks=tt // CHUNK),
        grid=(batch, steps),
        in_specs=[pl.BlockSpec((tt, IN_WIDTH_PADDED), lambda b, t: (b * steps + t, 0)),
                  const(_SEG_MATS.shape), const(_LEVEL_MAP.shape),
                  const(lb_logits.shape), const((1, HEAD_V)),
                  const(wgu.shape), const(bg.shape), const((1, HEAD_V))],
        out_specs=pl.BlockSpec((tt, heads * HEAD_V), lambda b, t: (b * steps + t, 0)),
        out_shape=jax.ShapeDtypeStruct((n, heads * HEAD_V), BF16),
        scratch_shapes=[pltpu.VMEM((heads, HEAD_V, LANES), F32)],
        compiler_params=pltpu.CompilerParams(dimension_semantics=("parallel", "arbitrary"),
                                             vmem_limit_bytes=VMEM_LIMIT),
        name="mixers",
    )(proj, jnp.asarray(_SEG_MATS, BF16), jnp.asarray(_LEVEL_MAP), lb_logits, hg_norm_g,
      wgu, bg, gla_norm_g)


def _outproj_body(x_ref, m_ref, wo_ref, g_ref, wq_ref, h_ref, hn_ref, q_ref):
    h = x_ref[...] + jnp.dot(m_ref[...], wo_ref[...], preferred_element_type=F32)
    h_ref[...] = h
    hn = h * lax.rsqrt(jnp.mean(h * h, axis=-1, keepdims=True) + EPS) * g_ref[...]
    hn_ref[...] = hn
    q = jnp.dot(hn.astype(BF16), wq_ref[...], preferred_element_type=F32)
    for j in range(q_ref.shape[0]):
        q_ref[j] = q[:, j * PEER_HALF:(j + 1) * PEER_HALF]


def _outproj(x2, mixed, wo, g2, wq, tm=256):
    n, d = x2.shape
    nq = wq.shape[1] // PEER_HALF
    return pl.pallas_call(
        _outproj_body,
        grid=(n // tm,),
        in_specs=[pl.BlockSpec((tm, d), lambda i: (i, 0)),
                  pl.BlockSpec((tm, mixed.shape[1]), lambda i: (i, 0)),
                  pl.BlockSpec(wo.shape, lambda i: (0, 0)),
                  pl.BlockSpec((1, d), lambda i: (0, 0)),
                  pl.BlockSpec(wq.shape, lambda i: (0, 0))],
        out_specs=[pl.BlockSpec((tm, d), lambda i: (i, 0)),
                   pl.BlockSpec((tm, d), lambda i: (i, 0)),
                   pl.BlockSpec((nq, tm, PEER_HALF), lambda i: (0, i, 0))],
        out_shape=[jax.ShapeDtypeStruct((n, d), F32),
                   jax.ShapeDtypeStruct((n, d), F32),
                   jax.ShapeDtypeStruct((nq, n, PEER_HALF), F32)],
        compiler_params=pltpu.CompilerParams(dimension_semantics=("parallel",),
                                             vmem_limit_bytes=VMEM_LIMIT),
        name="outproj",
    )(x2, mixed, wo, g2, wq)


def _candidate_constants():
    k = PEER_TOPK
    pos, valid = [], []
    for j in range(k):
        pos.append(j); valid.append(True)
    for i in range(1, 8):
        for j in range(8):
            pos.append(i * k + j); valid.append((i + 1) * (j + 1) <= k)
    for i in range(8, k):
        pos.append(i * k); valid.append(True)
    return np.asarray(pos, np.float32), np.asarray(valid, bool)


_CAND_POS, _CAND_VALID = _candidate_constants()
_N_CAND = _CAND_POS.shape[0]


def _topk_rows(ss, pos, extras, k):
    t = ss[0].shape[1]
    slot = lax.broadcasted_iota(jnp.int32, (k, t), 0)

    def body(it, carry):
        out = []
        for (s, vals, poss, exts), extra in zip(carry, extras):
            m = jnp.max(s, axis=0, keepdims=True)
            p = jnp.min(jnp.where(s == m, pos, 1e9), axis=0, keepdims=True)
            hit = pos == p
            vals = jnp.where(slot == it, m, vals)
            poss = jnp.where(slot == it, p, poss)
            if extra is not None:
                x = jnp.max(jnp.where(hit, extra, -1.0), axis=0, keepdims=True)
                exts = jnp.where(slot == it, x, exts)
            out.append((jnp.where(hit, -jnp.inf, s), vals, poss, exts))
        return tuple(out)

    z = jnp.zeros((k, t), F32)
    res = lax.fori_loop(0, k, body, tuple((s, z, z, z) for s in ss))
    return [(vals, poss, exts) for _, vals, poss, exts in res]


def _route_body(q_ref, keys_ref, cpos_ref, cvalid_ref, idx_ref, gate_ref):
    k = PEER_TOPK
    n_tiles = q_ref.shape[1] // LANES
    key_pos = lax.broadcasted_iota(jnp.int32, (PEER_N_KEYS, LANES), 0).astype(F32)
    cpos = cpos_ref[...]
    cvalid = cvalid_ref[...] > 0.5
    scores = [_nt(keys_ref[p], q_ref[p, pl.ds(lt * LANES, LANES), :].astype(BF16))
              for lt in range(n_tiles) for p in range(2)]
    tops = _topk_rows(scores, key_pos, [None] * len(scores), k)
    cands, experts = [], []
    for lt in range(n_tiles):
        (v0, i0, _), (v1, i1, _) = tops[2 * lt], tops[2 * lt + 1]
        vals = [v0[0:1] + v1]
        exps = [i0[0:1] * PEER_N_KEYS + i1]
        for i in range(1, 8):
            vals.append(v0[i:i + 1] + v1[0:8])
            exps.append(i0[i:i + 1] * PEER_N_KEYS + i1[0:8])
        vals.append(v0[8:k] + v1[0:1])
        exps.append(i0[8:k] * PEER_N_KEYS + i1[0:1])
        cands.append(jnp.where(cvalid, jnp.concatenate(vals, axis=0), -jnp.inf))
        experts.append(jnp.concatenate(exps, axis=0))
    picks = _topk_rows(cands, cpos, experts, k)
    for lt in range(n_tiles):
        best, _, expert = picks[lt]
        cols = pl.ds(lt * LANES, LANES)
        e = jnp.exp(best - jnp.max(best, axis=0, keepdims=True))
        gate_ref[:, cols] = e / jnp.sum(e, axis=0, keepdims=True)
        idx_ref[:, cols] = expert.astype(jnp.int32)


def _route(q3, keys, tb=512):
    nq, n, _ = q3.shape
    heads = nq // 2
    cpos = jnp.broadcast_to(jnp.asarray(_CAND_POS)[:, None], (_N_CAND, LANES))
    cvalid = jnp.broadcast_to(jnp.asarray(_CAND_VALID, F32)[:, None], (_N_CAND, LANES))
    return pl.pallas_call(
        _route_body,
        grid=(n // tb, heads),
        in_specs=[pl.BlockSpec((2, tb, PEER_HALF), lambda i, h: (h, i, 0)),
                  pl.BlockSpec((2, PEER_N_KEYS, PEER_HALF), lambda i, h: (h, 0, 0)),
                  pl.BlockSpec((_N_CAND, LANES), lambda i, h: (0, 0)),
                  pl.BlockSpec((_N_CAND, LANES), lambda i, h: (0, 0))],
        out_specs=[pl.BlockSpec((PEER_TOPK, tb), lambda i, h: (h, i)),
                   pl.BlockSpec((PEER_TOPK, tb), lambda i, h: (h, i))],
        out_shape=[jax.ShapeDtypeStruct((heads * PEER_TOPK, n), jnp.int32),
                   jax.ShapeDtypeStruct((heads * PEER_TOPK, n), F32)],
        compiler_params=pltpu.CompilerParams(dimension_semantics=("parallel", "parallel"),
                                             vmem_limit_bytes=VMEM_LIMIT),
        name="route",
    )(q3, keys, cpos, cvalid)


def _gather_cost(n_tokens, rows_per_token, d):
    elems = n_tokens * rows_per_token * d
    return pl.CostEstimate(flops=2 * elems, transcendentals=0, bytes_accessed=4 * elems)


def _sc_pipeline(idx_v, table_hbm, rows, sems, n_units, units_per_token, compute):
    parts = rows[0].shape[0] // SC_LANES
    assert n_units % 2 == 0

    def start(unit, slot):
        tl = unit // units_per_token
        g = unit % units_per_token
        for part in range(parts):
            iv = idx_v[tl, pl.ds((g * parts + part) * SC_LANES, SC_LANES)]
            pltpu.async_copy(table_hbm.at[iv], rows[slot].at[pl.ds(part * SC_LANES, SC_LANES)], sems[slot])

    def wait(slot):
        iv = idx_v[0, pl.ds(0, SC_LANES)]
        for part in range(parts):
            pltpu.make_async_copy(table_hbm.at[iv], rows[slot].at[pl.ds(part * SC_LANES, SC_LANES)],
                                  sems[slot]).wait()

    start(0, 0)

    def pair(i, carry):
        u0 = 2 * i
        start(u0 + 1, 1)
        wait(0)
        compute(u0 // units_per_token, u0 % units_per_token, rows[0])

        @pl.when(u0 + 2 < n_units)
        def _():
            start(u0 + 2, 0)

        wait(1)
        compute((u0 + 1) // units_per_token, (u0 + 1) % units_per_token, rows[1])
        return carry

    lax.fori_loop(0, n_units // 2, pair, 0)


def _sc_batch(tokens_per_worker):
    for tb in (40, 32, 24, 16, 8):
        if tokens_per_worker % tb == 0:
            return tb
    raise ValueError(f"tokens per SparseCore worker ({tokens_per_worker}) must be a multiple of 8")


def _peer_act(hn, idx, u):
    n, d = hn.shape
    kk = idx.shape[1]
    tpw = n // SC_WORKERS
    tb = _sc_batch(tpw)
    upt = kk // SC_LANES
    nchunk = d // SC_LANES
    mesh = plsc.VectorSubcoreMesh(core_axis_name="c", subcore_axis_name="s")

    def body(x_hbm, idx_hbm, u_hbm, out_hbm, x_v, idx_v, act_v, rows0, rows1, tmp_v, sem0, sem1):
        wid = lax.axis_index("s") * 2 + lax.axis_index("c")
        lane_row = lax.iota(jnp.int32, SC_LANES) * SC_LANES

        def compute(tl, g, r_ref):
            def cbody(c, accs):
                xv = x_v[tl, pl.ds(c * SC_LANES, SC_LANES)]
                return tuple(accs[r] + r_ref[r, pl.ds(c * SC_LANES, SC_LANES)] * xv
                             for r in range(SC_LANES))
            accs = lax.fori_loop(0, nchunk, cbody,
                                 tuple(jnp.zeros((SC_LANES,), F32) for _ in range(SC_LANES)))
            for r in range(SC_LANES):
                tmp_v[pl.ds(r * SC_LANES, SC_LANES)] = accs[r]
            tot = plsc.load_gather(tmp_v, [lane_row])
            for j in range(1, SC_LANES):
                tot = tot + plsc.load_gather(tmp_v, [lane_row + j])
            act_v[tl, pl.ds(g * SC_LANES, SC_LANES)] = tot

        def batch(b, carry):
            t0 = wid * tpw + b * tb
            pltpu.sync_copy(x_hbm.at[pl.ds(t0, tb)], x_v)
            pltpu.sync_copy(idx_hbm.at[pl.ds(t0, tb)], idx_v)
            _sc_pipeline(idx_v, u_hbm, (rows0, rows1), (sem0, sem1), tb * upt, upt, compute)
            pltpu.sync_copy(act_v, out_hbm.at[pl.ds(t0, tb)])
            return carry

        lax.fori_loop(0, tpw // tb, batch, 0)

    return pl.kernel(
        body, mesh=mesh,
        compiler_params=pltpu.CompilerParams(needs_layout_passes=False),
        out_type=jax.ShapeDtypeStruct((n, kk), F32),
        scratch_types=[pltpu.VMEM((tb, d), F32), pltpu.VMEM((tb, kk), jnp.int32),
                       pltpu.VMEM((tb, kk), F32),
                       pltpu.VMEM((SC_LANES, d), F32), pltpu.VMEM((SC_LANES, d), F32),
                       pltpu.VMEM((SC_LANES * SC_LANES,), F32),
                       pltpu.SemaphoreType.DMA, pltpu.SemaphoreType.DMA],
        cost_estimate=_gather_cost(n, kk, d),
        name="peer_act",
    )(hn, idx, u)


def _peer_out(w, idx, v, cb=16, unit_rows=2 * SC_LANES):
    n, kk = w.shape
    d = v.shape[1]
    tpw = n // SC_WORKERS
    tb = _sc_batch(tpw)
    upt = kk // unit_rows
    nchunk = d // SC_LANES
    mesh = plsc.VectorSubcoreMesh(core_axis_name="c", subcore_axis_name="s")

    def body(w_hbm, idx_hbm, v_hbm, out_hbm, w_v, idx_v, out_v, rows0, rows1, sem0, sem1):
        wid = lax.axis_index("s") * 2 + lax.axis_index("c")
        zero = jnp.zeros((SC_LANES,), F32)

        def compute(tl, g, r_ref):
            tls = jnp.full((SC_LANES,), tl, jnp.int32)

            def cb_body(ci, carry):
                c0 = ci * (cb * SC_LANES)
                accs = tuple(out_v[tl, pl.ds(c0 + j * SC_LANES, SC_LANES)] for j in range(cb))

                def rbody(r, accs):
                    wr = plsc.load_gather(w_v, [tls, jnp.full((SC_LANES,), g * unit_rows + r, jnp.int32)])
                    return tuple(accs[j] + wr * r_ref[r, pl.ds(c0 + j * SC_LANES, SC_LANES)]
                                 for j in range(cb))

                accs = lax.fori_loop(0, unit_rows, rbody, accs)
                for j in range(cb):
                    out_v[tl, pl.ds(c0 + j * SC_LANES, SC_LANES)] = accs[j]
                return carry

            lax.fori_loop(0, nchunk // cb, cb_body, 0)

        def batch(b, carry):
            t0 = wid * tpw + b * tb
            pltpu.sync_copy(w_hbm.at[pl.ds(t0, tb)], w_v)
            pltpu.sync_copy(idx_hbm.at[pl.ds(t0, tb)], idx_v)

            def zbody(i, carry):
                out_v[i // nchunk, pl.ds((i % nchunk) * SC_LANES, SC_LANES)] = zero
                return carry

            lax.fori_loop(0, tb * nchunk, zbody, 0)
            _sc_pipeline(idx_v, v_hbm, (rows0, rows1), (sem0, sem1), tb * upt, upt, compute)
            pltpu.sync_copy(out_v, out_hbm.at[pl.ds(t0, tb)])
            return carry

        lax.fori_loop(0, tpw // tb, batch, 0)

    return pl.kernel(
        body, mesh=mesh,
        compiler_params=pltpu.CompilerParams(needs_layout_passes=False),
        out_type=jax.ShapeDtypeStruct((n, d), F32),
        scratch_types=[pltpu.VMEM((tb, kk), F32), pltpu.VMEM((tb, kk), jnp.int32),
                       pltpu.VMEM((tb, d), F32),
                       pltpu.VMEM((unit_rows, d), F32), pltpu.VMEM((unit_rows, d), F32),
                       pltpu.SemaphoreType.DMA, pltpu.SemaphoreType.DMA],
        cost_estimate=_gather_cost(n, kk, d),
        name="peer_out",
    )(w, idx, v)


SUBLANES = 8
ROW_TILE = (SUBLANES, LANES)
TC_PEER_TOKENS = 8
ISSUE_UNROLL = 8


def _gelu(a):
    return a * (lax.erf(a / np.sqrt(2.0).astype(np.float32)) + 1.0) / 2.0


def _peer_tc_body(idx_ref, x_ref, g_ref, u_hbm, v_hbm, o_ref, ubuf, vbuf, sem_u, sem_v):
    tokens = x_ref.shape[0]
    kk = g_ref.shape[1]
    n_rows = tokens * kk

    def row_copy(table, buf, sem, expert, j):
        src = table.at[pl.ds(pl.multiple_of(expert * SUBLANES, SUBLANES), SUBLANES)]
        return pltpu.make_async_copy(src, buf.at[pl.ds(j * SUBLANES, SUBLANES)], sem)

    def issue(i, carry):
        for r in range(ISSUE_UNROLL):
            j = i * ISSUE_UNROLL + r
            expert = idx_ref[j]
            row_copy(u_hbm, ubuf, sem_u, expert, j).start()
            row_copy(v_hbm, vbuf, sem_v, expert, j).start()
        return carry

    lax.fori_loop(0, n_rows // ISSUE_UNROLL, issue, 0)

    def wait_all(table, buf, sem):
        pltpu.make_async_copy(table.at[pl.ds(0, n_rows * SUBLANES)], buf, sem).wait()

    def rows_of(buf, t, s):
        return buf[pl.ds(t * kk * SUBLANES + s, kk, stride=SUBLANES), :]

    wait_all(u_hbm, ubuf, sem_u)
    acts = []
    for t in range(tokens):
        xt = x_ref[t]
        acc = rows_of(ubuf, t, 0) * xt[0:1, :]
        for s in range(1, SUBLANES):
            acc = acc + rows_of(ubuf, t, s) * xt[s:s + 1, :]
        acts.append(jnp.sum(acc, axis=1, keepdims=True))
    w = g_ref[0] * _gelu(jnp.concatenate(acts, axis=1))

    wait_all(v_hbm, vbuf, sem_v)
    for t in range(tokens):
        wt = w[:, t:t + 1]
        o_ref[t] = jnp.concatenate(
            [jnp.sum(rows_of(vbuf, t, s) * wt, axis=0, keepdims=True) for s in range(SUBLANES)], axis=0)


def _peer_tc(hn, idx, gate_t, u3, v3):
    m, d = hn.shape
    kk = idx.shape[1]
    tb = TC_PEER_TOKENS
    x3 = hn.reshape(m, *ROW_TILE)
    g3 = gate_t.reshape(kk, m // tb, tb).transpose(1, 0, 2)
    rows = tb * kk * SUBLANES
    out = pl.pallas_call(
        _peer_tc_body,
        grid=(m // tb,),
        in_specs=[pl.BlockSpec((tb * kk,), lambda i: (i,), memory_space=pltpu.SMEM),
                  pl.BlockSpec((tb, *ROW_TILE), lambda i: (i, 0, 0)),
                  pl.BlockSpec((1, kk, tb), lambda i: (i, 0, 0)),
                  pl.BlockSpec(memory_space=pl.ANY),
                  pl.BlockSpec(memory_space=pl.ANY)],
        out_specs=pl.BlockSpec((tb, *ROW_TILE), lambda i: (i, 0, 0)),
        out_shape=jax.ShapeDtypeStruct((m, *ROW_TILE), F32),
        scratch_shapes=[pltpu.VMEM((rows, LANES), F32), pltpu.VMEM((rows, LANES), F32),
                        pltpu.SemaphoreType.DMA, pltpu.SemaphoreType.DMA],
        compiler_params=pltpu.CompilerParams(dimension_semantics=("arbitrary",),
                                             vmem_limit_bytes=VMEM_LIMIT),
        cost_estimate=_gather_cost(m, 2 * kk, d),
        name="peer_tc",
    )(idx.reshape(m * kk), x3, g3, u3, v3)
    return out.reshape(m, d)


def _gelu_gate_body(a_ref, g_ref, o_ref):
    o_ref[...] = g_ref[...] * _gelu(a_ref[...])


def _gelu_gate(act, gate, tm=SHARE_GRANULE):
    n, kk = act.shape
    tm = min(tm, n)
    assert n % tm == 0
    spec = pl.BlockSpec((tm, kk), lambda i: (i, 0))
    return pl.pallas_call(
        _gelu_gate_body, grid=(n // tm,), in_specs=[spec, spec], out_specs=spec,
        out_shape=jax.ShapeDtypeStruct((n, kk), F32),
        compiler_params=pltpu.CompilerParams(dimension_semantics=("parallel",)),
        name="gelu_gate",
    )(act, gate)


def _final_body(h_ref, pa_ref, pb_ref, g_ref, o_ref, *, a_blocks):
    peer = jnp.where(pl.program_id(0) < a_blocks, pa_ref[...], pb_ref[...])
    h = h_ref[...] + peer
    o_ref[...] = h * lax.rsqrt(jnp.mean(h * h, axis=-1, keepdims=True) + EPS) * g_ref[...]


def _final(h1, peer_a, peer_b, g, tm=256):
    n, d = h1.shape
    ma, mb = peer_a.shape[0], peer_b.shape[0]
    assert ma + mb == n and ma % tm == 0 and mb % tm == 0 and ma > 0 and mb > 0
    a_blocks = ma // tm
    spec = pl.BlockSpec((tm, d), lambda i: (i, 0))
    return pl.pallas_call(
        functools.partial(_final_body, a_blocks=a_blocks), grid=(n // tm,),
        in_specs=[spec,
                  pl.BlockSpec((tm, d), lambda i: (jnp.minimum(i, a_blocks - 1), 0)),
                  pl.BlockSpec((tm, d), lambda i: (jnp.maximum(i - a_blocks, 0), 0)),
                  pl.BlockSpec((1, d), lambda i: (0, 0))],
        out_specs=spec,
        out_shape=jax.ShapeDtypeStruct((n, d), F32),
        compiler_params=pltpu.CompilerParams(dimension_semantics=("parallel",)),
        name="final_norm",
    )(h1, peer_a, peer_b, g)


def _tc_share(n_tokens):
    m = int(n_tokens * TC_SHARE) // SHARE_GRANULE * SHARE_GRANULE
    return min(max(m, SHARE_GRANULE), n_tokens - SHARE_GRANULE)


def kernel(x, norm1_g, w_in, hg_lower_logits, hg_norm_g, gla_w_gate_up, gla_b_gate, gla_norm_g,
           w_out, norm2_g, peer_w_q, peer_sub_keys, peer_u, peer_v, norm_f_g):
    batch, seq, d = x.shape
    depth = w_in.shape[0]
    assert depth == 1, "single-layer block"
    layer = 0

    w = w_in[layer]
    pad = LANES - GLA_GATE_RANK
    w_r = jnp.concatenate(
        [w[:, :3072], w[:, 3072:3072 + GLA_GATE_RANK], jnp.zeros((d, pad), w.dtype),
         w[:, 3072 + GLA_GATE_RANK:]], axis=1).astype(BF16)
    wgu = jnp.concatenate([gla_w_gate_up[layer], jnp.zeros((pad, GLA_KEY), F32)], axis=0).astype(BF16)

    wo = w_out[layer].astype(BF16)
    wq = peer_w_q[layer].astype(BF16)
    keys = peer_sub_keys[layer].reshape(PEER_HEADS * 2, PEER_N_KEYS, PEER_HALF).astype(BF16)
    n_experts = peer_u.shape[1]
    u_tiles = peer_u[layer].reshape(n_experts * SUBLANES, LANES)
    v_tiles = peer_v[layer].reshape(n_experts * SUBLANES, LANES)

    groups = PIPELINE_GROUPS if batch % PIPELINE_GROUPS == 0 else 1
    bpg = batch // groups
    m_tc = _tc_share(bpg * seq)
    gf = norm_f_g.reshape(1, d)

    outs = []
    waiting = None
    rows_done = None
    for gi in range(groups):
        h = x[gi * bpg:(gi + 1) * bpg].reshape(bpg * seq, d)
        if rows_done is not None:
            h, rows_done = lax.optimization_barrier((h, rows_done))
        proj = _inproj(h, norm1_g[layer].reshape(1, d), w_r)
        mixed = _mixers(proj, bpg, seq, hg_lower_logits, hg_norm_g[layer].reshape(1, HEAD_V), wgu,
                        gla_b_gate[layer].reshape(1, GLA_KEY), gla_norm_g[layer].reshape(1, HEAD_V))
        h1, hn, q3 = _outproj(h, mixed, wo, norm2_g[layer].reshape(1, d), wq)
        idx_t, gate_t = _route(q3, keys)
        idx = idx_t.T
        if waiting is not None:
            ph1, ppeer_sc, tc_in = waiting
            tc_in, idx = lax.optimization_barrier((tc_in, idx))
            rows_done = _peer_tc(*tc_in, u_tiles, v_tiles)
            outs.append(_final(ph1, rows_done, ppeer_sc, gf))
        act = _peer_act(hn[m_tc:], idx[m_tc:], peer_u[layer])
        wts = _gelu_gate(act, gate_t[:, m_tc:].T)
        peer_sc = _peer_out(wts, idx[m_tc:], peer_v[layer])
        waiting = (h1, peer_sc, (hn[:m_tc], idx[:m_tc], gate_t[:, :m_tc]))
    ph1, ppeer_sc, tc_in = waiting
    outs.append(_final(ph1, _peer_tc(*tc_in, u_tiles, v_tiles), ppeer_sc, gf))
    return jnp.concatenate(outs, axis=0).reshape(batch, seq, d)
```

```python
import functools

import jax
import jax.numpy as jnp
import numpy as np
from jax import lax
from jax.experimental import pallas as pl
from jax.experimental.pallas import tpu as pltpu
from jax.experimental.pallas import tpu_sc as plsc

F32 = jnp.float32
BF16 = jnp.bfloat16
EPS = 1e-6

D_MODEL = 1024
HG_HEADS = 4
GLA_HEADS = 4
HEAD_V = 128
HG_KEY = 512
GLA_KEY = 256
GLA_HEAD_K = 64
GLA_GATE_RANK = 16
GLA_GATE_NORMALIZER = 16.0
CHUNK = 64
LANES = 128
PEER_HEADS = 8
PEER_N_KEYS = 128
PEER_HALF = 128
PEER_TOPK = 16

OFF_HQ, OFF_HF, OFF_HI, OFF_HGATE = 0, 512, 1024, 1536
OFF_GQ, OFF_GK, OFF_GV, OFF_GLOW, OFF_GGATE = 2048, 2304, 2560, 3072, 3200
IN_WIDTH_PADDED = 3712

VMEM_LIMIT = 48 * 1024 * 1024

SC_LANES = 16
SC_WORKERS = 32
PIPELINE_GROUPS = 4
TC_SHARE = 0.21875
SHARE_GRANULE = 256


def _nt(a, b):
    return lax.dot_general(a, b, (((1,), (1,)), ((), ())), preferred_element_type=F32)


def _tn(a, b):
    return lax.dot_general(a, b, (((0,), (0,)), ((), ())), preferred_element_type=F32)


def _inproj_body(x_ref, g_ref, w_ref, o_ref):
    x = x_ref[...]
    xn = x * lax.rsqrt(jnp.mean(x * x, axis=-1, keepdims=True) + EPS) * g_ref[...]
    o_ref[...] = jnp.dot(xn.astype(BF16), w_ref[...], preferred_element_type=F32)


def _inproj(x2, g, w, tm=256):
    n, d = x2.shape
    wd = w.shape[1]
    return pl.pallas_call(
        _inproj_body,
        grid=(n // tm,),
        in_specs=[pl.BlockSpec((tm, d), lambda i: (i, 0)),
                  pl.BlockSpec((1, d), lambda i: (0, 0)),
                  pl.BlockSpec((d, wd), lambda i: (0, 0))],
        out_specs=pl.BlockSpec((tm, wd), lambda i: (i, 0)),
        out_shape=jax.ShapeDtypeStruct((n, wd), F32),
        compiler_params=pltpu.CompilerParams(dimension_semantics=("parallel",),
                                             vmem_limit_bytes=VMEM_LIMIT),
        name="inproj",
    )(x2, g, w)


def _level_constants():
    c = CHUNK
    mats = [np.tril(np.ones((c, c), np.float32))]
    level = np.full((c, c), -1, np.int32)
    b, lvl = c // 2, 0
    while b >= 1:
        m = np.zeros((c, c), np.float32)
        for s in range(0, c, 2 * b):
            mid = s + b
            for i in range(mid, s + 2 * b):
                m[i, mid:i + 1] = 1.0
                level[i, s:mid] = lvl
            for j in range(s, mid):
                m[j, j + 1:mid] = 1.0
        mats.append(m)
        b //= 2
        lvl += 1
    level[np.arange(c), np.arange(c)] = lvl
    return np.concatenate(mats, axis=0), level, lvl


_SEG_MATS, _LEVEL_MAP, _N_LEVELS = _level_constants()


def _split3(a):
    hi = a.astype(BF16)
    r = a - hi.astype(F32)
    mid = r.astype(BF16)
    lo = (r - mid.astype(F32)).astype(BF16)
    return jnp.concatenate([hi, mid, lo], axis=1)


def _gla_chunk(q, k, g, v, st, seg, level):
    c = CHUNK
    ex3 = jnp.dot(seg, _split3(g), preferred_element_type=F32)
    ex = ex3[:, 0:LANES] + ex3[:, LANES:2 * LANES] + ex3[:, 2 * LANES:3 * LANES]
    cum = ex[0:c]
    scores = jnp.where(level == _N_LEVELS, _nt(q.astype(BF16), k.astype(BF16)), 0.0)
    for l in range(_N_LEVELS):
        e = jnp.exp(ex[c * (l + 1):c * (l + 2)])
        p = _nt((q * e).astype(BF16), (k * e).astype(BF16))
        scores = jnp.where(level == l, p, scores)
    last = cum[c - 1:c, :]
    qd = (q * jnp.exp(cum)).astype(BF16)
    kd = (k * jnp.exp(last - cum)).astype(BF16)
    vb = v.astype(BF16)
    o = jnp.dot(scores.astype(BF16), vb, preferred_element_type=F32) + _nt(qd, st.astype(BF16))
    st_new = st * jnp.exp(last) + _tn(vb, kd)
    return o, st_new


def _head_out(o, gain, gate):
    on = o * lax.rsqrt(jnp.mean(o * o, axis=-1, keepdims=True) + EPS) * gain
    return on * (gate * jax.nn.sigmoid(gate))


def _mixer_body(p_ref, seg_ref, level_ref, lbl_ref, hgn_ref, wgu_ref, bg_ref, ggn_ref,
                o_ref, st_ref, *, chunks):
    @pl.when(pl.program_id(1) == 0)
    def _():
        st_ref[...] = jnp.zeros_like(st_ref)

    seg = seg_ref[...]
    level = level_ref[...]
    logits = lbl_ref[...]
    ez = jnp.exp(logits - jnp.max(logits, axis=0, keepdims=True))
    lb = ez[0:1, :] / jnp.sum(ez, axis=0, keepdims=True)
    lane = lax.broadcasted_iota(jnp.int32, (CHUNK, LANES), 1)

    def chunk_body(ci, carry):
        rows = pl.ds(pl.multiple_of(ci * CHUNK, CHUNK), CHUNK)

        def col(off):
            return p_ref[rows, pl.ds(off, LANES)]

        for h in range(HG_HEADS):
            hq = col(OFF_HQ + h * LANES)
            q = hq * jax.nn.sigmoid(hq)
            lbh = lb[:, h * LANES:(h + 1) * LANES]
            forget = lbh + (1.0 - lbh) * jax.nn.sigmoid(col(OFF_HF + h * LANES))
            o, st = _gla_chunk(q, 1.0 - forget, jnp.log(forget), col(OFF_HI + h * LANES),
                               st_ref[h], seg, level)
            st_ref[h] = st
            o_ref[rows, pl.ds(h * HEAD_V, HEAD_V)] = _head_out(
                o, hgn_ref[...], col(OFF_HGATE + h * HEAD_V)).astype(o_ref.dtype)

        zg = jnp.dot(col(OFF_GLOW).astype(BF16), wgu_ref[...], preferred_element_type=F32) + bg_ref[...]
        log_g = (jnp.minimum(zg, 0.0) - jnp.log(1.0 + jnp.exp(-jnp.abs(zg)))) / GLA_GATE_NORMALIZER
        for h in range(GLA_HEADS):
            pair, half = h // 2, h % 2
            q = col(OFF_GQ + pair * LANES) * (GLA_HEAD_K ** -0.5)
            in_head = (lane >= half * GLA_HEAD_K) & (lane < (half + 1) * GLA_HEAD_K)
            k = jnp.where(in_head, col(OFF_GK + pair * LANES), 0.0)
            g = log_g[:, pair * LANES:(pair + 1) * LANES]
            o, st = _gla_chunk(q, k, g, col(OFF_GV + h * HEAD_V), st_ref[HG_HEADS + h], seg, level)
            st_ref[HG_HEADS + h] = st
            o_ref[rows, pl.ds((HG_HEADS + h) * HEAD_V, HEAD_V)] = _head_out(
                o, ggn_ref[...], col(OFF_GGATE + h * HEAD_V)).astype(o_ref.dtype)
        return carry

    lax.fori_loop(0, chunks, chunk_body, 0)


def _mixers(proj, batch, seq, lb_logits, hg_norm_g, wgu, bg, gla_norm_g, tt=512):
    n = batch * seq
    steps = seq // tt
    heads = HG_HEADS + GLA_HEADS
    const = lambda shape: pl.BlockSpec(shape, lambda b, t: (0,) * len(shape))
    return pl.pallas_call(
        functools.partial(_mixer_body, chunks=tt // CHUNK),
        grid=(batch, steps),
        in_specs=[pl.BlockSpec((tt, IN_WIDTH_PADDED), lambda b, t: (b * steps + t, 0)),
                  const(_SEG_MATS.shape), const(_LEVEL_MAP.shape),
                  const(lb_logits.shape), const((1, HEAD_V)),
                  const(wgu.shape), const(bg.shape), const((1, HEAD_V))],
        out_specs=pl.BlockSpec((tt, heads * HEAD_V), lambda b, t: (b * steps + t, 0)),
        out_shape=jax.ShapeDtypeStruct((n, heads * HEAD_V), BF16),
        scratch_shapes=[pltpu.VMEM((heads, HEAD_V, LANES), F32)],
        compiler_params=pltpu.CompilerParams(dimension_semantics=("parallel", "arbitrary"),
                                             vmem_limit_bytes=VMEM_LIMIT),
        name="mixers",
    )(proj, jnp.asarray(_SEG_MATS, BF16), jnp.asarray(_LEVEL_MAP), lb_logits, hg_norm_g,
      wgu, bg, gla_norm_g)


def _outproj_body(x_ref, m_ref, wo_ref, g_ref, wq_ref, h_ref, hn_ref, q_ref):
    h = x_ref[...] + jnp.dot(m_ref[...], wo_ref[...], preferred_element_type=F32)
    h_ref[...] = h
    hn = h * lax.rsqrt(jnp.mean(h * h, axis=-1, keepdims=True) + EPS) * g_ref[...]
    hn_ref[...] = hn
    q = jnp.dot(hn.astype(BF16), wq_ref[...], preferred_element_type=F32)
    for j in range(q_ref.shape[0]):
        q_ref[j] = q[:, j * PEER_HALF:(j + 1) * PEER_HALF]


def _outproj(x2, mixed, wo, g2, wq, tm=256):
    n, d = x2.shape
    nq = wq.shape[1] // PEER_HALF
    return pl.pallas_call(
        _outproj_body,
        grid=(n // tm,),
        in_specs=[pl.BlockSpec((tm, d), lambda i: (i, 0)),
                  pl.BlockSpec((tm, mixed.shape[1]), lambda i: (i, 0)),
                  pl.BlockSpec(wo.shape, lambda i: (0, 0)),
                  pl.BlockSpec((1, d), lambda i: (0, 0)),
                  pl.BlockSpec(wq.shape, lambda i: (0, 0))],
        out_specs=[pl.BlockSpec((tm, d), lambda i: (i, 0)),
                   pl.BlockSpec((tm, d), lambda i: (i, 0)),
                   pl.BlockSpec((nq, tm, PEER_HALF), lambda i: (0, i, 0))],
        out_shape=[jax.ShapeDtypeStruct((n, d), F32),
                   jax.ShapeDtypeStruct((n, d), F32),
                   jax.ShapeDtypeStruct((nq, n, PEER_HALF), F32)],
        compiler_params=pltpu.CompilerParams(dimension_semantics=("parallel",),
                                             vmem_limit_bytes=VMEM_LIMIT),
        name="outproj",
    )(x2, mixed, wo, g2, wq)


def _candidate_constants():
    k = PEER_TOPK
    pos, valid = [], []
    for j in range(k):
        pos.append(j); valid.append(True)
    for i in range(1, 8):
        for j in range(8):
            pos.append(i * k + j); valid.append((i + 1) * (j + 1) <= k)
    for i in range(8, k):
        pos.append(i * k); valid.append(True)
    return np.asarray(pos, np.float32), np.asarray(valid, bool)


_CAND_POS, _CAND_VALID = _candidate_constants()
_N_CAND = _CAND_POS.shape[0]


def _topk_rows(ss, pos, extras, k):
    t = ss[0].shape[1]
    slot = lax.broadcasted_iota(jnp.int32, (k, t), 0)

    def body(it, carry):
        out = []
        for (s, vals, poss, exts), extra in zip(carry, extras):
            m = jnp.max(s, axis=0, keepdims=True)
            p = jnp.min(jnp.where(s == m, pos, 1e9), axis=0, keepdims=True)
            hit = pos == p
            vals = jnp.where(slot == it, m, vals)
            poss = jnp.where(slot == it, p, poss)
            if extra is not None:
                x = jnp.max(jnp.where(hit, extra, -1.0), axis=0, keepdims=True)
                exts = jnp.where(slot == it, x, exts)
            out.append((jnp.where(hit, -jnp.inf, s), vals, poss, exts))
        return tuple(out)

    z = jnp.zeros((k, t), F32)
    res = lax.fori_loop(0, k, body, tuple((s, z, z, z) for s in ss))
    return [(vals, poss, exts) for _, vals, poss, exts in res]


def _route_body(q_ref, keys_ref, cpos_ref, cvalid_ref, idx_ref, gate_ref):
    k = PEER_TOPK
    n_tiles = q_ref.shape[1] // LANES
    key_pos = lax.broadcasted_iota(jnp.int32, (PEER_N_KEYS, LANES), 0).astype(F32)
    cpos = cpos_ref[...]
    cvalid = cvalid_ref[...] > 0.5
    scores = [_nt(keys_ref[p], q_ref[p, pl.ds(lt * LANES, LANES), :].astype(BF16))
              for lt in range(n_tiles) for p in range(2)]
    tops = _topk_rows(scores, key_pos, [None] * len(scores), k)
    cands, experts = [], []
    for lt in range(n_tiles):
        (v0, i0, _), (v1, i1, _) = tops[2 * lt], tops[2 * lt + 1]
        vals = [v0[0:1] + v1]
        exps = [i0[0:1] * PEER_N_KEYS + i1]
        for i in range(1, 8):
            vals.append(v0[i:i + 1] + v1[0:8])
            exps.append(i0[i:i + 1] * PEER_N_KEYS + i1[0:8])
        vals.append(v0[8:k] + v1[0:1])
        exps.append(i0[8:k] * PEER_N_KEYS + i1[0:1])
        cands.append(jnp.where(cvalid, jnp.concatenate(vals, axis=0), -jnp.inf))
        experts.append(jnp.concatenate(exps, axis=0))
    picks = _topk_rows(cands, cpos, experts, k)
    for lt in range(n_tiles):
        best, _, expert = picks[lt]
        cols = pl.ds(lt * LANES, LANES)
        e = jnp.exp(best - jnp.max(best, axis=0, keepdims=True))
        gate_ref[:, cols] = e / jnp.sum(e, axis=0, keepdims=True)
        idx_ref[:, cols] = expert.astype(jnp.int32)


def _route(q3, keys, tb=512):
    nq, n, _ = q3.shape
    heads = nq // 2
    cpos = jnp.broadcast_to(jnp.asarray(_CAND_POS)[:, None], (_N_CAND, LANES))
    cvalid = jnp.broadcast_to(jnp.asarray(_CAND_VALID, F32)[:, None], (_N_CAND, LANES))
    return pl.pallas_call(
        _route_body,
        grid=(n // tb, heads),
        in_specs=[pl.BlockSpec((2, tb, PEER_HALF), lambda i, h: (h, i, 0)),
                  pl.BlockSpec((2, PEER_N_KEYS, PEER_HALF), lambda i, h: (h, 0, 0)),
                  pl.BlockSpec((_N_CAND, LANES), lambda i, h: (0, 0)),
                  pl.BlockSpec((_N_CAND, LANES), lambda i, h: (0, 0))],
        out_specs=[pl.BlockSpec((PEER_TOPK, tb), lambda i, h: (h, i)),
                   pl.BlockSpec((PEER_TOPK, tb), lambda i, h: (h, i))],
        out_shape=[jax.ShapeDtypeStruct((heads * PEER_TOPK, n), jnp.int32),
                   jax.ShapeDtypeStruct((heads * PEER_TOPK, n), F32)],
        compiler_params=pltpu.CompilerParams(dimension_semantics=("parallel", "parallel"),
                                             vmem_limit_bytes=VMEM_LIMIT),
        name="route",
    )(q3, keys, cpos, cvalid)


def _sc_pipeline(idx_v, table_hbm, rows, sems, n_units, units_per_token, compute):
    parts = rows[0].shape[0] // SC_LANES
    assert n_units % 2 == 0

    def start(unit, slot):
        tl = unit // units_per_token
        g = unit % units_per_token
        for part in range(parts):
            iv = idx_v[tl, pl.ds((g * parts + part) * SC_LANES, SC_LANES)]
            pltpu.async_copy(table_hbm.at[iv], rows[slot].at[pl.ds(part * SC_LANES, SC_LANES)], sems[slot])

    def wait(slot):
        iv = idx_v[0, pl.ds(0, SC_LANES)]
        for part in range(parts):
            pltpu.make_async_copy(table_hbm.at[iv], rows[slot].at[pl.ds(part * SC_LANES, SC_LANES)],
                                  sems[slot]).wait()

    start(0, 0)

    def pair(i, carry):
        u0 = 2 * i
        start(u0 + 1, 1)
        wait(0)
        compute(u0 // units_per_token, u0 % units_per_token, rows[0])

        @pl.when(u0 + 2 < n_units)
        def _():
            start(u0 + 2, 0)

        wait(1)
        compute((u0 + 1) // units_per_token, (u0 + 1) % units_per_token, rows[1])
        return carry

    lax.fori_loop(0, n_units // 2, pair, 0)


def _sc_batch(tokens_per_worker):
    for tb in (40, 32, 24, 16, 8):
        if tokens_per_worker % tb == 0:
            return tb
    raise ValueError(f"tokens per SparseCore worker ({tokens_per_worker}) must be a multiple of 8")


def _peer_act(hn, idx, u):
    n, d = hn.shape
    kk = idx.shape[1]
    tpw = n // SC_WORKERS
    tb = _sc_batch(tpw)
    upt = kk // SC_LANES
    nchunk = d // SC_LANES
    mesh = plsc.VectorSubcoreMesh(core_axis_name="c", subcore_axis_name="s")

    def body(x_hbm, idx_hbm, u_hbm, out_hbm, x_v, idx_v, act_v, rows0, rows1, tmp_v, sem0, sem1):
        wid = lax.axis_index("s") * 2 + lax.axis_index("c")
        lane_row = lax.iota(jnp.int32, SC_LANES) * SC_LANES

        def compute(tl, g, r_ref):
            def cbody(c, accs):
                xv = x_v[tl, pl.ds(c * SC_LANES, SC_LANES)]
                return tuple(accs[r] + r_ref[r, pl.ds(c * SC_LANES, SC_LANES)] * xv
                             for r in range(SC_LANES))
            accs = lax.fori_loop(0, nchunk, cbody,
                                 tuple(jnp.zeros((SC_LANES,), F32) for _ in range(SC_LANES)))
            for r in range(SC_LANES):
                tmp_v[pl.ds(r * SC_LANES, SC_LANES)] = accs[r]
            tot = plsc.load_gather(tmp_v, [lane_row])
            for j in range(1, SC_LANES):
                tot = tot + plsc.load_gather(tmp_v, [lane_row + j])
            act_v[tl, pl.ds(g * SC_LANES, SC_LANES)] = tot

        def batch(b, carry):
            t0 = wid * tpw + b * tb
            pltpu.sync_copy(x_hbm.at[pl.ds(t0, tb)], x_v)
            pltpu.sync_copy(idx_hbm.at[pl.ds(t0, tb)], idx_v)
            _sc_pipeline(idx_v, u_hbm, (rows0, rows1), (sem0, sem1), tb * upt, upt, compute)
            pltpu.sync_copy(act_v, out_hbm.at[pl.ds(t0, tb)])
            return carry

        lax.fori_loop(0, tpw // tb, batch, 0)

    return pl.kernel(
        body, mesh=mesh,
        compiler_params=pltpu.CompilerParams(needs_layout_passes=False),
        out_type=jax.ShapeDtypeStruct((n, kk), F32),
        scratch_types=[pltpu.VMEM((tb, d), F32), pltpu.VMEM((tb, kk), jnp.int32),
                       pltpu.VMEM((tb, kk), F32),
                       pltpu.VMEM((SC_LANES, d), F32), pltpu.VMEM((SC_LANES, d), F32),
                       pltpu.VMEM((SC_LANES * SC_LANES,), F32),
                       pltpu.SemaphoreType.DMA, pltpu.SemaphoreType.DMA],
        name="peer_act",
    )(hn, idx, u)


def _peer_out(w, idx, v, cb=16, unit_rows=2 * SC_LANES):
    n, kk = w.shape
    d = v.shape[1]
    tpw = n // SC_WORKERS
    tb = _sc_batch(tpw)
    upt = kk // unit_rows
    nchunk = d // SC_LANES
    mesh = plsc.VectorSubcoreMesh(core_axis_name="c", subcore_axis_name="s")

    def body(w_hbm, idx_hbm, v_hbm, out_hbm, w_v, idx_v, out_v, rows0, rows1, sem0, sem1):
        wid = lax.axis_index("s") * 2 + lax.axis_index("c")
        zero = jnp.zeros((SC_LANES,), F32)

        def compute(tl, g, r_ref):
            tls = jnp.full((SC_LANES,), tl, jnp.int32)

            def cb_body(ci, carry):
                c0 = ci * (cb * SC_LANES)
                accs = tuple(out_v[tl, pl.ds(c0 + j * SC_LANES, SC_LANES)] for j in range(cb))

                def rbody(r, accs):
                    wr = plsc.load_gather(w_v, [tls, jnp.full((SC_LANES,), g * unit_rows + r, jnp.int32)])
                    return tuple(accs[j] + wr * r_ref[r, pl.ds(c0 + j * SC_LANES, SC_LANES)]
                                 for j in range(cb))

                accs = lax.fori_loop(0, unit_rows, rbody, accs)
                for j in range(cb):
                    out_v[tl, pl.ds(c0 + j * SC_LANES, SC_LANES)] = accs[j]
                return carry

            lax.fori_loop(0, nchunk // cb, cb_body, 0)

        def batch(b, carry):
            t0 = wid * tpw + b * tb
            pltpu.sync_copy(w_hbm.at[pl.ds(t0, tb)], w_v)
            pltpu.sync_copy(idx_hbm.at[pl.ds(t0, tb)], idx_v)

            def zbody(i, carry):
                out_v[i // nchunk, pl.ds((i % nchunk) * SC_LANES, SC_LANES)] = zero
                return carry

            lax.fori_loop(0, tb * nchunk, zbody, 0)
            _sc_pipeline(idx_v, v_hbm, (rows0, rows1), (sem0, sem1), tb * upt, upt, compute)
            pltpu.sync_copy(out_v, out_hbm.at[pl.ds(t0, tb)])
            return carry

        lax.fori_loop(0, tpw // tb, batch, 0)

    return pl.kernel(
        body, mesh=mesh,
        compiler_params=pltpu.CompilerParams(needs_layout_passes=False),
        out_type=jax.ShapeDtypeStruct((n, d), F32),
        scratch_types=[pltpu.VMEM((tb, kk), F32), pltpu.VMEM((tb, kk), jnp.int32),
                       pltpu.VMEM((tb, d), F32),
                       pltpu.VMEM((unit_rows, d), F32), pltpu.VMEM((unit_rows, d), F32),
                       pltpu.SemaphoreType.DMA, pltpu.SemaphoreType.DMA],
        name="peer_out",
    )(w, idx, v)


SUBLANES = 8
TC_PEER_TOKENS = 8
ISSUE_UNROLL = 8


def _gelu(a):
    return a * (lax.erf(a / np.sqrt(2.0).astype(np.float32)) + 1.0) / 2.0


def _peer_tc_body(idx_ref, x_ref, g_ref, u_hbm, v_hbm, o_ref, ubuf, vbuf, sem_u, sem_v):
    tokens = x_ref.shape[0]
    kk = g_ref.shape[1]
    n_rows = tokens * kk

    def row_copy(table, buf, sem, expert, j):
        src = table.at[pl.ds(pl.multiple_of(expert * SUBLANES, SUBLANES), SUBLANES)]
        return pltpu.make_async_copy(src, buf.at[pl.ds(j * SUBLANES, SUBLANES)], sem)

    def issue(i, carry):
        for r in range(ISSUE_UNROLL):
            j = i * ISSUE_UNROLL + r
            expert = idx_ref[j]
            row_copy(u_hbm, ubuf, sem_u, expert, j).start()
            row_copy(v_hbm, vbuf, sem_v, expert, j).start()
        return carry

    lax.fori_loop(0, n_rows // ISSUE_UNROLL, issue, 0)

    def wait_all(table, buf, sem):
        pltpu.make_async_copy(table.at[pl.ds(0, n_rows * SUBLANES)], buf, sem).wait()

    def rows_of(buf, t, s):
        return buf[pl.ds(t * kk * SUBLANES + s, kk, stride=SUBLANES), :]

    wait_all(u_hbm, ubuf, sem_u)
    acts = []
    for t in range(tokens):
        acc = rows_of(ubuf, t, 0) * x_ref[t:t + 1, 0:LANES]
        for s in range(1, SUBLANES):
            acc = acc + rows_of(ubuf, t, s) * x_ref[t:t + 1, s * LANES:(s + 1) * LANES]
        acts.append(jnp.sum(acc, axis=1, keepdims=True))
    w = g_ref[0] * _gelu(jnp.concatenate(acts, axis=1))

    wait_all(v_hbm, vbuf, sem_v)
    for t in range(tokens):
        wt = w[:, t:t + 1]
        for s in range(SUBLANES):
            o_ref[t:t + 1, s * LANES:(s + 1) * LANES] = jnp.sum(rows_of(vbuf, t, s) * wt, axis=0, keepdims=True)


def _peer_tc(hn, idx, gate_t, u_tiles, v_tiles):
    m, d = hn.shape
    kk = idx.shape[1]
    tb = TC_PEER_TOKENS
    assert d == SUBLANES * LANES and m % tb == 0
    g3 = gate_t.reshape(kk, m // tb, tb).transpose(1, 0, 2)
    rows = tb * kk * SUBLANES
    return pl.pallas_call(
        _peer_tc_body,
        grid=(m // tb,),
        in_specs=[pl.BlockSpec((tb * kk,), lambda i: (i,), memory_space=pltpu.SMEM),
                  pl.BlockSpec((tb, d), lambda i: (i, 0)),
                  pl.BlockSpec((1, kk, tb), lambda i: (i, 0, 0)),
                  pl.BlockSpec(memory_space=pl.ANY),
                  pl.BlockSpec(memory_space=pl.ANY)],
        out_specs=pl.BlockSpec((tb, d), lambda i: (i, 0)),
        out_shape=jax.ShapeDtypeStruct((m, d), F32),
        scratch_shapes=[pltpu.VMEM((rows, LANES), F32), pltpu.VMEM((rows, LANES), F32),
                        pltpu.SemaphoreType.DMA, pltpu.SemaphoreType.DMA],
        compiler_params=pltpu.CompilerParams(dimension_semantics=("arbitrary",),
                                             vmem_limit_bytes=VMEM_LIMIT),
        name="peer_tc",
    )(idx.reshape(m * kk), hn, g3, u_tiles, v_tiles)


def _gelu_gate_body(a_ref, g_ref, o_ref):
    o_ref[...] = g_ref[...] * _gelu(a_ref[...])


def _gelu_gate(act, gate, tm=SHARE_GRANULE):
    n, kk = act.shape
    tm = min(tm, n)
    assert n % tm == 0
    spec = pl.BlockSpec((tm, kk), lambda i: (i, 0))
    return pl.pallas_call(
        _gelu_gate_body, grid=(n // tm,), in_specs=[spec, spec], out_specs=spec,
        out_shape=jax.ShapeDtypeStruct((n, kk), F32),
        compiler_params=pltpu.CompilerParams(dimension_semantics=("parallel",)),
        name="gelu_gate",
    )(act, gate)


def _final_body(h_ref, pa_ref, pb_ref, g_ref, o_ref, *, a_blocks):
    peer = jnp.where(pl.program_id(0) < a_blocks, pa_ref[...], pb_ref[...])
    h = h_ref[...] + peer
    o_ref[...] = h * lax.rsqrt(jnp.mean(h * h, axis=-1, keepdims=True) + EPS) * g_ref[...]


def _final(h1, peer_a, peer_b, g, tm=256):
    n, d = h1.shape
    ma, mb = peer_a.shape[0], peer_b.shape[0]
    assert ma + mb == n and ma % tm == 0 and mb % tm == 0 and ma > 0 and mb > 0
    a_blocks = ma // tm
    spec = pl.BlockSpec((tm, d), lambda i: (i, 0))
    return pl.pallas_call(
        functools.partial(_final_body, a_blocks=a_blocks), grid=(n // tm,),
        in_specs=[spec,
                  pl.BlockSpec((tm, d), lambda i: (jnp.minimum(i, a_blocks - 1), 0)),
                  pl.BlockSpec((tm, d), lambda i: (jnp.maximum(i - a_blocks, 0), 0)),
                  pl.BlockSpec((1, d), lambda i: (0, 0))],
        out_specs=spec,
        out_shape=jax.ShapeDtypeStruct((n, d), F32),
        compiler_params=pltpu.CompilerParams(dimension_semantics=("parallel",)),
        name="final_norm",
    )(h1, peer_a, peer_b, g)


def _tc_share(n_tokens):
    m = int(n_tokens * TC_SHARE) // SHARE_GRANULE * SHARE_GRANULE
    return min(max(m, SHARE_GRANULE), n_tokens - SHARE_GRANULE)


def kernel(x, norm1_g, w_in, hg_lower_logits, hg_norm_g, gla_w_gate_up, gla_b_gate, gla_norm_g,
           w_out, norm2_g, peer_w_q, peer_sub_keys, peer_u, peer_v, norm_f_g):
    batch, seq, d = x.shape
    depth = w_in.shape[0]
    assert depth == 1, "single-layer block"
    layer = 0

    w = w_in[layer]
    pad = LANES - GLA_GATE_RANK
    w_r = jnp.concatenate(
        [w[:, :3072], w[:, 3072:3072 + GLA_GATE_RANK], jnp.zeros((d, pad), w.dtype),
         w[:, 3072 + GLA_GATE_RANK:]], axis=1).astype(BF16)
    wgu = jnp.concatenate([gla_w_gate_up[layer], jnp.zeros((pad, GLA_KEY), F32)], axis=0).astype(BF16)

    wo = w_out[layer].astype(BF16)
    wq = peer_w_q[layer].astype(BF16)
    keys = peer_sub_keys[layer].reshape(PEER_HEADS * 2, PEER_N_KEYS, PEER_HALF).astype(BF16)
    n_experts = peer_u.shape[1]
    u_tiles = peer_u[layer].reshape(n_experts * SUBLANES, LANES)
    v_tiles = peer_v[layer].reshape(n_experts * SUBLANES, LANES)

    groups = PIPELINE_GROUPS if batch % PIPELINE_GROUPS == 0 else 1
    bpg = batch // groups
    m_tc = _tc_share(bpg * seq)
    gf = norm_f_g.reshape(1, d)

    def after(value, anchor):
        value, _ = lax.optimization_barrier((value, anchor))
        return value

    h1s, tc_ins, acts, sc_ins = [], [], [], []
    for gi in range(groups):
        h = x[gi * bpg:(gi + 1) * bpg].reshape(bpg * seq, d)
        proj = _inproj(h, norm1_g[layer].reshape(1, d), w_r)
        mixed = _mixers(proj, bpg, seq, hg_lower_logits, hg_norm_g[layer].reshape(1, HEAD_V), wgu,
                        gla_b_gate[layer].reshape(1, GLA_KEY), gla_norm_g[layer].reshape(1, HEAD_V))
        h1, hn, q3 = _outproj(h, mixed, wo, norm2_g[layer].reshape(1, d), wq)
        idx_t, gate_t = _route(q3, keys)
        idx = idx_t.T
        h1s.append(h1)
        tc_ins.append((hn[:m_tc], idx[:m_tc], gate_t[:, :m_tc]))
        sc_ins.append((idx[m_tc:], gate_t[:, m_tc:].T))
        acts.append(_peer_act(hn[m_tc:], idx[m_tc:], peer_u[layer]))

    def sc_out(gi, anchor):
        wts = _gelu_gate(after(acts[gi], anchor), sc_ins[gi][1])
        return wts, _peer_out(wts, sc_ins[gi][0], peer_v[layer])

    rows, peer_sc = [None] * groups, [None] * groups
    rows[0] = _peer_tc(*after(tc_ins[0], sc_ins[-1][0]), u_tiles, v_tiles)
    anchor = rows[0]
    for gi in range(groups - 1):
        anchor, peer_sc[gi] = sc_out(gi, rows[0])
    if groups > 1:
        rows[1] = _peer_tc(*after(tc_ins[1], anchor), u_tiles, v_tiles)
        anchor = rows[1]
    anchor, peer_sc[groups - 1] = sc_out(groups - 1, anchor)
    for gi in range(2, groups):
        rows[gi] = _peer_tc(*after(tc_ins[gi], anchor), u_tiles, v_tiles)
        anchor = rows[gi]
    outs = [_final(h1s[gi], rows[gi], peer_sc[gi], gf) for gi in range(groups)]
    return jnp.concatenate(outs, axis=0).reshape(batch, seq, d)
```

```python
import functools

import jax
import jax.numpy as jnp
import numpy as np
from jax import lax
from jax.experimental import pallas as pl
from jax.experimental.pallas import tpu as pltpu
from jax.experimental.pallas import tpu_sc as plsc

F32 = jnp.float32
BF16 = jnp.bfloat16
EPS = 1e-6

D_MODEL = 1024
HG_HEADS = 4
GLA_HEADS = 4
HEAD_V = 128
HG_KEY = 512
GLA_KEY = 256
GLA_HEAD_K = 64
GLA_GATE_RANK = 16
GLA_GATE_NORMALIZER = 16.0
CHUNK = 64
LANES = 128
PEER_HEADS = 8
PEER_N_KEYS = 128
PEER_HALF = 128
PEER_TOPK = 16

OFF_HQ, OFF_HF, OFF_HI, OFF_HGATE = 0, 512, 1024, 1536
OFF_GQ, OFF_GK, OFF_GV, OFF_GLOW, OFF_GGATE = 2048, 2304, 2560, 3072, 3200
IN_WIDTH_PADDED = 3712

VMEM_LIMIT = 48 * 1024 * 1024

SC_LANES = 16
SC_WORKERS = 32
PIPELINE_GROUPS = 4
TC_SHARE = 0.21875
SHARE_GRANULE = 256


def _nt(a, b):
    return lax.dot_general(a, b, (((1,), (1,)), ((), ())), preferred_element_type=F32)


def _tn(a, b):
    return lax.dot_general(a, b, (((0,), (0,)), ((), ())), preferred_element_type=F32)


def _inproj_body(x_ref, g_ref, w_ref, o_ref):
    x = x_ref[...]
    xn = x * lax.rsqrt(jnp.mean(x * x, axis=-1, keepdims=True) + EPS) * g_ref[...]
    o_ref[...] = jnp.dot(xn.astype(BF16), w_ref[...], preferred_element_type=F32)


def _inproj(x2, row0, n, g, w, tm=256):
    d = x2.shape[1]
    wd = w.shape[1]
    assert row0 % tm == 0 and n % tm == 0
    return pl.pallas_call(
        _inproj_body,
        grid=(n // tm,),
        in_specs=[pl.BlockSpec((tm, d), lambda i: (i + row0 // tm, 0)),
                  pl.BlockSpec((1, d), lambda i: (0, 0)),
                  pl.BlockSpec((d, wd), lambda i: (0, 0))],
        out_specs=pl.BlockSpec((tm, wd), lambda i: (i, 0)),
        out_shape=jax.ShapeDtypeStruct((n, wd), F32),
        compiler_params=pltpu.CompilerParams(dimension_semantics=("parallel",),
                                             vmem_limit_bytes=VMEM_LIMIT),
        name="inproj",
    )(x2, g, w)


def _level_constants():
    c = CHUNK
    mats = [np.tril(np.ones((c, c), np.float32))]
    level = np.full((c, c), -1, np.int32)
    b, lvl = c // 2, 0
    while b >= 1:
        m = np.zeros((c, c), np.float32)
        for s in range(0, c, 2 * b):
            mid = s + b
            for i in range(mid, s + 2 * b):
                m[i, mid:i + 1] = 1.0
                level[i, s:mid] = lvl
            for j in range(s, mid):
                m[j, j + 1:mid] = 1.0
        mats.append(m)
        b //= 2
        lvl += 1
    level[np.arange(c), np.arange(c)] = lvl
    return np.concatenate(mats, axis=0), level, lvl


_SEG_MATS, _LEVEL_MAP, _N_LEVELS = _level_constants()


def _split3(a):
    hi = a.astype(BF16)
    r = a - hi.astype(F32)
    mid = r.astype(BF16)
    lo = (r - mid.astype(F32)).astype(BF16)
    return jnp.concatenate([hi, mid, lo], axis=1)


def _gla_chunk(q, k, g, v, st, seg, level):
    c = CHUNK
    ex3 = jnp.dot(seg, _split3(g), preferred_element_type=F32)
    ex = ex3[:, 0:LANES] + ex3[:, LANES:2 * LANES] + ex3[:, 2 * LANES:3 * LANES]
    cum = ex[0:c]
    scores = jnp.where(level == _N_LEVELS, _nt(q.astype(BF16), k.astype(BF16)), 0.0)
    for l in range(_N_LEVELS):
        e = jnp.exp(ex[c * (l + 1):c * (l + 2)])
        p = _nt((q * e).astype(BF16), (k * e).astype(BF16))
        scores = jnp.where(level == l, p, scores)
    last = cum[c - 1:c, :]
    qd = (q * jnp.exp(cum)).astype(BF16)
    kd = (k * jnp.exp(last - cum)).astype(BF16)
    vb = v.astype(BF16)
    o = jnp.dot(scores.astype(BF16), vb, preferred_element_type=F32) + _nt(qd, st.astype(BF16))
    st_new = st * jnp.exp(last) + _tn(vb, kd)
    return o, st_new


def _head_out(o, gain, gate):
    on = o * lax.rsqrt(jnp.mean(o * o, axis=-1, keepdims=True) + EPS) * gain
    return on * (gate * jax.nn.sigmoid(gate))


def _mixer_body(p_ref, seg_ref, level_ref, lbl_ref, hgn_ref, wgu_ref, bg_ref, ggn_ref,
                o_ref, st_ref, *, chunks):
    @pl.when(pl.program_id(1) == 0)
    def _():
        st_ref[...] = jnp.zeros_like(st_ref)

    seg = seg_ref[...]
    level = level_ref[...]
    logits = lbl_ref[...]
    ez = jnp.exp(logits - jnp.max(logits, axis=0, keepdims=True))
    lb = ez[0:1, :] / jnp.sum(ez, axis=0, keepdims=True)
    lane = lax.broadcasted_iota(jnp.int32, (CHUNK, LANES), 1)

    def chunk_body(ci, carry):
        rows = pl.ds(pl.multiple_of(ci * CHUNK, CHUNK), CHUNK)

        def col(off):
            return p_ref[rows, pl.ds(off, LANES)]

        for h in range(HG_HEADS):
            hq = col(OFF_HQ + h * LANES)
            q = hq * jax.nn.sigmoid(hq)
            lbh = lb[:, h * LANES:(h + 1) * LANES]
            forget = lbh + (1.0 - lbh) * jax.nn.sigmoid(col(OFF_HF + h * LANES))
            o, st = _gla_chunk(q, 1.0 - forget, jnp.log(forget), col(OFF_HI + h * LANES),
                               st_ref[h], seg, level)
            st_ref[h] = st
            o_ref[rows, pl.ds(h * HEAD_V, HEAD_V)] = _head_out(
                o, hgn_ref[...], col(OFF_HGATE + h * HEAD_V)).astype(o_ref.dtype)

        zg = jnp.dot(col(OFF_GLOW).astype(BF16), wgu_ref[...], preferred_element_type=F32) + bg_ref[...]
        log_g = (jnp.minimum(zg, 0.0) - jnp.log(1.0 + jnp.exp(-jnp.abs(zg)))) / GLA_GATE_NORMALIZER
        for h in range(GLA_HEADS):
            pair, half = h // 2, h % 2
            q = col(OFF_GQ + pair * LANES) * (GLA_HEAD_K ** -0.5)
            in_head = (lane >= half * GLA_HEAD_K) & (lane < (half + 1) * GLA_HEAD_K)
            k = jnp.where(in_head, col(OFF_GK + pair * LANES), 0.0)
            g = log_g[:, pair * LANES:(pair + 1) * LANES]
            o, st = _gla_chunk(q, k, g, col(OFF_GV + h * HEAD_V), st_ref[HG_HEADS + h], seg, level)
            st_ref[HG_HEADS + h] = st
            o_ref[rows, pl.ds((HG_HEADS + h) * HEAD_V, HEAD_V)] = _head_out(
                o, ggn_ref[...], col(OFF_GGATE + h * HEAD_V)).astype(o_ref.dtype)
        return carry

    lax.fori_loop(0, chunks, chunk_body, 0)


def _mixers(proj, batch, seq, lb_logits, hg_norm_g, wgu, bg, gla_norm_g, tt=512):
    n = batch * seq
    steps = seq // tt
    heads = HG_HEADS + GLA_HEADS
    const = lambda shape: pl.BlockSpec(shape, lambda b, t: (0,) * len(shape))
    return pl.pallas_call(
        functools.partial(_mixer_body, chunks=tt // CHUNK),
        grid=(batch, steps),
        in_specs=[pl.BlockSpec((tt, IN_WIDTH_PADDED), lambda b, t: (b * steps + t, 0)),
                  const(_SEG_MATS.shape), const(_LEVEL_MAP.shape),
                  const(lb_logits.shape), const((1, HEAD_V)),
                  const(wgu.shape), const(bg.shape), const((1, HEAD_V))],
        out_specs=pl.BlockSpec((tt, heads * HEAD_V), lambda b, t: (b * steps + t, 0)),
        out_shape=jax.ShapeDtypeStruct((n, heads * HEAD_V), BF16),
        scratch_shapes=[pltpu.VMEM((heads, HEAD_V, LANES), F32)],
        compiler_params=pltpu.CompilerParams(dimension_semantics=("parallel", "arbitrary"),
                                             vmem_limit_bytes=VMEM_LIMIT),
        name="mixers",
    )(proj, jnp.asarray(_SEG_MATS, BF16), jnp.asarray(_LEVEL_MAP), lb_logits, hg_norm_g,
      wgu, bg, gla_norm_g)


def _outproj_body(x_ref, m_ref, wo_ref, g_ref, wq_ref, h_ref, hn_ref, q_ref):
    h = x_ref[...] + jnp.dot(m_ref[...], wo_ref[...], preferred_element_type=F32)
    h_ref[...] = h
    hn = h * lax.rsqrt(jnp.mean(h * h, axis=-1, keepdims=True) + EPS) * g_ref[...]
    hn_ref[...] = hn
    q = jnp.dot(hn.astype(BF16), wq_ref[...], preferred_element_type=F32)
    for j in range(q_ref.shape[0]):
        q_ref[j] = q[:, j * PEER_HALF:(j + 1) * PEER_HALF]


def _outproj(x2, row0, mixed, wo, g2, wq, tm=256):
    n, d = mixed.shape[0], x2.shape[1]
    nq = wq.shape[1] // PEER_HALF
    assert row0 % tm == 0 and n % tm == 0
    return pl.pallas_call(
        _outproj_body,
        grid=(n // tm,),
        in_specs=[pl.BlockSpec((tm, d), lambda i: (i + row0 // tm, 0)),
                  pl.BlockSpec((tm, mixed.shape[1]), lambda i: (i, 0)),
                  pl.BlockSpec(wo.shape, lambda i: (0, 0)),
                  pl.BlockSpec((1, d), lambda i: (0, 0)),
                  pl.BlockSpec(wq.shape, lambda i: (0, 0))],
        out_specs=[pl.BlockSpec((tm, d), lambda i: (i, 0)),
                   pl.BlockSpec((tm, d), lambda i: (i, 0)),
                   pl.BlockSpec((nq, tm, PEER_HALF), lambda i: (0, i, 0))],
        out_shape=[jax.ShapeDtypeStruct((n, d), F32),
                   jax.ShapeDtypeStruct((n, d), F32),
                   jax.ShapeDtypeStruct((nq, n, PEER_HALF), F32)],
        compiler_params=pltpu.CompilerParams(dimension_semantics=("parallel",),
                                             vmem_limit_bytes=VMEM_LIMIT),
        name="outproj",
    )(x2, mixed, wo, g2, wq)


def _candidate_constants():
    k = PEER_TOPK
    pos, valid = [], []
    for j in range(k):
        pos.append(j); valid.append(True)
    for i in range(1, 8):
        for j in range(8):
            pos.append(i * k + j); valid.append((i + 1) * (j + 1) <= k)
    for i in range(8, k):
        pos.append(i * k); valid.append(True)
    return np.asarray(pos, np.float32), np.asarray(valid, bool)


_CAND_POS, _CAND_VALID = _candidate_constants()
_N_CAND = _CAND_POS.shape[0]


def _topk_rows(ss, pos, extras, k):
    t = ss[0].shape[1]
    slot = lax.broadcasted_iota(jnp.int32, (k, t), 0)

    def body(it, carry):
        out = []
        for (s, vals, poss, exts), extra in zip(carry, extras):
            m = jnp.max(s, axis=0, keepdims=True)
            p = jnp.min(jnp.where(s == m, pos, 1e9), axis=0, keepdims=True)
            hit = pos == p
            vals = jnp.where(slot == it, m, vals)
            poss = jnp.where(slot == it, p, poss)
            if extra is not None:
                x = jnp.max(jnp.where(hit, extra, -1.0), axis=0, keepdims=True)
                exts = jnp.where(slot == it, x, exts)
            out.append((jnp.where(hit, -jnp.inf, s), vals, poss, exts))
        return tuple(out)

    z = jnp.zeros((k, t), F32)
    res = lax.fori_loop(0, k, body, tuple((s, z, z, z) for s in ss))
    return [(vals, poss, exts) for _, vals, poss, exts in res]


def _route_body(q_ref, keys_ref, cpos_ref, cvalid_ref, idx_ref, gate_ref):
    k = PEER_TOPK
    n_tiles = q_ref.shape[1] // LANES
    key_pos = lax.broadcasted_iota(jnp.int32, (PEER_N_KEYS, LANES), 0).astype(F32)
    cpos = cpos_ref[...]
    cvalid = cvalid_ref[...] > 0.5
    scores = [_nt(keys_ref[p], q_ref[p, pl.ds(lt * LANES, LANES), :].astype(BF16))
              for lt in range(n_tiles) for p in range(2)]
    tops = _topk_rows(scores, key_pos, [None] * len(scores), k)
    cands, experts = [], []
    for lt in range(n_tiles):
        (v0, i0, _), (v1, i1, _) = tops[2 * lt], tops[2 * lt + 1]
        vals = [v0[0:1] + v1]
        exps = [i0[0:1] * PEER_N_KEYS + i1]
        for i in range(1, 8):
            vals.append(v0[i:i + 1] + v1[0:8])
            exps.append(i0[i:i + 1] * PEER_N_KEYS + i1[0:8])
        vals.append(v0[8:k] + v1[0:1])
        exps.append(i0[8:k] * PEER_N_KEYS + i1[0:1])
        cands.append(jnp.where(cvalid, jnp.concatenate(vals, axis=0), -jnp.inf))
        experts.append(jnp.concatenate(exps, axis=0))
    picks = _topk_rows(cands, cpos, experts, k)
    for lt in range(n_tiles):
        best, _, expert = picks[lt]
        cols = pl.ds(lt * LANES, LANES)
        e = jnp.exp(best - jnp.max(best, axis=0, keepdims=True))
        gate_ref[:, cols] = e / jnp.sum(e, axis=0, keepdims=True)
        idx_ref[:, cols] = expert.astype(jnp.int32)


def _route(q3, keys, tb=512):
    nq, n, _ = q3.shape
    heads = nq // 2
    cpos = jnp.broadcast_to(jnp.asarray(_CAND_POS)[:, None], (_N_CAND, LANES))
    cvalid = jnp.broadcast_to(jnp.asarray(_CAND_VALID, F32)[:, None], (_N_CAND, LANES))
    return pl.pallas_call(
        _route_body,
        grid=(n // tb, heads),
        in_specs=[pl.BlockSpec((2, tb, PEER_HALF), lambda i, h: (h, i, 0)),
                  pl.BlockSpec((2, PEER_N_KEYS, PEER_HALF), lambda i, h: (h, 0, 0)),
                  pl.BlockSpec((_N_CAND, LANES), lambda i, h: (0, 0)),
                  pl.BlockSpec((_N_CAND, LANES), lambda i, h: (0, 0))],
        out_specs=[pl.BlockSpec((PEER_TOPK, tb), lambda i, h: (h, i)),
                   pl.BlockSpec((PEER_TOPK, tb), lambda i, h: (h, i))],
        out_shape=[jax.ShapeDtypeStruct((heads * PEER_TOPK, n), jnp.int32),
                   jax.ShapeDtypeStruct((heads * PEER_TOPK, n), F32)],
        compiler_params=pltpu.CompilerParams(dimension_semantics=("parallel", "parallel"),
                                             vmem_limit_bytes=VMEM_LIMIT),
        name="route",
    )(q3, keys, cpos, cvalid)


def _sc_pipeline(idx_v, table_hbm, rows, sems, n_units, units_per_token, compute):
    parts = rows[0].shape[0] // SC_LANES
    assert n_units % 2 == 0

    def start(unit, slot):
        tl = unit // units_per_token
        g = unit % units_per_token
        for part in range(parts):
            iv = idx_v[tl, pl.ds((g * parts + part) * SC_LANES, SC_LANES)]
            pltpu.async_copy(table_hbm.at[iv], rows[slot].at[pl.ds(part * SC_LANES, SC_LANES)], sems[slot])

    def wait(slot):
        iv = idx_v[0, pl.ds(0, SC_LANES)]
        for part in range(parts):
            pltpu.make_async_copy(table_hbm.at[iv], rows[slot].at[pl.ds(part * SC_LANES, SC_LANES)],
                                  sems[slot]).wait()

    start(0, 0)

    def pair(i, carry):
        u0 = 2 * i
        start(u0 + 1, 1)
        wait(0)
        compute(u0 // units_per_token, u0 % units_per_token, rows[0])

        @pl.when(u0 + 2 < n_units)
        def _():
            start(u0 + 2, 0)

        wait(1)
        compute((u0 + 1) // units_per_token, (u0 + 1) % units_per_token, rows[1])
        return carry

    lax.fori_loop(0, n_units // 2, pair, 0)


def _sc_batch(tokens_per_worker):
    for tb in (40, 32, 24, 16, 8):
        if tokens_per_worker % tb == 0:
            return tb
    raise ValueError(f"tokens per SparseCore worker ({tokens_per_worker}) must be a multiple of 8")


def _peer_act(hn, idx, u):
    n, d = hn.shape
    kk = idx.shape[1]
    tpw = n // SC_WORKERS
    tb = _sc_batch(tpw)
    upt = kk // SC_LANES
    nchunk = d // SC_LANES
    mesh = plsc.VectorSubcoreMesh(core_axis_name="c", subcore_axis_name="s")

    def body(x_hbm, idx_hbm, u_hbm, out_hbm, x_v, idx_v, act_v, rows0, rows1, tmp_v, sem0, sem1):
        wid = lax.axis_index("s") * 2 + lax.axis_index("c")
        lane_row = lax.iota(jnp.int32, SC_LANES) * SC_LANES

        def compute(tl, g, r_ref):
            def cbody(c, accs):
                xv = x_v[tl, pl.ds(c * SC_LANES, SC_LANES)]
                return tuple(accs[r] + r_ref[r, pl.ds(c * SC_LANES, SC_LANES)] * xv
                             for r in range(SC_LANES))
            accs = lax.fori_loop(0, nchunk, cbody,
                                 tuple(jnp.zeros((SC_LANES,), F32) for _ in range(SC_LANES)))
            for r in range(SC_LANES):
                tmp_v[pl.ds(r * SC_LANES, SC_LANES)] = accs[r]
            tot = plsc.load_gather(tmp_v, [lane_row])
            for j in range(1, SC_LANES):
                tot = tot + plsc.load_gather(tmp_v, [lane_row + j])
            act_v[tl, pl.ds(g * SC_LANES, SC_LANES)] = tot

        def batch(b, carry):
            t0 = wid * tpw + b * tb
            pltpu.sync_copy(x_hbm.at[pl.ds(t0, tb)], x_v)
            pltpu.sync_copy(idx_hbm.at[pl.ds(t0, tb)], idx_v)
            _sc_pipeline(idx_v, u_hbm, (rows0, rows1), (sem0, sem1), tb * upt, upt, compute)
            pltpu.sync_copy(act_v, out_hbm.at[pl.ds(t0, tb)])
            return carry

        lax.fori_loop(0, tpw // tb, batch, 0)

    return pl.kernel(
        body, mesh=mesh,
        compiler_params=pltpu.CompilerParams(needs_layout_passes=False),
        out_type=jax.ShapeDtypeStruct((n, kk), F32),
        scratch_types=[pltpu.VMEM((tb, d), F32), pltpu.VMEM((tb, kk), jnp.int32),
                       pltpu.VMEM((tb, kk), F32),
                       pltpu.VMEM((SC_LANES, d), F32), pltpu.VMEM((SC_LANES, d), F32),
                       pltpu.VMEM((SC_LANES * SC_LANES,), F32),
                       pltpu.SemaphoreType.DMA, pltpu.SemaphoreType.DMA],
        name="peer_act",
    )(hn, idx, u)


def _peer_out(w, idx, v, cb=16, unit_rows=2 * SC_LANES):
    n, kk = w.shape
    d = v.shape[1]
    tpw = n // SC_WORKERS
    tb = _sc_batch(tpw)
    upt = kk // unit_rows
    nchunk = d // SC_LANES
    mesh = plsc.VectorSubcoreMesh(core_axis_name="c", subcore_axis_name="s")

    def body(w_hbm, idx_hbm, v_hbm, out_hbm, w_v, idx_v, out_v, rows0, rows1, sem0, sem1):
        wid = lax.axis_index("s") * 2 + lax.axis_index("c")
        zero = jnp.zeros((SC_LANES,), F32)

        def compute(tl, g, r_ref):
            tls = jnp.full((SC_LANES,), tl, jnp.int32)

            def cb_body(ci, carry):
                c0 = ci * (cb * SC_LANES)
                accs = tuple(out_v[tl, pl.ds(c0 + j * SC_LANES, SC_LANES)] for j in range(cb))

                def rbody(r, accs):
                    wr = plsc.load_gather(w_v, [tls, jnp.full((SC_LANES,), g * unit_rows + r, jnp.int32)])
                    return tuple(accs[j] + wr * r_ref[r, pl.ds(c0 + j * SC_LANES, SC_LANES)]
                                 for j in range(cb))

                accs = lax.fori_loop(0, unit_rows, rbody, accs)
                for j in range(cb):
                    out_v[tl, pl.ds(c0 + j * SC_LANES, SC_LANES)] = accs[j]
                return carry

            lax.fori_loop(0, nchunk // cb, cb_body, 0)

        def batch(b, carry):
            t0 = wid * tpw + b * tb
            pltpu.sync_copy(w_hbm.at[pl.ds(t0, tb)], w_v)
            pltpu.sync_copy(idx_hbm.at[pl.ds(t0, tb)], idx_v)

            def zbody(i, carry):
                out_v[i // nchunk, pl.ds((i % nchunk) * SC_LANES, SC_LANES)] = zero
                return carry

            lax.fori_loop(0, tb * nchunk, zbody, 0)
            _sc_pipeline(idx_v, v_hbm, (rows0, rows1), (sem0, sem1), tb * upt, upt, compute)
            pltpu.sync_copy(out_v, out_hbm.at[pl.ds(t0, tb)])
            return carry

        lax.fori_loop(0, tpw // tb, batch, 0)

    return pl.kernel(
        body, mesh=mesh,
        compiler_params=pltpu.CompilerParams(needs_layout_passes=False),
        out_type=jax.ShapeDtypeStruct((n, d), F32),
        scratch_types=[pltpu.VMEM((tb, kk), F32), pltpu.VMEM((tb, kk), jnp.int32),
                       pltpu.VMEM((tb, d), F32),
                       pltpu.VMEM((unit_rows, d), F32), pltpu.VMEM((unit_rows, d), F32),
                       pltpu.SemaphoreType.DMA, pltpu.SemaphoreType.DMA],
        name="peer_out",
    )(w, idx, v)


SUBLANES = 8
TC_PEER_TOKENS = 8
ISSUE_UNROLL = 8


def _gelu(a):
    return a * (lax.erf(a / np.sqrt(2.0).astype(np.float32)) + 1.0) / 2.0


def _peer_tc_body(idx_ref, idx_next_ref, x_ref, g_ref, u_hbm, v_hbm, o_ref, ubuf, vbuf, sem_u, sem_v):
    tokens = x_ref.shape[0]
    kk = g_ref.shape[1]
    n_rows = tokens * kk
    step = pl.program_id(0)
    slot = step % 2

    def row_copy(table, buf, sem, expert, j):
        src = table.at[pl.ds(pl.multiple_of(expert * SUBLANES, SUBLANES), SUBLANES)]
        return pltpu.make_async_copy(src, buf.at[pl.ds(j * SUBLANES, SUBLANES)], sem)

    def issue_block(ids_ref, into):
        def issue(i, carry):
            for r in range(ISSUE_UNROLL):
                j = i * ISSUE_UNROLL + r
                expert = ids_ref[j]
                row_copy(u_hbm, ubuf.at[into], sem_u.at[into], expert, j).start()
                row_copy(v_hbm, vbuf.at[into], sem_v.at[into], expert, j).start()
            return carry

        lax.fori_loop(0, n_rows // ISSUE_UNROLL, issue, 0)

    @pl.when(step == 0)
    def _():
        issue_block(idx_ref, 0)

    @pl.when(step + 1 < pl.num_programs(0))
    def _():
        issue_block(idx_next_ref, 1 - slot)

    def wait_all(table, buf, sem):
        pltpu.make_async_copy(table.at[pl.ds(0, n_rows * SUBLANES)], buf.at[slot], sem.at[slot]).wait()

    def rows_of(buf, t, s):
        return buf[slot, pl.ds(t * kk * SUBLANES + s, kk, stride=SUBLANES), :]

    wait_all(u_hbm, ubuf, sem_u)
    acts = []
    for t in range(tokens):
        acc = rows_of(ubuf, t, 0) * x_ref[t:t + 1, 0:LANES]
        for s in range(1, SUBLANES):
            acc = acc + rows_of(ubuf, t, s) * x_ref[t:t + 1, s * LANES:(s + 1) * LANES]
        acts.append(jnp.sum(acc, axis=1, keepdims=True))
    w = g_ref[0] * _gelu(jnp.concatenate(acts, axis=1))

    wait_all(v_hbm, vbuf, sem_v)
    for t in range(tokens):
        wt = w[:, t:t + 1]
        for s in range(SUBLANES):
            o_ref[t:t + 1, s * LANES:(s + 1) * LANES] = jnp.sum(rows_of(vbuf, t, s) * wt, axis=0, keepdims=True)


def _peer_tc(hn, idx, gate_t, u_tiles, v_tiles):
    m, d = hn.shape
    kk = idx.shape[1]
    tb = TC_PEER_TOKENS
    assert d == SUBLANES * LANES and m % tb == 0
    steps = m // tb
    g3 = gate_t.reshape(kk, steps, tb).transpose(1, 0, 2)
    rows = tb * kk * SUBLANES
    ids = idx.reshape(m * kk)
    return pl.pallas_call(
        _peer_tc_body,
        grid=(steps,),
        in_specs=[pl.BlockSpec((tb * kk,), lambda i: (i,), memory_space=pltpu.SMEM),
                  pl.BlockSpec((tb * kk,), lambda i: (jnp.minimum(i + 1, steps - 1),), memory_space=pltpu.SMEM),
                  pl.BlockSpec((tb, d), lambda i: (i, 0)),
                  pl.BlockSpec((1, kk, tb), lambda i: (i, 0, 0)),
                  pl.BlockSpec(memory_space=pl.ANY),
                  pl.BlockSpec(memory_space=pl.ANY)],
        out_specs=pl.BlockSpec((tb, d), lambda i: (i, 0)),
        out_shape=jax.ShapeDtypeStruct((m, d), F32),
        scratch_shapes=[pltpu.VMEM((2, rows, LANES), F32), pltpu.VMEM((2, rows, LANES), F32),
                        pltpu.SemaphoreType.DMA((2,)), pltpu.SemaphoreType.DMA((2,))],
        compiler_params=pltpu.CompilerParams(dimension_semantics=("arbitrary",),
                                             vmem_limit_bytes=VMEM_LIMIT),
        name="peer_tc",
    )(ids, ids, hn, g3, u_tiles, v_tiles)


def _gelu_gate_body(a_ref, g_ref, o_ref):
    o_ref[...] = g_ref[...] * _gelu(a_ref[...])


def _gelu_gate(act, gate, tm=SHARE_GRANULE):
    n, kk = act.shape
    tm = min(tm, n)
    assert n % tm == 0
    spec = pl.BlockSpec((tm, kk), lambda i: (i, 0))
    return pl.pallas_call(
        _gelu_gate_body, grid=(n // tm,), in_specs=[spec, spec], out_specs=spec,
        out_shape=jax.ShapeDtypeStruct((n, kk), F32),
        compiler_params=pltpu.CompilerParams(dimension_semantics=("parallel",)),
        name="gelu_gate",
    )(act, gate)


def _final_body(h_ref, pa_ref, pb_ref, g_ref, *rest, a_blocks):
    o_ref = rest[-1]
    peer = jnp.where(pl.program_id(0) < a_blocks, pa_ref[...], pb_ref[...])
    h = h_ref[...] + peer
    o_ref[...] = h * lax.rsqrt(jnp.mean(h * h, axis=-1, keepdims=True) + EPS) * g_ref[...]


def _final(h1, peer_a, peer_b, g, out, row0, n_total, tm=256):
    n, d = h1.shape
    ma, mb = peer_a.shape[0], peer_b.shape[0]
    assert ma + mb == n and ma % tm == 0 and mb % tm == 0 and ma > 0 and mb > 0 and row0 % tm == 0
    a_blocks = ma // tm
    spec = pl.BlockSpec((tm, d), lambda i: (i, 0))
    in_specs = [spec,
                pl.BlockSpec((tm, d), lambda i: (jnp.minimum(i, a_blocks - 1), 0)),
                pl.BlockSpec((tm, d), lambda i: (jnp.maximum(i - a_blocks, 0), 0)),
                pl.BlockSpec((1, d), lambda i: (0, 0))]
    args = [h1, peer_a, peer_b, g]
    aliases = {}
    if out is not None:
        in_specs.append(pl.BlockSpec(memory_space=pl.ANY))
        args.append(out)
        aliases = {4: 0}
    return pl.pallas_call(
        functools.partial(_final_body, a_blocks=a_blocks), grid=(n // tm,),
        in_specs=in_specs,
        out_specs=pl.BlockSpec((tm, d), lambda i: (i + row0 // tm, 0)),
        out_shape=jax.ShapeDtypeStruct((n_total, d), F32),
        input_output_aliases=aliases,
        compiler_params=pltpu.CompilerParams(dimension_semantics=("parallel",)),
        name="final_norm",
    )(*args)


def _tc_share(n_tokens):
    m = int(n_tokens * TC_SHARE) // SHARE_GRANULE * SHARE_GRANULE
    return min(max(m, SHARE_GRANULE), n_tokens - SHARE_GRANULE)


def kernel(x, norm1_g, w_in, hg_lower_logits, hg_norm_g, gla_w_gate_up, gla_b_gate, gla_norm_g,
           w_out, norm2_g, peer_w_q, peer_sub_keys, peer_u, peer_v, norm_f_g):
    batch, seq, d = x.shape
    depth = w_in.shape[0]
    assert depth == 1, "single-layer block"
    layer = 0

    w = w_in[layer]
    pad = LANES - GLA_GATE_RANK
    w_r = jnp.concatenate(
        [w[:, :3072], w[:, 3072:3072 + GLA_GATE_RANK], jnp.zeros((d, pad), w.dtype),
         w[:, 3072 + GLA_GATE_RANK:]], axis=1).astype(BF16)
    wgu = jnp.concatenate([gla_w_gate_up[layer], jnp.zeros((pad, GLA_KEY), F32)], axis=0).astype(BF16)

    wo = w_out[layer].astype(BF16)
    wq = peer_w_q[layer].astype(BF16)
    keys = peer_sub_keys[layer].reshape(PEER_HEADS * 2, PEER_N_KEYS, PEER_HALF).astype(BF16)
    n_experts = peer_u.shape[1]
    u_tiles = peer_u[layer].reshape(n_experts * SUBLANES, LANES)
    v_tiles = peer_v[layer].reshape(n_experts * SUBLANES, LANES)

    groups = PIPELINE_GROUPS if batch % PIPELINE_GROUPS == 0 else 1
    bpg = batch // groups
    ng = bpg * seq
    m_tc = _tc_share(ng)
    gf = norm_f_g.reshape(1, d)
    x2 = x.reshape(batch * seq, d)

    def after(value, anchor):
        value, _ = lax.optimization_barrier((value, anchor))
        return value

    h1s, tc_ins, acts, sc_ins = [], [], [], []
    for gi in range(groups):
        proj = _inproj(x2, gi * ng, ng, norm1_g[layer].reshape(1, d), w_r)
        mixed = _mixers(proj, bpg, seq, hg_lower_logits, hg_norm_g[layer].reshape(1, HEAD_V), wgu,
                        gla_b_gate[layer].reshape(1, GLA_KEY), gla_norm_g[layer].reshape(1, HEAD_V))
        h1, hn, q3 = _outproj(x2, gi * ng, mixed, wo, norm2_g[layer].reshape(1, d), wq)
        idx_t, gate_t = _route(q3, keys)
        idx = idx_t.T
        h1s.append(h1)
        tc_ins.append((hn[:m_tc], idx[:m_tc], gate_t[:, :m_tc]))
        sc_ins.append((idx[m_tc:], gate_t[:, m_tc:].T))
        acts.append(_peer_act(hn[m_tc:], idx[m_tc:], peer_u[layer]))

    def sc_out(gi, anchor):
        wts = _gelu_gate(after(acts[gi], anchor), sc_ins[gi][1])
        return wts, _peer_out(wts, sc_ins[gi][0], peer_v[layer])

    rows, peer_sc = [None] * groups, [None] * groups
    rows[0] = _peer_tc(*after(tc_ins[0], sc_ins[-1][0]), u_tiles, v_tiles)
    anchor = rows[0]
    for gi in range(groups - 1):
        anchor, peer_sc[gi] = sc_out(gi, rows[0])
    if groups > 1:
        rows[1] = _peer_tc(*after(tc_ins[1], anchor), u_tiles, v_tiles)
        anchor = rows[1]
    anchor, peer_sc[groups - 1] = sc_out(groups - 1, anchor)
    for gi in range(2, groups):
        rows[gi] = _peer_tc(*after(tc_ins[gi], anchor), u_tiles, v_tiles)
        anchor = rows[gi]
    out = None
    for gi in range(groups):
        out = _final(h1s[gi], rows[gi], peer_sc[gi], gf, out, gi * ng, batch * seq)
    return out.reshape(batch, seq, d)
```

```python
import functools

import jax
import jax.numpy as jnp
import numpy as np
from jax import lax
from jax.experimental import pallas as pl
from jax.experimental.pallas import tpu as pltpu
from jax.experimental.pallas import tpu_sc as plsc

F32 = jnp.float32
BF16 = jnp.bfloat16
EPS = 1e-6

D_MODEL = 1024
HG_HEADS = 4
GLA_HEADS = 4
HEAD_V = 128
HG_KEY = 512
GLA_KEY = 256
GLA_HEAD_K = 64
GLA_GATE_RANK = 16
GLA_GATE_NORMALIZER = 16.0
CHUNK = 64
LANES = 128
PEER_HEADS = 8
PEER_N_KEYS = 128
PEER_HALF = 128
PEER_TOPK = 16

OFF_HQ, OFF_HF, OFF_HI, OFF_HGATE = 0, 512, 1024, 1536
OFF_GQ, OFF_GK, OFF_GV, OFF_GLOW, OFF_GGATE = 2048, 2304, 2560, 3072, 3200
IN_WIDTH_PADDED = 3712

VMEM_LIMIT = 48 * 1024 * 1024

SC_LANES = 16
SC_WORKERS = 32
PIPELINE_GROUPS = 4
TC_SHARE = 0.25
SHARE_GRANULE = 256


def _nt(a, b):
    return lax.dot_general(a, b, (((1,), (1,)), ((), ())), preferred_element_type=F32)


def _tn(a, b):
    return lax.dot_general(a, b, (((0,), (0,)), ((), ())), preferred_element_type=F32)


def _inproj_body(x_ref, g_ref, w_ref, o_ref):
    x = x_ref[...]
    xn = x * lax.rsqrt(jnp.mean(x * x, axis=-1, keepdims=True) + EPS) * g_ref[...]
    o_ref[...] = jnp.dot(xn.astype(BF16), w_ref[...], preferred_element_type=F32)


def _inproj(x2, row0, n, g, w, tm=256):
    d = x2.shape[1]
    wd = w.shape[1]
    assert row0 % tm == 0 and n % tm == 0
    return pl.pallas_call(
        _inproj_body,
        grid=(n // tm,),
        in_specs=[pl.BlockSpec((tm, d), lambda i: (i + row0 // tm, 0)),
                  pl.BlockSpec((1, d), lambda i: (0, 0)),
                  pl.BlockSpec((d, wd), lambda i: (0, 0))],
        out_specs=pl.BlockSpec((tm, wd), lambda i: (i, 0)),
        out_shape=jax.ShapeDtypeStruct((n, wd), F32),
        compiler_params=pltpu.CompilerParams(dimension_semantics=("parallel",),
                                             vmem_limit_bytes=VMEM_LIMIT),
        name="inproj",
    )(x2, g, w)


def _level_constants():
    c = CHUNK
    mats = [np.tril(np.ones((c, c), np.float32))]
    level = np.full((c, c), -1, np.int32)
    b, lvl = c // 2, 0
    while b >= 1:
        m = np.zeros((c, c), np.float32)
        for s in range(0, c, 2 * b):
            mid = s + b
            for i in range(mid, s + 2 * b):
                m[i, mid:i + 1] = 1.0
                level[i, s:mid] = lvl
            for j in range(s, mid):
                m[j, j + 1:mid] = 1.0
        mats.append(m)
        b //= 2
        lvl += 1
    level[np.arange(c), np.arange(c)] = lvl
    return np.concatenate(mats, axis=0), level, lvl


_SEG_MATS, _LEVEL_MAP, _N_LEVELS = _level_constants()


def _split3(a):
    hi = a.astype(BF16)
    r = a - hi.astype(F32)
    mid = r.astype(BF16)
    lo = (r - mid.astype(F32)).astype(BF16)
    return jnp.concatenate([hi, mid, lo], axis=1)


def _gla_chunk(q, k, g, v, st, seg, level):
    c = CHUNK
    ex3 = jnp.dot(seg, _split3(g), preferred_element_type=F32)
    ex = ex3[:, 0:LANES] + ex3[:, LANES:2 * LANES] + ex3[:, 2 * LANES:3 * LANES]
    cum = ex[0:c]
    scores = jnp.where(level == _N_LEVELS, _nt(q.astype(BF16), k.astype(BF16)), 0.0)
    for l in range(_N_LEVELS):
        e = jnp.exp(ex[c * (l + 1):c * (l + 2)])
        p = _nt((q * e).astype(BF16), (k * e).astype(BF16))
        scores = jnp.where(level == l, p, scores)
    last = cum[c - 1:c, :]
    qd = (q * jnp.exp(cum)).astype(BF16)
    kd = (k * jnp.exp(last - cum)).astype(BF16)
    vb = v.astype(BF16)
    o = jnp.dot(scores.astype(BF16), vb, preferred_element_type=F32) + _nt(qd, st.astype(BF16))
    st_new = st * jnp.exp(last) + _tn(vb, kd)
    return o, st_new


def _head_out(o, gain, gate):
    on = o * lax.rsqrt(jnp.mean(o * o, axis=-1, keepdims=True) + EPS) * gain
    return on * (gate * jax.nn.sigmoid(gate))


def _mixer_body(p_ref, seg_ref, level_ref, lbl_ref, hgn_ref, wgu_ref, bg_ref, ggn_ref,
                o_ref, st_ref, *, chunks):
    @pl.when(pl.program_id(1) == 0)
    def _():
        st_ref[...] = jnp.zeros_like(st_ref)

    seg = seg_ref[...]
    level = level_ref[...]
    logits = lbl_ref[...]
    ez = jnp.exp(logits - jnp.max(logits, axis=0, keepdims=True))
    lb = ez[0:1, :] / jnp.sum(ez, axis=0, keepdims=True)
    lane = lax.broadcasted_iota(jnp.int32, (CHUNK, LANES), 1)

    def chunk_body(ci, carry):
        rows = pl.ds(pl.multiple_of(ci * CHUNK, CHUNK), CHUNK)

        def col(off):
            return p_ref[rows, pl.ds(off, LANES)]

        for h in range(HG_HEADS):
            hq = col(OFF_HQ + h * LANES)
            q = hq * jax.nn.sigmoid(hq)
            lbh = lb[:, h * LANES:(h + 1) * LANES]
            forget = lbh + (1.0 - lbh) * jax.nn.sigmoid(col(OFF_HF + h * LANES))
            o, st = _gla_chunk(q, 1.0 - forget, jnp.log(forget), col(OFF_HI + h * LANES),
                               st_ref[h], seg, level)
            st_ref[h] = st
            o_ref[rows, pl.ds(h * HEAD_V, HEAD_V)] = _head_out(
                o, hgn_ref[...], col(OFF_HGATE + h * HEAD_V)).astype(o_ref.dtype)

        zg = jnp.dot(col(OFF_GLOW).astype(BF16), wgu_ref[...], preferred_element_type=F32) + bg_ref[...]
        log_g = (jnp.minimum(zg, 0.0) - jnp.log(1.0 + jnp.exp(-jnp.abs(zg)))) / GLA_GATE_NORMALIZER
        for h in range(GLA_HEADS):
            pair, half = h // 2, h % 2
            q = col(OFF_GQ + pair * LANES) * (GLA_HEAD_K ** -0.5)
            in_head = (lane >= half * GLA_HEAD_K) & (lane < (half + 1) * GLA_HEAD_K)
            k = jnp.where(in_head, col(OFF_GK + pair * LANES), 0.0)
            g = log_g[:, pair * LANES:(pair + 1) * LANES]
            o, st = _gla_chunk(q, k, g, col(OFF_GV + h * HEAD_V), st_ref[HG_HEADS + h], seg, level)
            st_ref[HG_HEADS + h] = st
            o_ref[rows, pl.ds((HG_HEADS + h) * HEAD_V, HEAD_V)] = _head_out(
                o, ggn_ref[...], col(OFF_GGATE + h * HEAD_V)).astype(o_ref.dtype)
        return carry

    lax.fori_loop(0, chunks, chunk_body, 0)


def _mixers(proj, batch, seq, lb_logits, hg_norm_g, wgu, bg, gla_norm_g, tt=512):
    n = batch * seq
    steps = seq // tt
    heads = HG_HEADS + GLA_HEADS
    const = lambda shape: pl.BlockSpec(shape, lambda b, t: (0,) * len(shape))
    return pl.pallas_call(
        functools.partial(_mixer_body, chunks=tt // CHUNK),
        grid=(batch, steps),
        in_specs=[pl.BlockSpec((tt, IN_WIDTH_PADDED), lambda b, t: (b * steps + t, 0)),
                  const(_SEG_MATS.shape), const(_LEVEL_MAP.shape),
                  const(lb_logits.shape), const((1, HEAD_V)),
                  const(wgu.shape), const(bg.shape), const((1, HEAD_V))],
        out_specs=pl.BlockSpec((tt, heads * HEAD_V), lambda b, t: (b * steps + t, 0)),
        out_shape=jax.ShapeDtypeStruct((n, heads * HEAD_V), BF16),
        scratch_shapes=[pltpu.VMEM((heads, HEAD_V, LANES), F32)],
        compiler_params=pltpu.CompilerParams(dimension_semantics=("parallel", "arbitrary"),
                                             vmem_limit_bytes=VMEM_LIMIT),
        name="mixers",
    )(proj, jnp.asarray(_SEG_MATS, BF16), jnp.asarray(_LEVEL_MAP), lb_logits, hg_norm_g,
      wgu, bg, gla_norm_g)


def _outproj_body(x_ref, m_ref, wo_ref, g_ref, wq_ref, h_ref, hn_ref, q_ref):
    h = x_ref[...] + jnp.dot(m_ref[...], wo_ref[...], preferred_element_type=F32)
    h_ref[...] = h
    hn = h * lax.rsqrt(jnp.mean(h * h, axis=-1, keepdims=True) + EPS) * g_ref[...]
    hn_ref[...] = hn
    q = jnp.dot(hn.astype(BF16), wq_ref[...], preferred_element_type=F32)
    for j in range(q_ref.shape[0]):
        q_ref[j] = q[:, j * PEER_HALF:(j + 1) * PEER_HALF]


def _outproj(x2, row0, mixed, wo, g2, wq, tm=256):
    n, d = mixed.shape[0], x2.shape[1]
    nq = wq.shape[1] // PEER_HALF
    assert row0 % tm == 0 and n % tm == 0
    return pl.pallas_call(
        _outproj_body,
        grid=(n // tm,),
        in_specs=[pl.BlockSpec((tm, d), lambda i: (i + row0 // tm, 0)),
                  pl.BlockSpec((tm, mixed.shape[1]), lambda i: (i, 0)),
                  pl.BlockSpec(wo.shape, lambda i: (0, 0)),
                  pl.BlockSpec((1, d), lambda i: (0, 0)),
                  pl.BlockSpec(wq.shape, lambda i: (0, 0))],
        out_specs=[pl.BlockSpec((tm, d), lambda i: (i, 0)),
                   pl.BlockSpec((tm, d), lambda i: (i, 0)),
                   pl.BlockSpec((nq, tm, PEER_HALF), lambda i: (0, i, 0))],
        out_shape=[jax.ShapeDtypeStruct((n, d), F32),
                   jax.ShapeDtypeStruct((n, d), F32),
                   jax.ShapeDtypeStruct((nq, n, PEER_HALF), F32)],
        compiler_params=pltpu.CompilerParams(dimension_semantics=("parallel",),
                                             vmem_limit_bytes=VMEM_LIMIT),
        name="outproj",
    )(x2, mixed, wo, g2, wq)


def _candidate_constants():
    k = PEER_TOPK
    pos, valid = [], []
    for j in range(k):
        pos.append(j); valid.append(True)
    for i in range(1, 8):
        for j in range(8):
            pos.append(i * k + j); valid.append((i + 1) * (j + 1) <= k)
    for i in range(8, k):
        pos.append(i * k); valid.append(True)
    return np.asarray(pos, np.float32), np.asarray(valid, bool)


_CAND_POS, _CAND_VALID = _candidate_constants()
_N_CAND = _CAND_POS.shape[0]


def _topk_rows(ss, pos, extras, k):
    t = ss[0].shape[1]
    slot = lax.broadcasted_iota(jnp.int32, (k, t), 0)

    def body(it, carry):
        out = []
        for (s, vals, poss, exts), extra in zip(carry, extras):
            m = jnp.max(s, axis=0, keepdims=True)
            p = jnp.min(jnp.where(s == m, pos, 1e9), axis=0, keepdims=True)
            hit = pos == p
            vals = jnp.where(slot == it, m, vals)
            poss = jnp.where(slot == it, p, poss)
            if extra is not None:
                x = jnp.max(jnp.where(hit, extra, -1.0), axis=0, keepdims=True)
                exts = jnp.where(slot == it, x, exts)
            out.append((jnp.where(hit, -jnp.inf, s), vals, poss, exts))
        return tuple(out)

    z = jnp.zeros((k, t), F32)
    res = lax.fori_loop(0, k, body, tuple((s, z, z, z) for s in ss))
    return [(vals, poss, exts) for _, vals, poss, exts in res]


def _route_body(q_ref, keys_ref, cpos_ref, cvalid_ref, idx_ref, gate_ref):
    k = PEER_TOPK
    n_tiles = q_ref.shape[1] // LANES
    key_pos = lax.broadcasted_iota(jnp.int32, (PEER_N_KEYS, LANES), 0).astype(F32)
    cpos = cpos_ref[...]
    cvalid = cvalid_ref[...] > 0.5
    scores = [_nt(keys_ref[p], q_ref[p, pl.ds(lt * LANES, LANES), :].astype(BF16))
              for lt in range(n_tiles) for p in range(2)]
    tops = _topk_rows(scores, key_pos, [None] * len(scores), k)
    cands, experts = [], []
    for lt in range(n_tiles):
        (v0, i0, _), (v1, i1, _) = tops[2 * lt], tops[2 * lt + 1]
        vals = [v0[0:1] + v1]
        exps = [i0[0:1] * PEER_N_KEYS + i1]
        for i in range(1, 8):
            vals.append(v0[i:i + 1] + v1[0:8])
            exps.append(i0[i:i + 1] * PEER_N_KEYS + i1[0:8])
        vals.append(v0[8:k] + v1[0:1])
        exps.append(i0[8:k] * PEER_N_KEYS + i1[0:1])
        cands.append(jnp.where(cvalid, jnp.concatenate(vals, axis=0), -jnp.inf))
        experts.append(jnp.concatenate(exps, axis=0))
    picks = _topk_rows(cands, cpos, experts, k)
    for lt in range(n_tiles):
        best, _, expert = picks[lt]
        cols = pl.ds(lt * LANES, LANES)
        e = jnp.exp(best - jnp.max(best, axis=0, keepdims=True))
        gate_ref[:, cols] = e / jnp.sum(e, axis=0, keepdims=True)
        idx_ref[:, cols] = expert.astype(jnp.int32)


def _route(q3, keys, tok0, n, tb=512):
    heads = q3.shape[0] // 2
    assert tok0 % tb == 0 and n % tb == 0
    cpos = jnp.broadcast_to(jnp.asarray(_CAND_POS)[:, None], (_N_CAND, LANES))
    cvalid = jnp.broadcast_to(jnp.asarray(_CAND_VALID, F32)[:, None], (_N_CAND, LANES))
    return pl.pallas_call(
        _route_body,
        grid=(n // tb, heads),
        in_specs=[pl.BlockSpec((2, tb, PEER_HALF), lambda i, h: (h, i + tok0 // tb, 0)),
                  pl.BlockSpec((2, PEER_N_KEYS, PEER_HALF), lambda i, h: (h, 0, 0)),
                  pl.BlockSpec((_N_CAND, LANES), lambda i, h: (0, 0)),
                  pl.BlockSpec((_N_CAND, LANES), lambda i, h: (0, 0))],
        out_specs=[pl.BlockSpec((PEER_TOPK, tb), lambda i, h: (h, i)),
                   pl.BlockSpec((PEER_TOPK, tb), lambda i, h: (h, i))],
        out_shape=[jax.ShapeDtypeStruct((heads * PEER_TOPK, n), jnp.int32),
                   jax.ShapeDtypeStruct((heads * PEER_TOPK, n), F32)],
        compiler_params=pltpu.CompilerParams(dimension_semantics=("parallel", "parallel"),
                                             vmem_limit_bytes=VMEM_LIMIT),
        name="route",
    )(q3, keys, cpos, cvalid)


def _sc_pipeline(idx_v, table_hbm, rows, sems, n_units, units_per_token, compute):
    parts = rows[0].shape[0] // SC_LANES
    assert n_units % 2 == 0

    def start(unit, slot):
        tl = unit // units_per_token
        g = unit % units_per_token
        for part in range(parts):
            iv = idx_v[tl, pl.ds((g * parts + part) * SC_LANES, SC_LANES)]
            pltpu.async_copy(table_hbm.at[iv], rows[slot].at[pl.ds(part * SC_LANES, SC_LANES)], sems[slot])

    def wait(slot):
        iv = idx_v[0, pl.ds(0, SC_LANES)]
        for part in range(parts):
            pltpu.make_async_copy(table_hbm.at[iv], rows[slot].at[pl.ds(part * SC_LANES, SC_LANES)],
                                  sems[slot]).wait()

    start(0, 0)

    def pair(i, carry):
        u0 = 2 * i
        start(u0 + 1, 1)
        wait(0)
        compute(u0 // units_per_token, u0 % units_per_token, rows[0])

        @pl.when(u0 + 2 < n_units)
        def _():
            start(u0 + 2, 0)

        wait(1)
        compute((u0 + 1) // units_per_token, (u0 + 1) % units_per_token, rows[1])
        return carry

    lax.fori_loop(0, n_units // 2, pair, 0)


def _sc_batch(tokens_per_worker):
    for tb in (40, 32, 24, 16, 8):
        if tokens_per_worker % tb == 0:
            return tb
    raise ValueError(f"tokens per SparseCore worker ({tokens_per_worker}) must be a multiple of 8")


def _peer_act(hn, idx, u):
    n, d = hn.shape
    kk = idx.shape[1]
    tpw = n // SC_WORKERS
    tb = _sc_batch(tpw)
    upt = kk // SC_LANES
    nchunk = d // SC_LANES
    mesh = plsc.VectorSubcoreMesh(core_axis_name="c", subcore_axis_name="s")

    def body(x_hbm, idx_hbm, u_hbm, out_hbm, x_v, idx_v, act_v, rows0, rows1, tmp_v, sem0, sem1):
        wid = lax.axis_index("s") * 2 + lax.axis_index("c")
        lane_row = lax.iota(jnp.int32, SC_LANES) * SC_LANES

        def compute(tl, g, r_ref):
            def cbody(c, accs):
                xv = x_v[tl, pl.ds(c * SC_LANES, SC_LANES)]
                return tuple(accs[r] + r_ref[r, pl.ds(c * SC_LANES, SC_LANES)] * xv
                             for r in range(SC_LANES))
            accs = lax.fori_loop(0, nchunk, cbody,
                                 tuple(jnp.zeros((SC_LANES,), F32) for _ in range(SC_LANES)))
            for r in range(SC_LANES):
                tmp_v[pl.ds(r * SC_LANES, SC_LANES)] = accs[r]
            tot = plsc.load_gather(tmp_v, [lane_row])
            for j in range(1, SC_LANES):
                tot = tot + plsc.load_gather(tmp_v, [lane_row + j])
            act_v[tl, pl.ds(g * SC_LANES, SC_LANES)] = tot

        def batch(b, carry):
            t0 = wid * tpw + b * tb
            pltpu.sync_copy(x_hbm.at[pl.ds(t0, tb)], x_v)
            pltpu.sync_copy(idx_hbm.at[pl.ds(t0, tb)], idx_v)
            _sc_pipeline(idx_v, u_hbm, (rows0, rows1), (sem0, sem1), tb * upt, upt, compute)
            pltpu.sync_copy(act_v, out_hbm.at[pl.ds(t0, tb)])
            return carry

        lax.fori_loop(0, tpw // tb, batch, 0)

    return pl.kernel(
        body, mesh=mesh,
        compiler_params=pltpu.CompilerParams(needs_layout_passes=False),
        out_type=jax.ShapeDtypeStruct((n, kk), F32),
        scratch_types=[pltpu.VMEM((tb, d), F32), pltpu.VMEM((tb, kk), jnp.int32),
                       pltpu.VMEM((tb, kk), F32),
                       pltpu.VMEM((SC_LANES, d), F32), pltpu.VMEM((SC_LANES, d), F32),
                       pltpu.VMEM((SC_LANES * SC_LANES,), F32),
                       pltpu.SemaphoreType.DMA, pltpu.SemaphoreType.DMA],
        name="peer_act",
    )(hn, idx, u)


def _peer_out(w, idx, v, cb=16, unit_rows=2 * SC_LANES):
    n, kk = w.shape
    d = v.shape[1]
    tpw = n // SC_WORKERS
    tb = _sc_batch(tpw)
    upt = kk // unit_rows
    nchunk = d // SC_LANES
    mesh = plsc.VectorSubcoreMesh(core_axis_name="c", subcore_axis_name="s")

    def body(w_hbm, idx_hbm, v_hbm, out_hbm, w_v, idx_v, out_v, rows0, rows1, sem0, sem1):
        wid = lax.axis_index("s") * 2 + lax.axis_index("c")
        zero = jnp.zeros((SC_LANES,), F32)

        def compute(tl, g, r_ref):
            tls = jnp.full((SC_LANES,), tl, jnp.int32)

            def cb_body(ci, carry):
                c0 = ci * (cb * SC_LANES)
                accs = tuple(out_v[tl, pl.ds(c0 + j * SC_LANES, SC_LANES)] for j in range(cb))

                def rbody(r, accs):
                    wr = plsc.load_gather(w_v, [tls, jnp.full((SC_LANES,), g * unit_rows + r, jnp.int32)])
                    return tuple(accs[j] + wr * r_ref[r, pl.ds(c0 + j * SC_LANES, SC_LANES)]
                                 for j in range(cb))

                accs = lax.fori_loop(0, unit_rows, rbody, accs)
                for j in range(cb):
                    out_v[tl, pl.ds(c0 + j * SC_LANES, SC_LANES)] = accs[j]
                return carry

            lax.fori_loop(0, nchunk // cb, cb_body, 0)

        def batch(b, carry):
            t0 = wid * tpw + b * tb
            pltpu.sync_copy(w_hbm.at[pl.ds(t0, tb)], w_v)
            pltpu.sync_copy(idx_hbm.at[pl.ds(t0, tb)], idx_v)

            def zbody(i, carry):
                out_v[i // nchunk, pl.ds((i % nchunk) * SC_LANES, SC_LANES)] = zero
                return carry

            lax.fori_loop(0, tb * nchunk, zbody, 0)
            _sc_pipeline(idx_v, v_hbm, (rows0, rows1), (sem0, sem1), tb * upt, upt, compute)
            pltpu.sync_copy(out_v, out_hbm.at[pl.ds(t0, tb)])
            return carry

        lax.fori_loop(0, tpw // tb, batch, 0)

    return pl.kernel(
        body, mesh=mesh,
        compiler_params=pltpu.CompilerParams(needs_layout_passes=False),
        out_type=jax.ShapeDtypeStruct((n, d), F32),
        scratch_types=[pltpu.VMEM((tb, kk), F32), pltpu.VMEM((tb, kk), jnp.int32),
                       pltpu.VMEM((tb, d), F32),
                       pltpu.VMEM((unit_rows, d), F32), pltpu.VMEM((unit_rows, d), F32),
                       pltpu.SemaphoreType.DMA, pltpu.SemaphoreType.DMA],
        name="peer_out",
    )(w, idx, v)


SUBLANES = 8
TC_PEER_TOKENS = 8
ISSUE_UNROLL = 8


def _gelu(a):
    return a * (lax.erf(a / np.sqrt(2.0).astype(np.float32)) + 1.0) / 2.0


def _peer_tc_body(idx_ref, idx_next_ref, x_ref, g_ref, u_hbm, v_hbm, o_ref, ubuf, vbuf, sem_u, sem_v):
    tokens = x_ref.shape[0]
    kk = g_ref.shape[1]
    n_rows = tokens * kk
    step = pl.program_id(0)
    slot = step % 2

    def row_copy(table, buf, sem, expert, j):
        src = table.at[pl.ds(pl.multiple_of(expert * SUBLANES, SUBLANES), SUBLANES)]
        return pltpu.make_async_copy(src, buf.at[pl.ds(j * SUBLANES, SUBLANES)], sem)

    def issue_block(ids_ref, into):
        def issue(i, carry):
            for r in range(ISSUE_UNROLL):
                j = i * ISSUE_UNROLL + r
                expert = ids_ref[j]
                row_copy(u_hbm, ubuf.at[into], sem_u.at[into], expert, j).start()
                row_copy(v_hbm, vbuf.at[into], sem_v.at[into], expert, j).start()
            return carry

        lax.fori_loop(0, n_rows // ISSUE_UNROLL, issue, 0)

    @pl.when(step == 0)
    def _():
        issue_block(idx_ref, 0)

    @pl.when(step + 1 < pl.num_programs(0))
    def _():
        issue_block(idx_next_ref, 1 - slot)

    def wait_all(table, buf, sem):
        pltpu.make_async_copy(table.at[pl.ds(0, n_rows * SUBLANES)], buf.at[slot], sem.at[slot]).wait()

    def rows_of(buf, t, s):
        return buf[slot, pl.ds(t * kk * SUBLANES + s, kk, stride=SUBLANES), :]

    wait_all(u_hbm, ubuf, sem_u)
    acts = []
    for t in range(tokens):
        acc = rows_of(ubuf, t, 0) * x_ref[t:t + 1, 0:LANES]
        for s in range(1, SUBLANES):
            acc = acc + rows_of(ubuf, t, s) * x_ref[t:t + 1, s * LANES:(s + 1) * LANES]
        acts.append(jnp.sum(acc, axis=1, keepdims=True))
    w = g_ref[0] * _gelu(jnp.concatenate(acts, axis=1))

    wait_all(v_hbm, vbuf, sem_v)
    for t in range(tokens):
        wt = w[:, t:t + 1]
        for s in range(SUBLANES):
            o_ref[t:t + 1, s * LANES:(s + 1) * LANES] = jnp.sum(rows_of(vbuf, t, s) * wt, axis=0, keepdims=True)


def _peer_tc(hn, idx, gate_t, u_tiles, v_tiles):
    m, d = hn.shape
    kk = idx.shape[1]
    tb = TC_PEER_TOKENS
    assert d == SUBLANES * LANES and m % tb == 0
    steps = m // tb
    g3 = gate_t.reshape(kk, steps, tb).transpose(1, 0, 2)
    rows = tb * kk * SUBLANES
    ids = idx.reshape(m * kk)
    return pl.pallas_call(
        _peer_tc_body,
        grid=(steps,),
        in_specs=[pl.BlockSpec((tb * kk,), lambda i: (i,), memory_space=pltpu.SMEM),
                  pl.BlockSpec((tb * kk,), lambda i: (jnp.minimum(i + 1, steps - 1),), memory_space=pltpu.SMEM),
                  pl.BlockSpec((tb, d), lambda i: (i, 0)),
                  pl.BlockSpec((1, kk, tb), lambda i: (i, 0, 0)),
                  pl.BlockSpec(memory_space=pl.ANY),
                  pl.BlockSpec(memory_space=pl.ANY)],
        out_specs=pl.BlockSpec((tb, d), lambda i: (i, 0)),
        out_shape=jax.ShapeDtypeStruct((m, d), F32),
        scratch_shapes=[pltpu.VMEM((2, rows, LANES), F32), pltpu.VMEM((2, rows, LANES), F32),
                        pltpu.SemaphoreType.DMA((2,)), pltpu.SemaphoreType.DMA((2,))],
        compiler_params=pltpu.CompilerParams(dimension_semantics=("arbitrary",),
                                             vmem_limit_bytes=VMEM_LIMIT),
        name="peer_tc",
    )(ids, ids, hn, g3, u_tiles, v_tiles)


def _gelu_gate_body(a_ref, g_ref, o_ref):
    o_ref[...] = g_ref[...] * _gelu(a_ref[...])


def _gelu_gate(act, gate, tm=SHARE_GRANULE):
    n, kk = act.shape
    tm = min(tm, n)
    assert n % tm == 0
    spec = pl.BlockSpec((tm, kk), lambda i: (i, 0))
    return pl.pallas_call(
        _gelu_gate_body, grid=(n // tm,), in_specs=[spec, spec], out_specs=spec,
        out_shape=jax.ShapeDtypeStruct((n, kk), F32),
        compiler_params=pltpu.CompilerParams(dimension_semantics=("parallel",)),
        name="gelu_gate",
    )(act, gate)


def _final_body(h_ref, g_ref, *rest, starts, aliased):
    o_ref = rest[-1]
    parts = rest[:len(starts)]
    assert len(rest) == len(starts) + 1 + int(aliased)
    peer = parts[0][...]
    for start, p_ref in zip(starts[1:], parts[1:]):
        peer = jnp.where(pl.program_id(0) >= start, p_ref[...], peer)
    h = h_ref[...] + peer
    o_ref[...] = h * lax.rsqrt(jnp.mean(h * h, axis=-1, keepdims=True) + EPS) * g_ref[...]


def _final(h1, peer_parts, g, out, row0, n_total, tm=256):
    n, d = h1.shape
    sizes = [p.shape[0] for p in peer_parts]
    assert sum(sizes) == n and all(m > 0 and m % tm == 0 for m in sizes) and row0 % tm == 0
    starts = tuple(int(v) // tm for v in np.cumsum([0] + sizes[:-1]))

    def part_spec(start, size):
        return pl.BlockSpec((tm, d), lambda i: (jnp.clip(i - start, 0, size // tm - 1), 0))

    in_specs = [pl.BlockSpec((tm, d), lambda i: (i, 0)), pl.BlockSpec((1, d), lambda i: (0, 0))]
    in_specs += [part_spec(s, m) for s, m in zip(starts, sizes)]
    args = [h1, g, *peer_parts]
    aliases = {}
    if out is not None:
        in_specs.append(pl.BlockSpec(memory_space=pl.ANY))
        aliases = {len(args): 0}
        args.append(out)
    return pl.pallas_call(
        functools.partial(_final_body, starts=starts, aliased=out is not None), grid=(n // tm,),
        in_specs=in_specs,
        out_specs=pl.BlockSpec((tm, d), lambda i: (i + row0 // tm, 0)),
        out_shape=jax.ShapeDtypeStruct((n_total, d), F32),
        input_output_aliases=aliases,
        compiler_params=pltpu.CompilerParams(dimension_semantics=("parallel",)),
        name="final_norm",
    )(*args)


def _tc_share(n_tokens):
    m = int(n_tokens * TC_SHARE) // SHARE_GRANULE * SHARE_GRANULE
    return min(max(m, SHARE_GRANULE), n_tokens - SHARE_GRANULE)


def kernel(x, norm1_g, w_in, hg_lower_logits, hg_norm_g, gla_w_gate_up, gla_b_gate, gla_norm_g,
           w_out, norm2_g, peer_w_q, peer_sub_keys, peer_u, peer_v, norm_f_g):
    batch, seq, d = x.shape
    depth = w_in.shape[0]
    assert depth == 1, "single-layer block"
    layer = 0

    w = w_in[layer]
    pad = LANES - GLA_GATE_RANK
    w_r = jnp.concatenate(
        [w[:, :3072], w[:, 3072:3072 + GLA_GATE_RANK], jnp.zeros((d, pad), w.dtype),
         w[:, 3072 + GLA_GATE_RANK:]], axis=1).astype(BF16)
    wgu = jnp.concatenate([gla_w_gate_up[layer], jnp.zeros((pad, GLA_KEY), F32)], axis=0).astype(BF16)

    wo = w_out[layer].astype(BF16)
    wq = peer_w_q[layer].astype(BF16)
    keys = peer_sub_keys[layer].reshape(PEER_HEADS * 2, PEER_N_KEYS, PEER_HALF).astype(BF16)
    n_experts = peer_u.shape[1]
    u_tiles = peer_u[layer].reshape(n_experts * SUBLANES, LANES)
    v_tiles = peer_v[layer].reshape(n_experts * SUBLANES, LANES)

    groups = PIPELINE_GROUPS if batch % PIPELINE_GROUPS == 0 else 1
    bpg = batch // groups
    ng = bpg * seq
    m_tc = _tc_share(ng)
    gf = norm_f_g.reshape(1, d)
    x2 = x.reshape(batch * seq, d)

    def after(value, anchor):
        value, _ = lax.optimization_barrier((value, anchor))
        return value

    half = ng // 2
    assert m_tc <= half
    h1s, tc_ins, sc_units = [], [], []
    for gi in range(groups):
        proj = _inproj(x2, gi * ng, ng, norm1_g[layer].reshape(1, d), w_r)
        mixed = _mixers(proj, bpg, seq, hg_lower_logits, hg_norm_g[layer].reshape(1, HEAD_V), wgu,
                        gla_b_gate[layer].reshape(1, GLA_KEY), gla_norm_g[layer].reshape(1, HEAD_V))
        h1, hn, q3 = _outproj(x2, gi * ng, mixed, wo, norm2_g[layer].reshape(1, d), wq)
        h1s.append(h1)
        bounds = (0, half, ng) if gi == 0 and m_tc < half else (0, ng)
        units = []
        for lo, hi in zip(bounds[:-1], bounds[1:]):
            idx_t, gate_t = _route(q3, keys, lo, hi - lo)
            idx = idx_t.T
            if lo == 0:
                tc_ins.append((hn[:m_tc], idx[:m_tc], gate_t[:, :m_tc]))
            s0 = max(lo, m_tc) - lo
            units.append(dict(idx=idx[s0:], gate=gate_t[:, s0:].T,
                              act=_peer_act(hn[lo + s0:hi], idx[s0:], peer_u[layer])))
        sc_units.append(units)

    def sc_out(gi, anchor):
        outs = []
        for unit in sc_units[gi]:
            wts = _gelu_gate(after(unit["act"], anchor), unit["gate"])
            outs.append(_peer_out(wts, unit["idx"], peer_v[layer]))
        return wts, outs

    rows, peer_sc = [None] * groups, [None] * groups
    rows[0] = _peer_tc(*after(tc_ins[0], sc_units[-1][-1]["idx"]), u_tiles, v_tiles)
    anchor = rows[0]
    for gi in range(groups - 1):
        anchor, peer_sc[gi] = sc_out(gi, rows[0])
    if groups > 1:
        rows[1] = _peer_tc(*after(tc_ins[1], anchor), u_tiles, v_tiles)
        anchor = rows[1]
    anchor, peer_sc[groups - 1] = sc_out(groups - 1, anchor)
    for gi in range(2, groups):
        rows[gi] = _peer_tc(*after(tc_ins[gi], anchor), u_tiles, v_tiles)
        anchor = rows[gi]
    out = None
    for gi in range(groups):
        out = _final(h1s[gi], [rows[gi], *peer_sc[gi]], gf, out, gi * ng, batch * seq)
    return out.reshape(batch, seq, d)
```

```python
import functools

import jax
import jax.numpy as jnp
import numpy as np
from jax import lax
from jax.experimental import pallas as pl
from jax.experimental.pallas import tpu as pltpu
from jax.experimental.pallas import tpu_sc as plsc

F32 = jnp.float32
BF16 = jnp.bfloat16
EPS = 1e-6

D_MODEL = 1024
HG_HEADS = 4
GLA_HEADS = 4
HEAD_V = 128
HG_KEY = 512
GLA_KEY = 256
GLA_HEAD_K = 64
GLA_GATE_RANK = 16
GLA_GATE_NORMALIZER = 16.0
CHUNK = 64
LANES = 128
PEER_HEADS = 8
PEER_N_KEYS = 128
PEER_HALF = 128
PEER_TOPK = 16

OFF_HQ, OFF_HF, OFF_HI, OFF_HGATE = 0, 512, 1024, 1536
OFF_GQ, OFF_GK, OFF_GV, OFF_GLOW, OFF_GGATE = 2048, 2304, 2560, 3072, 3200
IN_WIDTH_PADDED = 3712

VMEM_LIMIT = 48 * 1024 * 1024

SC_LANES = 16
SC_WORKERS = 32
PIPELINE_GROUPS = 4
TC_SHARE = 0.25
SHARE_GRANULE = 256


def _nt(a, b):
    return lax.dot_general(a, b, (((1,), (1,)), ((), ())), preferred_element_type=F32)


def _tn(a, b):
    return lax.dot_general(a, b, (((0,), (0,)), ((), ())), preferred_element_type=F32)


def _inproj_body(x_ref, g_ref, w_ref, o_ref):
    x = x_ref[...]
    xn = x * lax.rsqrt(jnp.mean(x * x, axis=-1, keepdims=True) + EPS) * g_ref[...]
    o_ref[...] = jnp.dot(xn.astype(BF16), w_ref[...], preferred_element_type=F32)


def _inproj(x2, row0, n, g, w, tm=256):
    d = x2.shape[1]
    wd = w.shape[1]
    assert row0 % tm == 0 and n % tm == 0
    return pl.pallas_call(
        _inproj_body,
        grid=(n // tm,),
        in_specs=[pl.BlockSpec((tm, d), lambda i: (i + row0 // tm, 0)),
                  pl.BlockSpec((1, d), lambda i: (0, 0)),
                  pl.BlockSpec((d, wd), lambda i: (0, 0))],
        out_specs=pl.BlockSpec((tm, wd), lambda i: (i, 0)),
        out_shape=jax.ShapeDtypeStruct((n, wd), F32),
        compiler_params=pltpu.CompilerParams(dimension_semantics=("parallel",),
                                             vmem_limit_bytes=VMEM_LIMIT),
        name="inproj",
    )(x2, g, w)


def _level_constants():
    c = CHUNK
    mats = [np.tril(np.ones((c, c), np.float32))]
    level = np.full((c, c), -1, np.int32)
    b, lvl = c // 2, 0
    while b >= 1:
        m = np.zeros((c, c), np.float32)
        for s in range(0, c, 2 * b):
            mid = s + b
            for i in range(mid, s + 2 * b):
                m[i, mid:i + 1] = 1.0
                level[i, s:mid] = lvl
            for j in range(s, mid):
                m[j, j + 1:mid] = 1.0
        mats.append(m)
        b //= 2
        lvl += 1
    level[np.arange(c), np.arange(c)] = lvl
    return np.concatenate(mats, axis=0), level, lvl


_SEG_MATS, _LEVEL_MAP, _N_LEVELS = _level_constants()


def _split3(a):
    hi = a.astype(BF16)
    r = a - hi.astype(F32)
    mid = r.astype(BF16)
    lo = (r - mid.astype(F32)).astype(BF16)
    return jnp.concatenate([hi, mid, lo], axis=1)


def _gla_chunk(q, k, g, v, st, seg, level):
    c = CHUNK
    ex3 = jnp.dot(seg, _split3(g), preferred_element_type=F32)
    ex = ex3[:, 0:LANES] + ex3[:, LANES:2 * LANES] + ex3[:, 2 * LANES:3 * LANES]
    cum = ex[0:c]
    scores = jnp.where(level == _N_LEVELS, _nt(q.astype(BF16), k.astype(BF16)), 0.0)
    for l in range(_N_LEVELS):
        e = jnp.exp(ex[c * (l + 1):c * (l + 2)])
        p = _nt((q * e).astype(BF16), (k * e).astype(BF16))
        scores = jnp.where(level == l, p, scores)
    last = cum[c - 1:c, :]
    qd = (q * jnp.exp(cum)).astype(BF16)
    kd = (k * jnp.exp(last - cum)).astype(BF16)
    vb = v.astype(BF16)
    o = jnp.dot(scores.astype(BF16), vb, preferred_element_type=F32) + _nt(qd, st.astype(BF16))
    st_new = st * jnp.exp(last) + _tn(vb, kd)
    return o, st_new


def _head_out(o, gain, gate):
    on = o * lax.rsqrt(jnp.mean(o * o, axis=-1, keepdims=True) + EPS) * gain
    return on * (gate * jax.nn.sigmoid(gate))


def _mixer_body(p_ref, seg_ref, level_ref, lbl_ref, hgn_ref, wgu_ref, bg_ref, ggn_ref,
                o_ref, st_ref, *, chunks):
    @pl.when(pl.program_id(1) == 0)
    def _():
        st_ref[...] = jnp.zeros_like(st_ref)

    seg = seg_ref[...]
    level = level_ref[...]
    logits = lbl_ref[...]
    ez = jnp.exp(logits - jnp.max(logits, axis=0, keepdims=True))
    lb = ez[0:1, :] / jnp.sum(ez, axis=0, keepdims=True)
    lane = lax.broadcasted_iota(jnp.int32, (CHUNK, LANES), 1)

    def chunk_body(ci, carry):
        rows = pl.ds(pl.multiple_of(ci * CHUNK, CHUNK), CHUNK)

        def col(off):
            return p_ref[rows, pl.ds(off, LANES)]

        for h in range(HG_HEADS):
            hq = col(OFF_HQ + h * LANES)
            q = hq * jax.nn.sigmoid(hq)
            lbh = lb[:, h * LANES:(h + 1) * LANES]
            forget = lbh + (1.0 - lbh) * jax.nn.sigmoid(col(OFF_HF + h * LANES))
            o, st = _gla_chunk(q, 1.0 - forget, jnp.log(forget), col(OFF_HI + h * LANES),
                               st_ref[h], seg, level)
            st_ref[h] = st
            o_ref[rows, pl.ds(h * HEAD_V, HEAD_V)] = _head_out(
                o, hgn_ref[...], col(OFF_HGATE + h * HEAD_V)).astype(o_ref.dtype)

        zg = jnp.dot(col(OFF_GLOW).astype(BF16), wgu_ref[...], preferred_element_type=F32) + bg_ref[...]
        log_g = (jnp.minimum(zg, 0.0) - jnp.log(1.0 + jnp.exp(-jnp.abs(zg)))) / GLA_GATE_NORMALIZER
        for h in range(GLA_HEADS):
            pair, half = h // 2, h % 2
            q = col(OFF_GQ + pair * LANES) * (GLA_HEAD_K ** -0.5)
            in_head = (lane >= half * GLA_HEAD_K) & (lane < (half + 1) * GLA_HEAD_K)
            k = jnp.where(in_head, col(OFF_GK + pair * LANES), 0.0)
            g = log_g[:, pair * LANES:(pair + 1) * LANES]
            o, st = _gla_chunk(q, k, g, col(OFF_GV + h * HEAD_V), st_ref[HG_HEADS + h], seg, level)
            st_ref[HG_HEADS + h] = st
            o_ref[rows, pl.ds((HG_HEADS + h) * HEAD_V, HEAD_V)] = _head_out(
                o, ggn_ref[...], col(OFF_GGATE + h * HEAD_V)).astype(o_ref.dtype)
        return carry

    lax.fori_loop(0, chunks, chunk_body, 0)


def _mixers(proj, batch, seq, lb_logits, hg_norm_g, wgu, bg, gla_norm_g, tt=512):
    n = batch * seq
    steps = seq // tt
    heads = HG_HEADS + GLA_HEADS
    const = lambda shape: pl.BlockSpec(shape, lambda b, t: (0,) * len(shape))
    return pl.pallas_call(
        functools.partial(_mixer_body, chunks=tt // CHUNK),
        grid=(batch, steps),
        in_specs=[pl.BlockSpec((tt, IN_WIDTH_PADDED), lambda b, t: (b * steps + t, 0)),
                  const(_SEG_MATS.shape), const(_LEVEL_MAP.shape),
                  const(lb_logits.shape), const((1, HEAD_V)),
                  const(wgu.shape), const(bg.shape), const((1, HEAD_V))],
        out_specs=pl.BlockSpec((tt, heads * HEAD_V), lambda b, t: (b * steps + t, 0)),
        out_shape=jax.ShapeDtypeStruct((n, heads * HEAD_V), BF16),
        scratch_shapes=[pltpu.VMEM((heads, HEAD_V, LANES), F32)],
        compiler_params=pltpu.CompilerParams(dimension_semantics=("parallel", "arbitrary"),
                                             vmem_limit_bytes=VMEM_LIMIT),
        name="mixers",
    )(proj, jnp.asarray(_SEG_MATS, BF16), jnp.asarray(_LEVEL_MAP), lb_logits, hg_norm_g,
      wgu, bg, gla_norm_g)


def _outproj_body(x_ref, m_ref, wo_ref, g_ref, wq_ref, h_ref, hn_ref, q_ref):
    h = x_ref[...] + jnp.dot(m_ref[...], wo_ref[...], preferred_element_type=F32)
    h_ref[...] = h
    hn = h * lax.rsqrt(jnp.mean(h * h, axis=-1, keepdims=True) + EPS) * g_ref[...]
    hn_ref[...] = hn
    q = jnp.dot(hn.astype(BF16), wq_ref[...], preferred_element_type=F32)
    for j in range(q_ref.shape[0]):
        q_ref[j] = q[:, j * PEER_HALF:(j + 1) * PEER_HALF]


def _outproj(x2, row0, mixed, wo, g2, wq, tm=256):
    n, d = mixed.shape[0], x2.shape[1]
    nq = wq.shape[1] // PEER_HALF
    assert row0 % tm == 0 and n % tm == 0
    return pl.pallas_call(
        _outproj_body,
        grid=(n // tm,),
        in_specs=[pl.BlockSpec((tm, d), lambda i: (i + row0 // tm, 0)),
                  pl.BlockSpec((tm, mixed.shape[1]), lambda i: (i, 0)),
                  pl.BlockSpec(wo.shape, lambda i: (0, 0)),
                  pl.BlockSpec((1, d), lambda i: (0, 0)),
                  pl.BlockSpec(wq.shape, lambda i: (0, 0))],
        out_specs=[pl.BlockSpec((tm, d), lambda i: (i, 0)),
                   pl.BlockSpec((tm, d), lambda i: (i, 0)),
                   pl.BlockSpec((nq, tm, PEER_HALF), lambda i: (0, i, 0))],
        out_shape=[jax.ShapeDtypeStruct((n, d), F32),
                   jax.ShapeDtypeStruct((n, d), F32),
                   jax.ShapeDtypeStruct((nq, n, PEER_HALF), F32)],
        compiler_params=pltpu.CompilerParams(dimension_semantics=("parallel",),
                                             vmem_limit_bytes=VMEM_LIMIT),
        name="outproj",
    )(x2, mixed, wo, g2, wq)


def _candidate_constants():
    k = PEER_TOPK
    pos, valid = [], []
    for j in range(k):
        pos.append(j); valid.append(True)
    for i in range(1, 8):
        for j in range(8):
            pos.append(i * k + j); valid.append((i + 1) * (j + 1) <= k)
    for i in range(8, k):
        pos.append(i * k); valid.append(True)
    return np.asarray(pos, np.float32), np.asarray(valid, bool)


_CAND_POS, _CAND_VALID = _candidate_constants()
_N_CAND = _CAND_POS.shape[0]


def _topk_rows(ss, pos, extras, k):
    t = ss[0].shape[1]
    slot = lax.broadcasted_iota(jnp.int32, (k, t), 0)

    def body(it, carry):
        out = []
        for (s, vals, poss, exts), extra in zip(carry, extras):
            m = jnp.max(s, axis=0, keepdims=True)
            p = jnp.min(jnp.where(s == m, pos, 1e9), axis=0, keepdims=True)
            hit = pos == p
            vals = jnp.where(slot == it, m, vals)
            poss = jnp.where(slot == it, p, poss)
            if extra is not None:
                x = jnp.max(jnp.where(hit, extra, -1.0), axis=0, keepdims=True)
                exts = jnp.where(slot == it, x, exts)
            out.append((jnp.where(hit, -jnp.inf, s), vals, poss, exts))
        return tuple(out)

    z = jnp.zeros((k, t), F32)
    res = lax.fori_loop(0, k, body, tuple((s, z, z, z) for s in ss))
    return [(vals, poss, exts) for _, vals, poss, exts in res]


def _route_body(q_ref, keys_ref, cpos_ref, cvalid_ref, idx_ref, gate_ref):
    k = PEER_TOPK
    n_tiles = q_ref.shape[1] // LANES
    key_pos = lax.broadcasted_iota(jnp.int32, (PEER_N_KEYS, LANES), 0).astype(F32)
    cpos = cpos_ref[...]
    cvalid = cvalid_ref[...] > 0.5
    scores = [_nt(keys_ref[p], q_ref[p, pl.ds(lt * LANES, LANES), :].astype(BF16))
              for lt in range(n_tiles) for p in range(2)]
    tops = _topk_rows(scores, key_pos, [None] * len(scores), k)
    cands, experts = [], []
    for lt in range(n_tiles):
        (v0, i0, _), (v1, i1, _) = tops[2 * lt], tops[2 * lt + 1]
        vals = [v0[0:1] + v1]
        exps = [i0[0:1] * PEER_N_KEYS + i1]
        for i in range(1, 8):
            vals.append(v0[i:i + 1] + v1[0:8])
            exps.append(i0[i:i + 1] * PEER_N_KEYS + i1[0:8])
        vals.append(v0[8:k] + v1[0:1])
        exps.append(i0[8:k] * PEER_N_KEYS + i1[0:1])
        cands.append(jnp.where(cvalid, jnp.concatenate(vals, axis=0), -jnp.inf))
        experts.append(jnp.concatenate(exps, axis=0))
    picks = _topk_rows(cands, cpos, experts, k)
    for lt in range(n_tiles):
        best, _, expert = picks[lt]
        cols = pl.ds(lt * LANES, LANES)
        e = jnp.exp(best - jnp.max(best, axis=0, keepdims=True))
        gate_ref[:, cols] = e / jnp.sum(e, axis=0, keepdims=True)
        idx_ref[:, cols] = expert.astype(jnp.int32)


def _route(q3, keys, tok0, n, tb=512):
    heads = q3.shape[0] // 2
    assert tok0 % tb == 0 and n % tb == 0
    cpos = jnp.broadcast_to(jnp.asarray(_CAND_POS)[:, None], (_N_CAND, LANES))
    cvalid = jnp.broadcast_to(jnp.asarray(_CAND_VALID, F32)[:, None], (_N_CAND, LANES))
    return pl.pallas_call(
        _route_body,
        grid=(n // tb, heads),
        in_specs=[pl.BlockSpec((2, tb, PEER_HALF), lambda i, h: (h, i + tok0 // tb, 0)),
                  pl.BlockSpec((2, PEER_N_KEYS, PEER_HALF), lambda i, h: (h, 0, 0)),
                  pl.BlockSpec((_N_CAND, LANES), lambda i, h: (0, 0)),
                  pl.BlockSpec((_N_CAND, LANES), lambda i, h: (0, 0))],
        out_specs=[pl.BlockSpec((PEER_TOPK, tb), lambda i, h: (h, i)),
                   pl.BlockSpec((PEER_TOPK, tb), lambda i, h: (h, i))],
        out_shape=[jax.ShapeDtypeStruct((heads * PEER_TOPK, n), jnp.int32),
                   jax.ShapeDtypeStruct((heads * PEER_TOPK, n), F32)],
        compiler_params=pltpu.CompilerParams(dimension_semantics=("parallel", "parallel"),
                                             vmem_limit_bytes=VMEM_LIMIT),
        name="route",
    )(q3, keys, cpos, cvalid)


def _sc_pipeline(idx_v, table_hbm, rows, sems, n_units, units_per_token, compute):
    parts = rows[0].shape[0] // SC_LANES
    assert n_units % 2 == 0

    def start(unit, slot):
        tl = unit // units_per_token
        g = unit % units_per_token
        for part in range(parts):
            iv = idx_v[tl, pl.ds((g * parts + part) * SC_LANES, SC_LANES)]
            pltpu.async_copy(table_hbm.at[iv], rows[slot].at[pl.ds(part * SC_LANES, SC_LANES)], sems[slot])

    def wait(slot):
        iv = idx_v[0, pl.ds(0, SC_LANES)]
        for part in range(parts):
            pltpu.make_async_copy(table_hbm.at[iv], rows[slot].at[pl.ds(part * SC_LANES, SC_LANES)],
                                  sems[slot]).wait()

    start(0, 0)

    def pair(i, carry):
        u0 = 2 * i
        start(u0 + 1, 1)
        wait(0)
        compute(u0 // units_per_token, u0 % units_per_token, rows[0])

        @pl.when(u0 + 2 < n_units)
        def _():
            start(u0 + 2, 0)

        wait(1)
        compute((u0 + 1) // units_per_token, (u0 + 1) % units_per_token, rows[1])
        return carry

    lax.fori_loop(0, n_units // 2, pair, 0)


def _sc_batch(tokens_per_worker):
    for tb in (40, 32, 24, 16, 8):
        if tokens_per_worker % tb == 0:
            return tb
    raise ValueError(f"tokens per SparseCore worker ({tokens_per_worker}) must be a multiple of 8")


def _peer_act(hn, idx, u, unit_rows=2 * SC_LANES):
    n, d = hn.shape
    kk = idx.shape[1]
    tpw = n // SC_WORKERS
    tb = _sc_batch(tpw)
    upt = kk // unit_rows
    parts = unit_rows // SC_LANES
    nchunk = d // SC_LANES
    mesh = plsc.VectorSubcoreMesh(core_axis_name="c", subcore_axis_name="s")

    def body(x_hbm, idx_hbm, u_hbm, out_hbm, x_v, idx_v, act_v, rows0, rows1, tmp_v, sem0, sem1):
        wid = lax.axis_index("s") * 2 + lax.axis_index("c")
        lane_row = lax.iota(jnp.int32, SC_LANES) * SC_LANES

        def compute(tl, g, r_ref):
            for part in range(parts):
                row0 = part * SC_LANES

                def cbody(c, accs, row0=row0):
                    xv = x_v[tl, pl.ds(c * SC_LANES, SC_LANES)]
                    return tuple(accs[r] + r_ref[row0 + r, pl.ds(c * SC_LANES, SC_LANES)] * xv
                                 for r in range(SC_LANES))
                accs = lax.fori_loop(0, nchunk, cbody,
                                     tuple(jnp.zeros((SC_LANES,), F32) for _ in range(SC_LANES)))
                for r in range(SC_LANES):
                    tmp_v[pl.ds(r * SC_LANES, SC_LANES)] = accs[r]
                tot = plsc.load_gather(tmp_v, [lane_row])
                for j in range(1, SC_LANES):
                    tot = tot + plsc.load_gather(tmp_v, [lane_row + j])
                act_v[tl, pl.ds((g * parts + part) * SC_LANES, SC_LANES)] = tot

        def batch(b, carry):
            t0 = wid * tpw + b * tb
            pltpu.sync_copy(x_hbm.at[pl.ds(t0, tb)], x_v)
            pltpu.sync_copy(idx_hbm.at[pl.ds(t0, tb)], idx_v)
            _sc_pipeline(idx_v, u_hbm, (rows0, rows1), (sem0, sem1), tb * upt, upt, compute)
            pltpu.sync_copy(act_v, out_hbm.at[pl.ds(t0, tb)])
            return carry

        lax.fori_loop(0, tpw // tb, batch, 0)

    return pl.kernel(
        body, mesh=mesh,
        compiler_params=pltpu.CompilerParams(needs_layout_passes=False),
        out_type=jax.ShapeDtypeStruct((n, kk), F32),
        scratch_types=[pltpu.VMEM((tb, d), F32), pltpu.VMEM((tb, kk), jnp.int32),
                       pltpu.VMEM((tb, kk), F32),
                       pltpu.VMEM((unit_rows, d), F32), pltpu.VMEM((unit_rows, d), F32),
                       pltpu.VMEM((SC_LANES * SC_LANES,), F32),
                       pltpu.SemaphoreType.DMA, pltpu.SemaphoreType.DMA],
        name="peer_act",
    )(hn, idx, u)


def _peer_out(w, idx, v, cb=16, unit_rows=2 * SC_LANES):
    n, kk = w.shape
    d = v.shape[1]
    tpw = n // SC_WORKERS
    tb = _sc_batch(tpw)
    upt = kk // unit_rows
    nchunk = d // SC_LANES
    mesh = plsc.VectorSubcoreMesh(core_axis_name="c", subcore_axis_name="s")

    def body(w_hbm, idx_hbm, v_hbm, out_hbm, w_v, idx_v, out_v, rows0, rows1, sem0, sem1):
        wid = lax.axis_index("s") * 2 + lax.axis_index("c")
        zero = jnp.zeros((SC_LANES,), F32)

        def compute(tl, g, r_ref):
            tls = jnp.full((SC_LANES,), tl, jnp.int32)

            def cb_body(ci, carry):
                c0 = ci * (cb * SC_LANES)
                accs = tuple(out_v[tl, pl.ds(c0 + j * SC_LANES, SC_LANES)] for j in range(cb))

                def rbody(r, accs):
                    wr = plsc.load_gather(w_v, [tls, jnp.full((SC_LANES,), g * unit_rows + r, jnp.int32)])
                    return tuple(accs[j] + wr * r_ref[r, pl.ds(c0 + j * SC_LANES, SC_LANES)]
                                 for j in range(cb))

                accs = lax.fori_loop(0, unit_rows, rbody, accs)
                for j in range(cb):
                    out_v[tl, pl.ds(c0 + j * SC_LANES, SC_LANES)] = accs[j]
                return carry

            lax.fori_loop(0, nchunk // cb, cb_body, 0)

        def batch(b, carry):
            t0 = wid * tpw + b * tb
            pltpu.sync_copy(w_hbm.at[pl.ds(t0, tb)], w_v)
            pltpu.sync_copy(idx_hbm.at[pl.ds(t0, tb)], idx_v)

            def zbody(i, carry):
                out_v[i // nchunk, pl.ds((i % nchunk) * SC_LANES, SC_LANES)] = zero
                return carry

            lax.fori_loop(0, tb * nchunk, zbody, 0)
            _sc_pipeline(idx_v, v_hbm, (rows0, rows1), (sem0, sem1), tb * upt, upt, compute)
            pltpu.sync_copy(out_v, out_hbm.at[pl.ds(t0, tb)])
            return carry

        lax.fori_loop(0, tpw // tb, batch, 0)

    return pl.kernel(
        body, mesh=mesh,
        compiler_params=pltpu.CompilerParams(needs_layout_passes=False),
        out_type=jax.ShapeDtypeStruct((n, d), F32),
        scratch_types=[pltpu.VMEM((tb, kk), F32), pltpu.VMEM((tb, kk), jnp.int32),
                       pltpu.VMEM((tb, d), F32),
                       pltpu.VMEM((unit_rows, d), F32), pltpu.VMEM((unit_rows, d), F32),
                       pltpu.SemaphoreType.DMA, pltpu.SemaphoreType.DMA],
        name="peer_out",
    )(w, idx, v)


SUBLANES = 8
TC_PEER_TOKENS = 8
ISSUE_UNROLL = 8


def _gelu(a):
    return a * (lax.erf(a / np.sqrt(2.0).astype(np.float32)) + 1.0) / 2.0


def _peer_tc_body(idx_ref, idx_next_ref, x_ref, g_ref, u_hbm, v_hbm, o_ref, ubuf, vbuf, sem_u, sem_v):
    tokens = x_ref.shape[0]
    kk = g_ref.shape[1]
    n_rows = tokens * kk
    step = pl.program_id(0)
    slot = step % 2

    def row_copy(table, buf, sem, expert, j):
        src = table.at[pl.ds(pl.multiple_of(expert * SUBLANES, SUBLANES), SUBLANES)]
        return pltpu.make_async_copy(src, buf.at[pl.ds(j * SUBLANES, SUBLANES)], sem)

    def issue_block(ids_ref, into):
        def issue(i, carry):
            for r in range(ISSUE_UNROLL):
                j = i * ISSUE_UNROLL + r
                expert = ids_ref[j]
                row_copy(u_hbm, ubuf.at[into], sem_u.at[into], expert, j).start()
                row_copy(v_hbm, vbuf.at[into], sem_v.at[into], expert, j).start()
            return carry

        lax.fori_loop(0, n_rows // ISSUE_UNROLL, issue, 0)

    @pl.when(step == 0)
    def _():
        issue_block(idx_ref, 0)

    @pl.when(step + 1 < pl.num_programs(0))
    def _():
        issue_block(idx_next_ref, 1 - slot)

    def wait_all(table, buf, sem):
        pltpu.make_async_copy(table.at[pl.ds(0, n_rows * SUBLANES)], buf.at[slot], sem.at[slot]).wait()

    def rows_of(buf, t, s):
        return buf[slot, pl.ds(t * kk * SUBLANES + s, kk, stride=SUBLANES), :]

    wait_all(u_hbm, ubuf, sem_u)
    acts = []
    for t in range(tokens):
        acc = rows_of(ubuf, t, 0) * x_ref[t:t + 1, 0:LANES]
        for s in range(1, SUBLANES):
            acc = acc + rows_of(ubuf, t, s) * x_ref[t:t + 1, s * LANES:(s + 1) * LANES]
        acts.append(jnp.sum(acc, axis=1, keepdims=True))
    w = g_ref[0] * _gelu(jnp.concatenate(acts, axis=1))

    wait_all(v_hbm, vbuf, sem_v)
    for t in range(tokens):
        wt = w[:, t:t + 1]
        for s in range(SUBLANES):
            o_ref[t:t + 1, s * LANES:(s + 1) * LANES] = jnp.sum(rows_of(vbuf, t, s) * wt, axis=0, keepdims=True)


def _peer_tc(hn, idx, gate_t, u_tiles, v_tiles):
    m, d = hn.shape
    kk = idx.shape[1]
    tb = TC_PEER_TOKENS
    assert d == SUBLANES * LANES and m % tb == 0
    steps = m // tb
    g3 = gate_t.reshape(kk, steps, tb).transpose(1, 0, 2)
    rows = tb * kk * SUBLANES
    ids = idx.reshape(m * kk)
    return pl.pallas_call(
        _peer_tc_body,
        grid=(steps,),
        in_specs=[pl.BlockSpec((tb * kk,), lambda i: (i,), memory_space=pltpu.SMEM),
                  pl.BlockSpec((tb * kk,), lambda i: (jnp.minimum(i + 1, steps - 1),), memory_space=pltpu.SMEM),
                  pl.BlockSpec((tb, d), lambda i: (i, 0)),
                  pl.BlockSpec((1, kk, tb), lambda i: (i, 0, 0)),
                  pl.BlockSpec(memory_space=pl.ANY),
                  pl.BlockSpec(memory_space=pl.ANY)],
        out_specs=pl.BlockSpec((tb, d), lambda i: (i, 0)),
        out_shape=jax.ShapeDtypeStruct((m, d), F32),
        scratch_shapes=[pltpu.VMEM((2, rows, LANES), F32), pltpu.VMEM((2, rows, LANES), F32),
                        pltpu.SemaphoreType.DMA((2,)), pltpu.SemaphoreType.DMA((2,))],
        compiler_params=pltpu.CompilerParams(dimension_semantics=("arbitrary",),
                                             vmem_limit_bytes=VMEM_LIMIT),
        name="peer_tc",
    )(ids, ids, hn, g3, u_tiles, v_tiles)


def _gelu_gate_body(a_ref, g_ref, o_ref):
    o_ref[...] = g_ref[...] * _gelu(a_ref[...])


def _gelu_gate(act, gate, tm=SHARE_GRANULE):
    n, kk = act.shape
    tm = min(tm, n)
    assert n % tm == 0
    spec = pl.BlockSpec((tm, kk), lambda i: (i, 0))
    return pl.pallas_call(
        _gelu_gate_body, grid=(n // tm,), in_specs=[spec, spec], out_specs=spec,
        out_shape=jax.ShapeDtypeStruct((n, kk), F32),
        compiler_params=pltpu.CompilerParams(dimension_semantics=("parallel",)),
        name="gelu_gate",
    )(act, gate)


def _final_body(h_ref, g_ref, *rest, starts, aliased):
    o_ref = rest[-1]
    parts = rest[:len(starts)]
    assert len(rest) == len(starts) + 1 + int(aliased)
    peer = parts[0][...]
    for start, p_ref in zip(starts[1:], parts[1:]):
        peer = jnp.where(pl.program_id(0) >= start, p_ref[...], peer)
    h = h_ref[...] + peer
    o_ref[...] = h * lax.rsqrt(jnp.mean(h * h, axis=-1, keepdims=True) + EPS) * g_ref[...]


def _final(h1, peer_parts, g, out, row0, n_total, tm=256):
    n, d = h1.shape
    sizes = [p.shape[0] for p in peer_parts]
    assert sum(sizes) == n and all(m > 0 and m % tm == 0 for m in sizes) and row0 % tm == 0
    starts = tuple(int(v) // tm for v in np.cumsum([0] + sizes[:-1]))

    def part_spec(start, size):
        return pl.BlockSpec((tm, d), lambda i: (jnp.clip(i - start, 0, size // tm - 1), 0))

    in_specs = [pl.BlockSpec((tm, d), lambda i: (i, 0)), pl.BlockSpec((1, d), lambda i: (0, 0))]
    in_specs += [part_spec(s, m) for s, m in zip(starts, sizes)]
    args = [h1, g, *peer_parts]
    aliases = {}
    if out is not None:
        in_specs.append(pl.BlockSpec(memory_space=pl.ANY))
        aliases = {len(args): 0}
        args.append(out)
    return pl.pallas_call(
        functools.partial(_final_body, starts=starts, aliased=out is not None), grid=(n // tm,),
        in_specs=in_specs,
        out_specs=pl.BlockSpec((tm, d), lambda i: (i + row0 // tm, 0)),
        out_shape=jax.ShapeDtypeStruct((n_total, d), F32),
        input_output_aliases=aliases,
        compiler_params=pltpu.CompilerParams(dimension_semantics=("parallel",)),
        name="final_norm",
    )(*args)


def _tc_share(n_tokens):
    m = int(n_tokens * TC_SHARE) // SHARE_GRANULE * SHARE_GRANULE
    return min(max(m, SHARE_GRANULE), n_tokens - SHARE_GRANULE)


def kernel(x, norm1_g, w_in, hg_lower_logits, hg_norm_g, gla_w_gate_up, gla_b_gate, gla_norm_g,
           w_out, norm2_g, peer_w_q, peer_sub_keys, peer_u, peer_v, norm_f_g):
    batch, seq, d = x.shape
    depth = w_in.shape[0]
    assert depth == 1, "single-layer block"
    layer = 0

    w = w_in[layer]
    pad = LANES - GLA_GATE_RANK
    w_r = jnp.concatenate(
        [w[:, :3072], w[:, 3072:3072 + GLA_GATE_RANK], jnp.zeros((d, pad), w.dtype),
         w[:, 3072 + GLA_GATE_RANK:]], axis=1).astype(BF16)
    wgu = jnp.concatenate([gla_w_gate_up[layer], jnp.zeros((pad, GLA_KEY), F32)], axis=0).astype(BF16)

    wo = w_out[layer].astype(BF16)
    wq = peer_w_q[layer].astype(BF16)
    keys = peer_sub_keys[layer].reshape(PEER_HEADS * 2, PEER_N_KEYS, PEER_HALF).astype(BF16)
    n_experts = peer_u.shape[1]
    u_tiles = peer_u[layer].reshape(n_experts * SUBLANES, LANES)
    v_tiles = peer_v[layer].reshape(n_experts * SUBLANES, LANES)

    groups = PIPELINE_GROUPS if batch % PIPELINE_GROUPS == 0 else 1
    bpg = batch // groups
    ng = bpg * seq
    m_tc = _tc_share(ng)
    gf = norm_f_g.reshape(1, d)
    x2 = x.reshape(batch * seq, d)

    def after(value, anchor):
        value, _ = lax.optimization_barrier((value, anchor))
        return value

    half = ng // 2
    assert m_tc <= half
    h1s, tc_ins, sc_units = [], [], []
    for gi in range(groups):
        proj = _inproj(x2, gi * ng, ng, norm1_g[layer].reshape(1, d), w_r)
        mixed = _mixers(proj, bpg, seq, hg_lower_logits, hg_norm_g[layer].reshape(1, HEAD_V), wgu,
                        gla_b_gate[layer].reshape(1, GLA_KEY), gla_norm_g[layer].reshape(1, HEAD_V))
        h1, hn, q3 = _outproj(x2, gi * ng, mixed, wo, norm2_g[layer].reshape(1, d), wq)
        h1s.append(h1)
        m_g = m_tc - SHARE_GRANULE if 0 < gi == groups - 1 and m_tc > SHARE_GRANULE else m_tc
        bounds = (0, half, ng) if gi == 0 and m_g < half else (0, ng)
        units = []
        for lo, hi in zip(bounds[:-1], bounds[1:]):
            idx_t, gate_t = _route(q3, keys, lo, hi - lo)
            idx = idx_t.T
            if lo == 0:
                tc_ins.append((hn[:m_g], idx[:m_g], gate_t[:, :m_g]))
            s0 = max(lo, m_g) - lo
            units.append(dict(idx=idx[s0:], gate=gate_t[:, s0:].T,
                              act=_peer_act(hn[lo + s0:hi], idx[s0:], peer_u[layer])))
        sc_units.append(units)

    def sc_out(gi, anchor):
        outs = []
        for unit in sc_units[gi]:
            wts = _gelu_gate(after(unit["act"], anchor), unit["gate"])
            outs.append(_peer_out(wts, unit["idx"], peer_v[layer]))
        return wts, outs

    rows, peer_sc = [None] * groups, [None] * groups
    rows[0] = _peer_tc(*after(tc_ins[0], sc_units[-1][-1]["idx"]), u_tiles, v_tiles)
    anchor = rows[0]
    for gi in range(groups - 1):
        anchor, peer_sc[gi] = sc_out(gi, rows[0])
    if groups > 1:
        rows[1] = _peer_tc(*after(tc_ins[1], anchor), u_tiles, v_tiles)
        anchor = rows[1]
    anchor, peer_sc[groups - 1] = sc_out(groups - 1, anchor)
    for gi in range(2, groups):
        rows[gi] = _peer_tc(*after(tc_ins[gi], anchor), u_tiles, v_tiles)
        anchor = rows[gi]
    out = None
    for gi in range(groups):
        out = _final(h1s[gi], [rows[gi], *peer_sc[gi]], gf, out, gi * ng, batch * seq)
    return out.reshape(batch, seq, d)
```

```python
import functools

import jax
import jax.numpy as jnp
import numpy as np
from jax import lax
from jax.experimental import pallas as pl
from jax.experimental.pallas import tpu as pltpu
from jax.experimental.pallas import tpu_sc as plsc

F32 = jnp.float32
BF16 = jnp.bfloat16
EPS = 1e-6

D_MODEL = 1024
HG_HEADS = 4
GLA_HEADS = 4
HEAD_V = 128
HG_KEY = 512
GLA_KEY = 256
GLA_HEAD_K = 64
GLA_GATE_RANK = 16
GLA_GATE_NORMALIZER = 16.0
CHUNK = 64
LANES = 128
PEER_HEADS = 8
PEER_N_KEYS = 128
PEER_HALF = 128
PEER_TOPK = 16

OFF_HQ, OFF_HF, OFF_HI, OFF_HGATE = 0, 512, 1024, 1536
OFF_GQ, OFF_GK, OFF_GV, OFF_GLOW, OFF_GGATE = 2048, 2304, 2560, 3072, 3200
IN_WIDTH_PADDED = 3712

VMEM_LIMIT = 48 * 1024 * 1024

SC_LANES = 16
SC_WORKERS = 32
PIPELINE_GROUPS = 4
TC_SHARE = 0.25
SHARE_GRANULE = 256


def _nt(a, b):
    return lax.dot_general(a, b, (((1,), (1,)), ((), ())), preferred_element_type=F32)


def _tn(a, b):
    return lax.dot_general(a, b, (((0,), (0,)), ((), ())), preferred_element_type=F32)


def _inproj_body(x_ref, g_ref, w_ref, o_ref):
    x = x_ref[...]
    xn = x * lax.rsqrt(jnp.mean(x * x, axis=-1, keepdims=True) + EPS) * g_ref[...]
    o_ref[...] = jnp.dot(xn.astype(BF16), w_ref[...], preferred_element_type=F32)


def _inproj(x2, row0, n, g, w, tm=256):
    d = x2.shape[1]
    wd = w.shape[1]
    assert row0 % tm == 0 and n % tm == 0
    return pl.pallas_call(
        _inproj_body,
        grid=(n // tm,),
        in_specs=[pl.BlockSpec((tm, d), lambda i: (i + row0 // tm, 0)),
                  pl.BlockSpec((1, d), lambda i: (0, 0)),
                  pl.BlockSpec((d, wd), lambda i: (0, 0))],
        out_specs=pl.BlockSpec((tm, wd), lambda i: (i, 0)),
        out_shape=jax.ShapeDtypeStruct((n, wd), F32),
        compiler_params=pltpu.CompilerParams(dimension_semantics=("parallel",),
                                             vmem_limit_bytes=VMEM_LIMIT),
        name="inproj",
    )(x2, g, w)


def _level_constants():
    c = CHUNK
    mats = [np.tril(np.ones((c, c), np.float32))]
    level = np.full((c, c), -1, np.int32)
    b, lvl = c // 2, 0
    while b >= 1:
        m = np.zeros((c, c), np.float32)
        for s in range(0, c, 2 * b):
            mid = s + b
            for i in range(mid, s + 2 * b):
                m[i, mid:i + 1] = 1.0
                level[i, s:mid] = lvl
            for j in range(s, mid):
                m[j, j + 1:mid] = 1.0
        mats.append(m)
        b //= 2
        lvl += 1
    level[np.arange(c), np.arange(c)] = lvl
    return np.concatenate(mats, axis=0), level, lvl


_SEG_MATS, _LEVEL_MAP, _N_LEVELS = _level_constants()


def _split3(a):
    hi = a.astype(BF16)
    r = a - hi.astype(F32)
    mid = r.astype(BF16)
    lo = (r - mid.astype(F32)).astype(BF16)
    return jnp.concatenate([hi, mid, lo], axis=1)


def _gla_chunk(q, k, g, v, st, seg, level):
    c = CHUNK
    ex3 = jnp.dot(seg, _split3(g), preferred_element_type=F32)
    ex = ex3[:, 0:LANES] + ex3[:, LANES:2 * LANES] + ex3[:, 2 * LANES:3 * LANES]
    cum = ex[0:c]
    scores = jnp.where(level == _N_LEVELS, _nt(q.astype(BF16), k.astype(BF16)), 0.0)
    for l in range(_N_LEVELS):
        e = jnp.exp(ex[c * (l + 1):c * (l + 2)])
        p = _nt((q * e).astype(BF16), (k * e).astype(BF16))
        scores = jnp.where(level == l, p, scores)
    last = cum[c - 1:c, :]
    qd = (q * jnp.exp(cum)).astype(BF16)
    kd = (k * jnp.exp(last - cum)).astype(BF16)
    vb = v.astype(BF16)
    o = jnp.dot(scores.astype(BF16), vb, preferred_element_type=F32) + _nt(qd, st.astype(BF16))
    st_new = st * jnp.exp(last) + _tn(vb, kd)
    return o, st_new


def _head_out(o, gain, gate):
    on = o * lax.rsqrt(jnp.mean(o * o, axis=-1, keepdims=True) + EPS) * gain
    return on * (gate * jax.nn.sigmoid(gate))


def _mixer_body(p_ref, seg_ref, level_ref, lbl_ref, hgn_ref, wgu_ref, bg_ref, ggn_ref,
                o_ref, st_ref, *, chunks):
    @pl.when(pl.program_id(1) == 0)
    def _():
        st_ref[...] = jnp.zeros_like(st_ref)

    seg = seg_ref[...]
    level = level_ref[...]
    logits = lbl_ref[...]
    ez = jnp.exp(logits - jnp.max(logits, axis=0, keepdims=True))
    lb = ez[0:1, :] / jnp.sum(ez, axis=0, keepdims=True)
    lane = lax.broadcasted_iota(jnp.int32, (CHUNK, LANES), 1)

    def chunk_body(ci, carry):
        rows = pl.ds(pl.multiple_of(ci * CHUNK, CHUNK), CHUNK)

        def col(off):
            return p_ref[rows, pl.ds(off, LANES)]

        for h in range(HG_HEADS):
            hq = col(OFF_HQ + h * LANES)
            q = hq * jax.nn.sigmoid(hq)
            lbh = lb[:, h * LANES:(h + 1) * LANES]
            forget = lbh + (1.0 - lbh) * jax.nn.sigmoid(col(OFF_HF + h * LANES))
            o, st = _gla_chunk(q, 1.0 - forget, jnp.log(forget), col(OFF_HI + h * LANES),
                               st_ref[h], seg, level)
            st_ref[h] = st
            o_ref[rows, pl.ds(h * HEAD_V, HEAD_V)] = _head_out(
                o, hgn_ref[...], col(OFF_HGATE + h * HEAD_V)).astype(o_ref.dtype)

        zg = jnp.dot(col(OFF_GLOW).astype(BF16), wgu_ref[...], preferred_element_type=F32) + bg_ref[...]
        log_g = (jnp.minimum(zg, 0.0) - jnp.log(1.0 + jnp.exp(-jnp.abs(zg)))) / GLA_GATE_NORMALIZER
        for h in range(GLA_HEADS):
            pair, half = h // 2, h % 2
            q = col(OFF_GQ + pair * LANES) * (GLA_HEAD_K ** -0.5)
            in_head = (lane >= half * GLA_HEAD_K) & (lane < (half + 1) * GLA_HEAD_K)
            k = jnp.where(in_head, col(OFF_GK + pair * LANES), 0.0)
            g = log_g[:, pair * LANES:(pair + 1) * LANES]
            o, st = _gla_chunk(q, k, g, col(OFF_GV + h * HEAD_V), st_ref[HG_HEADS + h], seg, level)
            st_ref[HG_HEADS + h] = st
            o_ref[rows, pl.ds((HG_HEADS + h) * HEAD_V, HEAD_V)] = _head_out(
                o, ggn_ref[...], col(OFF_GGATE + h * HEAD_V)).astype(o_ref.dtype)
        return carry

    lax.fori_loop(0, chunks, chunk_body, 0)


def _mixers(proj, batch, seq, lb_logits, hg_norm_g, wgu, bg, gla_norm_g, tt=512):
    n = batch * seq
    steps = seq // tt
    heads = HG_HEADS + GLA_HEADS
    const = lambda shape: pl.BlockSpec(shape, lambda b, t: (0,) * len(shape))
    return pl.pallas_call(
        functools.partial(_mixer_body, chunks=tt // CHUNK),
        grid=(batch, steps),
        in_specs=[pl.BlockSpec((tt, IN_WIDTH_PADDED), lambda b, t: (b * steps + t, 0)),
                  const(_SEG_MATS.shape), const(_LEVEL_MAP.shape),
                  const(lb_logits.shape), const((1, HEAD_V)),
                  const(wgu.shape), const(bg.shape), const((1, HEAD_V))],
        out_specs=pl.BlockSpec((tt, heads * HEAD_V), lambda b, t: (b * steps + t, 0)),
        out_shape=jax.ShapeDtypeStruct((n, heads * HEAD_V), BF16),
        scratch_shapes=[pltpu.VMEM((heads, HEAD_V, LANES), F32)],
        compiler_params=pltpu.CompilerParams(dimension_semantics=("parallel", "arbitrary"),
                                             vmem_limit_bytes=VMEM_LIMIT),
        name="mixers",
    )(proj, jnp.asarray(_SEG_MATS, BF16), jnp.asarray(_LEVEL_MAP), lb_logits, hg_norm_g,
      wgu, bg, gla_norm_g)


def _outproj_body(x_ref, m_ref, wo_ref, g_ref, wq_ref, h_ref, hn_ref, q_ref):
    h = x_ref[...] + jnp.dot(m_ref[...], wo_ref[...], preferred_element_type=F32)
    h_ref[...] = h
    hn = h * lax.rsqrt(jnp.mean(h * h, axis=-1, keepdims=True) + EPS) * g_ref[...]
    hn_ref[...] = hn
    q = jnp.dot(hn.astype(BF16), wq_ref[...], preferred_element_type=F32)
    for j in range(q_ref.shape[0]):
        q_ref[j] = q[:, j * PEER_HALF:(j + 1) * PEER_HALF]


def _outproj(x2, row0, mixed, wo, g2, wq, tm=256):
    n, d = mixed.shape[0], x2.shape[1]
    nq = wq.shape[1] // PEER_HALF
    assert row0 % tm == 0 and n % tm == 0
    return pl.pallas_call(
        _outproj_body,
        grid=(n // tm,),
        in_specs=[pl.BlockSpec((tm, d), lambda i: (i + row0 // tm, 0)),
                  pl.BlockSpec((tm, mixed.shape[1]), lambda i: (i, 0)),
                  pl.BlockSpec(wo.shape, lambda i: (0, 0)),
                  pl.BlockSpec((1, d), lambda i: (0, 0)),
                  pl.BlockSpec(wq.shape, lambda i: (0, 0))],
        out_specs=[pl.BlockSpec((tm, d), lambda i: (i, 0)),
                   pl.BlockSpec((tm, d), lambda i: (i, 0)),
                   pl.BlockSpec((nq, tm, PEER_HALF), lambda i: (0, i, 0))],
        out_shape=[jax.ShapeDtypeStruct((n, d), F32),
                   jax.ShapeDtypeStruct((n, d), F32),
                   jax.ShapeDtypeStruct((nq, n, PEER_HALF), F32)],
        compiler_params=pltpu.CompilerParams(dimension_semantics=("parallel",),
                                             vmem_limit_bytes=VMEM_LIMIT),
        name="outproj",
    )(x2, mixed, wo, g2, wq)


def _candidate_constants():
    k = PEER_TOPK
    pos, valid = [], []
    for j in range(k):
        pos.append(j); valid.append(True)
    for i in range(1, 8):
        for j in range(8):
            pos.append(i * k + j); valid.append((i + 1) * (j + 1) <= k)
    for i in range(8, k):
        pos.append(i * k); valid.append(True)
    return np.asarray(pos, np.float32), np.asarray(valid, bool)


_CAND_POS, _CAND_VALID = _candidate_constants()
_N_CAND = _CAND_POS.shape[0]


def _topk_rows(ss, pos, extras, k):
    t = ss[0].shape[1]
    slot = lax.broadcasted_iota(jnp.int32, (k, t), 0)

    def body(it, carry):
        out = []
        for (s, vals, poss, exts), extra in zip(carry, extras):
            m = jnp.max(s, axis=0, keepdims=True)
            p = jnp.min(jnp.where(s == m, pos, 1e9), axis=0, keepdims=True)
            hit = pos == p
            vals = jnp.where(slot == it, m, vals)
            poss = jnp.where(slot == it, p, poss)
            if extra is not None:
                x = jnp.max(jnp.where(hit, extra, -1.0), axis=0, keepdims=True)
                exts = jnp.where(slot == it, x, exts)
            out.append((jnp.where(hit, -jnp.inf, s), vals, poss, exts))
        return tuple(out)

    z = jnp.zeros((k, t), F32)
    res = lax.fori_loop(0, k, body, tuple((s, z, z, z) for s in ss))
    return [(vals, poss, exts) for _, vals, poss, exts in res]


def _route_body(q_ref, keys_ref, cpos_ref, cvalid_ref, idx_ref, gate_ref):
    k = PEER_TOPK
    n_tiles = q_ref.shape[1] // LANES
    key_pos = lax.broadcasted_iota(jnp.int32, (PEER_N_KEYS, LANES), 0).astype(F32)
    cpos = cpos_ref[...]
    cvalid = cvalid_ref[...] > 0.5
    scores = [_nt(keys_ref[p], q_ref[p, pl.ds(lt * LANES, LANES), :].astype(BF16))
              for lt in range(n_tiles) for p in range(2)]
    tops = _topk_rows(scores, key_pos, [None] * len(scores), k)
    cands, experts = [], []
    for lt in range(n_tiles):
        (v0, i0, _), (v1, i1, _) = tops[2 * lt], tops[2 * lt + 1]
        vals = [v0[0:1] + v1]
        exps = [i0[0:1] * PEER_N_KEYS + i1]
        for i in range(1, 8):
            vals.append(v0[i:i + 1] + v1[0:8])
            exps.append(i0[i:i + 1] * PEER_N_KEYS + i1[0:8])
        vals.append(v0[8:k] + v1[0:1])
        exps.append(i0[8:k] * PEER_N_KEYS + i1[0:1])
        cands.append(jnp.where(cvalid, jnp.concatenate(vals, axis=0), -jnp.inf))
        experts.append(jnp.concatenate(exps, axis=0))
    picks = _topk_rows(cands, cpos, experts, k)
    for lt in range(n_tiles):
        best, _, expert = picks[lt]
        cols = pl.ds(lt * LANES, LANES)
        e = jnp.exp(best - jnp.max(best, axis=0, keepdims=True))
        gate_ref[:, cols] = e / jnp.sum(e, axis=0, keepdims=True)
        idx_ref[:, cols] = expert.astype(jnp.int32)


def _route(q3, keys, tok0, n):
    heads = q3.shape[0] // 2
    tb = 512 if tok0 % 512 == 0 and n % 512 == 0 else SHARE_GRANULE
    assert tok0 % tb == 0 and n % tb == 0
    cpos = jnp.broadcast_to(jnp.asarray(_CAND_POS)[:, None], (_N_CAND, LANES))
    cvalid = jnp.broadcast_to(jnp.asarray(_CAND_VALID, F32)[:, None], (_N_CAND, LANES))
    return pl.pallas_call(
        _route_body,
        grid=(n // tb, heads),
        in_specs=[pl.BlockSpec((2, tb, PEER_HALF), lambda i, h: (h, i + tok0 // tb, 0)),
                  pl.BlockSpec((2, PEER_N_KEYS, PEER_HALF), lambda i, h: (h, 0, 0)),
                  pl.BlockSpec((_N_CAND, LANES), lambda i, h: (0, 0)),
                  pl.BlockSpec((_N_CAND, LANES), lambda i, h: (0, 0))],
        out_specs=[pl.BlockSpec((PEER_TOPK, tb), lambda i, h: (h, i)),
                   pl.BlockSpec((PEER_TOPK, tb), lambda i, h: (h, i))],
        out_shape=[jax.ShapeDtypeStruct((heads * PEER_TOPK, n), jnp.int32),
                   jax.ShapeDtypeStruct((heads * PEER_TOPK, n), F32)],
        compiler_params=pltpu.CompilerParams(dimension_semantics=("parallel", "parallel"),
                                             vmem_limit_bytes=VMEM_LIMIT),
        name="route",
    )(q3, keys, cpos, cvalid)


def _sc_pipeline(idx_v, table_hbm, rows, sems, n_units, units_per_token, compute):
    parts = rows[0].shape[0] // SC_LANES
    assert n_units % 2 == 0

    def start(unit, slot):
        tl = unit // units_per_token
        g = unit % units_per_token
        for part in range(parts):
            iv = idx_v[tl, pl.ds((g * parts + part) * SC_LANES, SC_LANES)]
            pltpu.async_copy(table_hbm.at[iv], rows[slot].at[pl.ds(part * SC_LANES, SC_LANES)], sems[slot])

    def wait(slot):
        iv = idx_v[0, pl.ds(0, SC_LANES)]
        for part in range(parts):
            pltpu.make_async_copy(table_hbm.at[iv], rows[slot].at[pl.ds(part * SC_LANES, SC_LANES)],
                                  sems[slot]).wait()

    start(0, 0)

    def pair(i, carry):
        u0 = 2 * i
        start(u0 + 1, 1)
        wait(0)
        compute(u0 // units_per_token, u0 % units_per_token, rows[0])

        @pl.when(u0 + 2 < n_units)
        def _():
            start(u0 + 2, 0)

        wait(1)
        compute((u0 + 1) // units_per_token, (u0 + 1) % units_per_token, rows[1])
        return carry

    lax.fori_loop(0, n_units // 2, pair, 0)


def _sc_batch(tokens_per_worker):
    for tb in (40, 32, 24, 16, 8):
        if tokens_per_worker % tb == 0:
            return tb
    raise ValueError(f"tokens per SparseCore worker ({tokens_per_worker}) must be a multiple of 8")


def _peer_act(hn, idx, u, unit_rows=2 * SC_LANES):
    n, d = hn.shape
    kk = idx.shape[1]
    tpw = n // SC_WORKERS
    tb = _sc_batch(tpw)
    upt = kk // unit_rows
    parts = unit_rows // SC_LANES
    nchunk = d // SC_LANES
    mesh = plsc.VectorSubcoreMesh(core_axis_name="c", subcore_axis_name="s")

    def body(x_hbm, idx_hbm, u_hbm, out_hbm, x_v, idx_v, act_v, rows0, rows1, tmp_v, sem0, sem1):
        wid = lax.axis_index("s") * 2 + lax.axis_index("c")
        lane_row = lax.iota(jnp.int32, SC_LANES) * SC_LANES

        def compute(tl, g, r_ref):
            for part in range(parts):
                row0 = part * SC_LANES

                def cbody(c, accs, row0=row0):
                    xv = x_v[tl, pl.ds(c * SC_LANES, SC_LANES)]
                    return tuple(accs[r] + r_ref[row0 + r, pl.ds(c * SC_LANES, SC_LANES)] * xv
                                 for r in range(SC_LANES))
                accs = lax.fori_loop(0, nchunk, cbody,
                                     tuple(jnp.zeros((SC_LANES,), F32) for _ in range(SC_LANES)))
                for r in range(SC_LANES):
                    tmp_v[pl.ds(r * SC_LANES, SC_LANES)] = accs[r]
                tot = plsc.load_gather(tmp_v, [lane_row])
                for j in range(1, SC_LANES):
                    tot = tot + plsc.load_gather(tmp_v, [lane_row + j])
                act_v[tl, pl.ds((g * parts + part) * SC_LANES, SC_LANES)] = tot

        def batch(b, carry):
            t0 = wid * tpw + b * tb
            pltpu.sync_copy(x_hbm.at[pl.ds(t0, tb)], x_v)
            pltpu.sync_copy(idx_hbm.at[pl.ds(t0, tb)], idx_v)
            _sc_pipeline(idx_v, u_hbm, (rows0, rows1), (sem0, sem1), tb * upt, upt, compute)
            pltpu.sync_copy(act_v, out_hbm.at[pl.ds(t0, tb)])
            return carry

        lax.fori_loop(0, tpw // tb, batch, 0)

    return pl.kernel(
        body, mesh=mesh,
        compiler_params=pltpu.CompilerParams(needs_layout_passes=False),
        out_type=jax.ShapeDtypeStruct((n, kk), F32),
        scratch_types=[pltpu.VMEM((tb, d), F32), pltpu.VMEM((tb, kk), jnp.int32),
                       pltpu.VMEM((tb, kk), F32),
                       pltpu.VMEM((unit_rows, d), F32), pltpu.VMEM((unit_rows, d), F32),
                       pltpu.VMEM((SC_LANES * SC_LANES,), F32),
                       pltpu.SemaphoreType.DMA, pltpu.SemaphoreType.DMA],
        name="peer_act",
    )(hn, idx, u)


def _peer_out(w, idx, v, cb=16, unit_rows=2 * SC_LANES):
    n, kk = w.shape
    d = v.shape[1]
    tpw = n // SC_WORKERS
    tb = _sc_batch(tpw)
    upt = kk // unit_rows
    nchunk = d // SC_LANES
    mesh = plsc.VectorSubcoreMesh(core_axis_name="c", subcore_axis_name="s")

    def body(w_hbm, idx_hbm, v_hbm, out_hbm, w_v, idx_v, out_v, rows0, rows1, sem0, sem1):
        wid = lax.axis_index("s") * 2 + lax.axis_index("c")
        zero = jnp.zeros((SC_LANES,), F32)

        def compute(tl, g, r_ref):
            tls = jnp.full((SC_LANES,), tl, jnp.int32)

            def cb_body(ci, carry):
                c0 = ci * (cb * SC_LANES)
                accs = tuple(out_v[tl, pl.ds(c0 + j * SC_LANES, SC_LANES)] for j in range(cb))

                def rbody(r, accs):
                    wr = plsc.load_gather(w_v, [tls, jnp.full((SC_LANES,), g * unit_rows + r, jnp.int32)])
                    return tuple(accs[j] + wr * r_ref[r, pl.ds(c0 + j * SC_LANES, SC_LANES)]
                                 for j in range(cb))

                accs = lax.fori_loop(0, unit_rows, rbody, accs)
                for j in range(cb):
                    out_v[tl, pl.ds(c0 + j * SC_LANES, SC_LANES)] = accs[j]
                return carry

            lax.fori_loop(0, nchunk // cb, cb_body, 0)

        def batch(b, carry):
            t0 = wid * tpw + b * tb
            pltpu.sync_copy(w_hbm.at[pl.ds(t0, tb)], w_v)
            pltpu.sync_copy(idx_hbm.at[pl.ds(t0, tb)], idx_v)

            def zbody(i, carry):
                out_v[i // nchunk, pl.ds((i % nchunk) * SC_LANES, SC_LANES)] = zero
                return carry

            lax.fori_loop(0, tb * nchunk, zbody, 0)
            _sc_pipeline(idx_v, v_hbm, (rows0, rows1), (sem0, sem1), tb * upt, upt, compute)
            pltpu.sync_copy(out_v, out_hbm.at[pl.ds(t0, tb)])
            return carry

        lax.fori_loop(0, tpw // tb, batch, 0)

    return pl.kernel(
        body, mesh=mesh,
        compiler_params=pltpu.CompilerParams(needs_layout_passes=False),
        out_type=jax.ShapeDtypeStruct((n, d), F32),
        scratch_types=[pltpu.VMEM((tb, kk), F32), pltpu.VMEM((tb, kk), jnp.int32),
                       pltpu.VMEM((tb, d), F32),
                       pltpu.VMEM((unit_rows, d), F32), pltpu.VMEM((unit_rows, d), F32),
                       pltpu.SemaphoreType.DMA, pltpu.SemaphoreType.DMA],
        name="peer_out",
    )(w, idx, v)


SUBLANES = 8
TC_PEER_TOKENS = 8
ISSUE_UNROLL = 8


def _gelu(a):
    return a * (lax.erf(a / np.sqrt(2.0).astype(np.float32)) + 1.0) / 2.0


def _peer_tc_body(idx_ref, idx_next_ref, x_ref, g_ref, u_hbm, v_hbm, o_ref, ubuf, vbuf, sem_u, sem_v):
    tokens = x_ref.shape[0]
    kk = g_ref.shape[1]
    n_rows = tokens * kk
    step = pl.program_id(0)
    slot = step % 2

    def row_copy(table, buf, sem, expert, j):
        src = table.at[pl.ds(pl.multiple_of(expert * SUBLANES, SUBLANES), SUBLANES)]
        return pltpu.make_async_copy(src, buf.at[pl.ds(j * SUBLANES, SUBLANES)], sem)

    def issue_block(ids_ref, into):
        def issue(i, carry):
            for r in range(ISSUE_UNROLL):
                j = i * ISSUE_UNROLL + r
                expert = ids_ref[j]
                row_copy(u_hbm, ubuf.at[into], sem_u.at[into], expert, j).start()
                row_copy(v_hbm, vbuf.at[into], sem_v.at[into], expert, j).start()
            return carry

        lax.fori_loop(0, n_rows // ISSUE_UNROLL, issue, 0)

    @pl.when(step == 0)
    def _():
        issue_block(idx_ref, 0)

    @pl.when(step + 1 < pl.num_programs(0))
    def _():
        issue_block(idx_next_ref, 1 - slot)

    def wait_all(table, buf, sem):
        pltpu.make_async_copy(table.at[pl.ds(0, n_rows * SUBLANES)], buf.at[slot], sem.at[slot]).wait()

    def rows_of(buf, t, s):
        return buf[slot, pl.ds(t * kk * SUBLANES + s, kk, stride=SUBLANES), :]

    wait_all(u_hbm, ubuf, sem_u)
    acts = []
    for t in range(tokens):
        acc = rows_of(ubuf, t, 0) * x_ref[t:t + 1, 0:LANES]
        for s in range(1, SUBLANES):
            acc = acc + rows_of(ubuf, t, s) * x_ref[t:t + 1, s * LANES:(s + 1) * LANES]
        acts.append(jnp.sum(acc, axis=1, keepdims=True))
    w = g_ref[0] * _gelu(jnp.concatenate(acts, axis=1))

    wait_all(v_hbm, vbuf, sem_v)
    for t in range(tokens):
        wt = w[:, t:t + 1]
        for s in range(SUBLANES):
            o_ref[t:t + 1, s * LANES:(s + 1) * LANES] = jnp.sum(rows_of(vbuf, t, s) * wt, axis=0, keepdims=True)


def _peer_tc(hn, idx, gate_t, u_tiles, v_tiles):
    m, d = hn.shape
    kk = idx.shape[1]
    tb = TC_PEER_TOKENS
    assert d == SUBLANES * LANES and m % tb == 0
    steps = m // tb
    g3 = gate_t.reshape(kk, steps, tb).transpose(1, 0, 2)
    rows = tb * kk * SUBLANES
    ids = idx.reshape(m * kk)
    return pl.pallas_call(
        _peer_tc_body,
        grid=(steps,),
        in_specs=[pl.BlockSpec((tb * kk,), lambda i: (i,), memory_space=pltpu.SMEM),
                  pl.BlockSpec((tb * kk,), lambda i: (jnp.minimum(i + 1, steps - 1),), memory_space=pltpu.SMEM),
                  pl.BlockSpec((tb, d), lambda i: (i, 0)),
                  pl.BlockSpec((1, kk, tb), lambda i: (i, 0, 0)),
                  pl.BlockSpec(memory_space=pl.ANY),
                  pl.BlockSpec(memory_space=pl.ANY)],
        out_specs=pl.BlockSpec((tb, d), lambda i: (i, 0)),
        out_shape=jax.ShapeDtypeStruct((m, d), F32),
        scratch_shapes=[pltpu.VMEM((2, rows, LANES), F32), pltpu.VMEM((2, rows, LANES), F32),
                        pltpu.SemaphoreType.DMA((2,)), pltpu.SemaphoreType.DMA((2,))],
        compiler_params=pltpu.CompilerParams(dimension_semantics=("arbitrary",),
                                             vmem_limit_bytes=VMEM_LIMIT),
        name="peer_tc",
    )(ids, ids, hn, g3, u_tiles, v_tiles)


def _gelu_gate_body(a_ref, g_ref, o_ref):
    o_ref[...] = g_ref[...] * _gelu(a_ref[...])


def _gelu_gate(act, gate, tm=SHARE_GRANULE):
    n, kk = act.shape
    tm = min(tm, n)
    assert n % tm == 0
    spec = pl.BlockSpec((tm, kk), lambda i: (i, 0))
    return pl.pallas_call(
        _gelu_gate_body, grid=(n // tm,), in_specs=[spec, spec], out_specs=spec,
        out_shape=jax.ShapeDtypeStruct((n, kk), F32),
        compiler_params=pltpu.CompilerParams(dimension_semantics=("parallel",)),
        name="gelu_gate",
    )(act, gate)


def _final_body(h_ref, g_ref, *rest, starts, aliased):
    o_ref = rest[-1]
    parts = rest[:len(starts)]
    assert len(rest) == len(starts) + 1 + int(aliased)
    peer = parts[0][...]
    for start, p_ref in zip(starts[1:], parts[1:]):
        peer = jnp.where(pl.program_id(0) >= start, p_ref[...], peer)
    h = h_ref[...] + peer
    o_ref[...] = h * lax.rsqrt(jnp.mean(h * h, axis=-1, keepdims=True) + EPS) * g_ref[...]


def _final(h1, peer_parts, g, out, row0, n_total, tm=256):
    n, d = h1.shape
    sizes = [p.shape[0] for p in peer_parts]
    assert sum(sizes) == n and all(m > 0 and m % tm == 0 for m in sizes) and row0 % tm == 0
    starts = tuple(int(v) // tm for v in np.cumsum([0] + sizes[:-1]))

    def part_spec(start, size):
        return pl.BlockSpec((tm, d), lambda i: (jnp.clip(i - start, 0, size // tm - 1), 0))

    in_specs = [pl.BlockSpec((tm, d), lambda i: (i, 0)), pl.BlockSpec((1, d), lambda i: (0, 0))]
    in_specs += [part_spec(s, m) for s, m in zip(starts, sizes)]
    args = [h1, g, *peer_parts]
    aliases = {}
    if out is not None:
        in_specs.append(pl.BlockSpec(memory_space=pl.ANY))
        aliases = {len(args): 0}
        args.append(out)
    return pl.pallas_call(
        functools.partial(_final_body, starts=starts, aliased=out is not None), grid=(n // tm,),
        in_specs=in_specs,
        out_specs=pl.BlockSpec((tm, d), lambda i: (i + row0 // tm, 0)),
        out_shape=jax.ShapeDtypeStruct((n_total, d), F32),
        input_output_aliases=aliases,
        compiler_params=pltpu.CompilerParams(dimension_semantics=("parallel",)),
        name="final_norm",
    )(*args)


def _tc_share(n_tokens):
    m = int(n_tokens * TC_SHARE) // SHARE_GRANULE * SHARE_GRANULE
    return min(max(m, SHARE_GRANULE), n_tokens - SHARE_GRANULE)


def kernel(x, norm1_g, w_in, hg_lower_logits, hg_norm_g, gla_w_gate_up, gla_b_gate, gla_norm_g,
           w_out, norm2_g, peer_w_q, peer_sub_keys, peer_u, peer_v, norm_f_g):
    batch, seq, d = x.shape
    depth = w_in.shape[0]
    assert depth == 1, "single-layer block"
    layer = 0

    w = w_in[layer]
    pad = LANES - GLA_GATE_RANK
    w_r = jnp.concatenate(
        [w[:, :3072], w[:, 3072:3072 + GLA_GATE_RANK], jnp.zeros((d, pad), w.dtype),
         w[:, 3072 + GLA_GATE_RANK:]], axis=1).astype(BF16)
    wgu = jnp.concatenate([gla_w_gate_up[layer], jnp.zeros((pad, GLA_KEY), F32)], axis=0).astype(BF16)

    wo = w_out[layer].astype(BF16)
    wq = peer_w_q[layer].astype(BF16)
    keys = peer_sub_keys[layer].reshape(PEER_HEADS * 2, PEER_N_KEYS, PEER_HALF).astype(BF16)
    n_experts = peer_u.shape[1]
    u_tiles = peer_u[layer].reshape(n_experts * SUBLANES, LANES)
    v_tiles = peer_v[layer].reshape(n_experts * SUBLANES, LANES)

    groups = PIPELINE_GROUPS if batch % PIPELINE_GROUPS == 0 else 1
    bpg = batch // groups
    ng = bpg * seq
    m_tc = _tc_share(ng)
    gf = norm_f_g.reshape(1, d)
    x2 = x.reshape(batch * seq, d)

    def after(value, anchor):
        value, _ = lax.optimization_barrier((value, anchor))
        return value

    half = ng // 2
    h1s, tc_ins, sc_units = [], [], []
    for gi in range(groups):
        proj = _inproj(x2, gi * ng, ng, norm1_g[layer].reshape(1, d), w_r)
        mixed = _mixers(proj, bpg, seq, hg_lower_logits, hg_norm_g[layer].reshape(1, HEAD_V), wgu,
                        gla_b_gate[layer].reshape(1, GLA_KEY), gla_norm_g[layer].reshape(1, HEAD_V))
        h1, hn, q3 = _outproj(x2, gi * ng, mixed, wo, norm2_g[layer].reshape(1, d), wq)
        h1s.append(h1)
        m_g = m_tc - SHARE_GRANULE if 0 < gi == groups - 1 and m_tc > SHARE_GRANULE else m_tc
        bounds = (m_g, half, ng) if gi == 0 and m_g < half else (m_g, ng)
        units = []
        for lo, hi in zip(bounds[:-1], bounds[1:]):
            idx_t, gate_t = _route(q3, keys, lo, hi - lo)
            idx = idx_t.T
            units.append(dict(idx=idx, gate=gate_t.T, act=_peer_act(hn[lo:hi], idx, peer_u[layer])))
        sc_units.append(units)
        idx_t, gate_t = _route(q3, keys, 0, m_g)
        tc_ins.append((hn[:m_g], idx_t.T, gate_t))

    def sc_out(gi, anchor):
        outs = []
        for unit in sc_units[gi]:
            wts = _gelu_gate(after(unit["act"], anchor), unit["gate"])
            outs.append(_peer_out(wts, unit["idx"], peer_v[layer]))
        return wts, outs

    rows, peer_sc = [None] * groups, [None] * groups
    rows[0] = _peer_tc(*after(tc_ins[0], sc_units[-1][-1]["idx"]), u_tiles, v_tiles)
    anchor = rows[0]
    for gi in range(groups - 1):
        anchor, peer_sc[gi] = sc_out(gi, rows[0])
    if groups > 1:
        rows[1] = _peer_tc(*after(tc_ins[1], anchor), u_tiles, v_tiles)
        anchor = rows[1]
    anchor, peer_sc[groups - 1] = sc_out(groups - 1, anchor)
    for gi in range(2, groups):
        rows[gi] = _peer_tc(*after(tc_ins[gi], anchor), u_tiles, v_tiles)
        anchor = rows[gi]
    out = None
    for gi in range(groups):
        out = _final(h1s[gi], [rows[gi], *peer_sc[gi]], gf, out, gi * ng, batch * seq)
    return out.reshape(batch, seq, d)
```

```python
import functools

import jax
import jax.numpy as jnp
import numpy as np
from jax import lax
from jax.experimental import pallas as pl
from jax.experimental.pallas import tpu as pltpu
from jax.experimental.pallas import tpu_sc as plsc

F32 = jnp.float32
BF16 = jnp.bfloat16
EPS = 1e-6

HG_HEADS = 4
GLA_HEADS = 4
HEAD_V = 128
HG_KEY = 512
GLA_KEY = 256
GLA_HEAD_K = 64
GLA_GATE_RANK = 16
GLA_GATE_NORMALIZER = 16.0
CHUNK = 64
LANES = 128
PEER_HEADS = 8
PEER_N_KEYS = 128
PEER_HALF = 128
PEER_TOPK = 16

OFF_HQ, OFF_HF, OFF_HI, OFF_HGATE = 0, 512, 1024, 1536
OFF_GQ, OFF_GK, OFF_GV, OFF_GLOW, OFF_GGATE = 2048, 2304, 2560, 3072, 3200
IN_WIDTH_PADDED = 3712

VMEM_LIMIT = 48 * 1024 * 1024

SC_LANES = 16
SC_WORKERS = 32
PIPELINE_GROUPS = 4
TC_SHARE = 0.25
SHARE_GRANULE = 256


def _nt(a, b):
    return lax.dot_general(a, b, (((1,), (1,)), ((), ())), preferred_element_type=F32)


def _tn(a, b):
    return lax.dot_general(a, b, (((0,), (0,)), ((), ())), preferred_element_type=F32)


def _inproj_body(x_ref, g_ref, w_ref, o_ref):
    x = x_ref[...]
    xn = x * lax.rsqrt(jnp.mean(x * x, axis=-1, keepdims=True) + EPS) * g_ref[...]
    o_ref[...] = jnp.dot(xn.astype(BF16), w_ref[...], preferred_element_type=F32)


def _inproj(x2, row0, n, g, w, tm=256):
    d = x2.shape[1]
    wd = w.shape[1]
    assert row0 % tm == 0 and n % tm == 0
    return pl.pallas_call(
        _inproj_body,
        grid=(n // tm,),
        in_specs=[pl.BlockSpec((tm, d), lambda i: (i + row0 // tm, 0)),
                  pl.BlockSpec((1, d), lambda i: (0, 0)),
                  pl.BlockSpec((d, wd), lambda i: (0, 0))],
        out_specs=pl.BlockSpec((tm, wd), lambda i: (i, 0)),
        out_shape=jax.ShapeDtypeStruct((n, wd), F32),
        compiler_params=pltpu.CompilerParams(dimension_semantics=("parallel",),
                                             vmem_limit_bytes=VMEM_LIMIT),
        name="inproj",
    )(x2, g, w)


def _level_constants():
    c = CHUNK
    mats = [np.tril(np.ones((c, c), np.float32))]
    level = np.full((c, c), -1, np.int32)
    b, lvl = c // 2, 0
    while b >= 1:
        m = np.zeros((c, c), np.float32)
        for s in range(0, c, 2 * b):
            mid = s + b
            for i in range(mid, s + 2 * b):
                m[i, mid:i + 1] = 1.0
                level[i, s:mid] = lvl
            for j in range(s, mid):
                m[j, j + 1:mid] = 1.0
        mats.append(m)
        b //= 2
        lvl += 1
    level[np.arange(c), np.arange(c)] = lvl
    return np.concatenate(mats, axis=0), level, lvl


_SEG_MATS, _LEVEL_MAP, _N_LEVELS = _level_constants()


def _split3(a):
    hi = a.astype(BF16)
    r = a - hi.astype(F32)
    mid = r.astype(BF16)
    lo = (r - mid.astype(F32)).astype(BF16)
    return jnp.concatenate([hi, mid, lo], axis=1)


def _gla_chunk(q, k, g, v, st, seg, level):
    c = CHUNK
    ex3 = jnp.dot(seg, _split3(g), preferred_element_type=F32)
    ex = ex3[:, 0:LANES] + ex3[:, LANES:2 * LANES] + ex3[:, 2 * LANES:3 * LANES]
    cum = ex[0:c]
    scores = jnp.where(level == _N_LEVELS, _nt(q.astype(BF16), k.astype(BF16)), 0.0)
    for l in range(_N_LEVELS):
        e = jnp.exp(ex[c * (l + 1):c * (l + 2)])
        p = _nt((q * e).astype(BF16), (k * e).astype(BF16))
        scores = jnp.where(level == l, p, scores)
    last = cum[c - 1:c, :]
    qd = (q * jnp.exp(cum)).astype(BF16)
    kd = (k * jnp.exp(last - cum)).astype(BF16)
    vb = v.astype(BF16)
    o = jnp.dot(scores.astype(BF16), vb, preferred_element_type=F32) + _nt(qd, st.astype(BF16))
    st_new = st * jnp.exp(last) + _tn(vb, kd)
    return o, st_new


def _head_out(o, gain, gate):
    on = o * lax.rsqrt(jnp.mean(o * o, axis=-1, keepdims=True) + EPS) * gain
    return on * (gate * jax.nn.sigmoid(gate))


def _mixer_body(p_ref, seg_ref, level_ref, lbl_ref, hgn_ref, wgu_ref, bg_ref, ggn_ref,
                o_ref, st_ref, *, chunks):
    @pl.when(pl.program_id(1) == 0)
    def _():
        st_ref[...] = jnp.zeros_like(st_ref)

    seg = seg_ref[...]
    level = level_ref[...]
    logits = lbl_ref[...]
    ez = jnp.exp(logits - jnp.max(logits, axis=0, keepdims=True))
    lb = ez[0:1, :] / jnp.sum(ez, axis=0, keepdims=True)
    lane = lax.broadcasted_iota(jnp.int32, (CHUNK, LANES), 1)

    def chunk_body(ci, carry):
        rows = pl.ds(pl.multiple_of(ci * CHUNK, CHUNK), CHUNK)

        def col(off):
            return p_ref[rows, pl.ds(off, LANES)]

        for h in range(HG_HEADS):
            hq = col(OFF_HQ + h * LANES)
            q = hq * jax.nn.sigmoid(hq)
            lbh = lb[:, h * LANES:(h + 1) * LANES]
            forget = lbh + (1.0 - lbh) * jax.nn.sigmoid(col(OFF_HF + h * LANES))
            o, st = _gla_chunk(q, 1.0 - forget, jnp.log(forget), col(OFF_HI + h * LANES),
                               st_ref[h], seg, level)
            st_ref[h] = st
            o_ref[rows, pl.ds(h * HEAD_V, HEAD_V)] = _head_out(
                o, hgn_ref[...], col(OFF_HGATE + h * HEAD_V)).astype(o_ref.dtype)

        zg = jnp.dot(col(OFF_GLOW).astype(BF16), wgu_ref[...], preferred_element_type=F32) + bg_ref[...]
        log_g = (jnp.minimum(zg, 0.0) - jnp.log(1.0 + jnp.exp(-jnp.abs(zg)))) / GLA_GATE_NORMALIZER
        for h in range(GLA_HEADS):
            pair, half = h // 2, h % 2
            q = col(OFF_GQ + pair * LANES) * (GLA_HEAD_K ** -0.5)
            in_head = (lane >= half * GLA_HEAD_K) & (lane < (half + 1) * GLA_HEAD_K)
            k = jnp.where(in_head, col(OFF_GK + pair * LANES), 0.0)
            g = log_g[:, pair * LANES:(pair + 1) * LANES]
            o, st = _gla_chunk(q, k, g, col(OFF_GV + h * HEAD_V), st_ref[HG_HEADS + h], seg, level)
            st_ref[HG_HEADS + h] = st
            o_ref[rows, pl.ds((HG_HEADS + h) * HEAD_V, HEAD_V)] = _head_out(
                o, ggn_ref[...], col(OFF_GGATE + h * HEAD_V)).astype(o_ref.dtype)
        return carry

    lax.fori_loop(0, chunks, chunk_body, 0)


def _mixers(proj, batch, seq, lb_logits, hg_norm_g, wgu, bg, gla_norm_g, tt=512):
    n = batch * seq
    steps = seq // tt
    heads = HG_HEADS + GLA_HEADS
    const = lambda shape: pl.BlockSpec(shape, lambda b, t: (0,) * len(shape))
    return pl.pallas_call(
        functools.partial(_mixer_body, chunks=tt // CHUNK),
        grid=(batch, steps),
        in_specs=[pl.BlockSpec((tt, IN_WIDTH_PADDED), lambda b, t: (b * steps + t, 0)),
                  const(_SEG_MATS.shape), const(_LEVEL_MAP.shape),
                  const(lb_logits.shape), const((1, HEAD_V)),
                  const(wgu.shape), const(bg.shape), const((1, HEAD_V))],
        out_specs=pl.BlockSpec((tt, heads * HEAD_V), lambda b, t: (b * steps + t, 0)),
        out_shape=jax.ShapeDtypeStruct((n, heads * HEAD_V), BF16),
        scratch_shapes=[pltpu.VMEM((heads, HEAD_V, LANES), F32)],
        compiler_params=pltpu.CompilerParams(dimension_semantics=("parallel", "arbitrary"),
                                             vmem_limit_bytes=VMEM_LIMIT),
        name="mixers",
    )(proj, jnp.asarray(_SEG_MATS, BF16), jnp.asarray(_LEVEL_MAP), lb_logits, hg_norm_g,
      wgu, bg, gla_norm_g)


def _outproj_body(x_ref, m_ref, wo_ref, g_ref, wq_ref, h_ref, hn_ref, q_ref):
    h = x_ref[...] + jnp.dot(m_ref[...], wo_ref[...], preferred_element_type=F32)
    h_ref[...] = h
    hn = h * lax.rsqrt(jnp.mean(h * h, axis=-1, keepdims=True) + EPS) * g_ref[...]
    hn_ref[...] = hn
    q = jnp.dot(hn.astype(BF16), wq_ref[...], preferred_element_type=F32)
    for j in range(q_ref.shape[0]):
        q_ref[j] = q[:, j * PEER_HALF:(j + 1) * PEER_HALF]


def _outproj(x2, row0, mixed, wo, g2, wq, tm=256):
    n, d = mixed.shape[0], x2.shape[1]
    nq = wq.shape[1] // PEER_HALF
    assert row0 % tm == 0 and n % tm == 0
    return pl.pallas_call(
        _outproj_body,
        grid=(n // tm,),
        in_specs=[pl.BlockSpec((tm, d), lambda i: (i + row0 // tm, 0)),
                  pl.BlockSpec((tm, mixed.shape[1]), lambda i: (i, 0)),
                  pl.BlockSpec(wo.shape, lambda i: (0, 0)),
                  pl.BlockSpec((1, d), lambda i: (0, 0)),
                  pl.BlockSpec(wq.shape, lambda i: (0, 0))],
        out_specs=[pl.BlockSpec((tm, d), lambda i: (i, 0)),
                   pl.BlockSpec((tm, d), lambda i: (i, 0)),
                   pl.BlockSpec((nq, tm, PEER_HALF), lambda i: (0, i, 0))],
        out_shape=[jax.ShapeDtypeStruct((n, d), F32),
                   jax.ShapeDtypeStruct((n, d), F32),
                   jax.ShapeDtypeStruct((nq, n, PEER_HALF), F32)],
        compiler_params=pltpu.CompilerParams(dimension_semantics=("parallel",),
                                             vmem_limit_bytes=VMEM_LIMIT),
        name="outproj",
    )(x2, mixed, wo, g2, wq)


def _candidate_constants():
    k = PEER_TOPK
    pos, valid = [], []
    for j in range(k):
        pos.append(j); valid.append(True)
    for i in range(1, 8):
        for j in range(8):
            pos.append(i * k + j); valid.append((i + 1) * (j + 1) <= k)
    for i in range(8, k):
        pos.append(i * k); valid.append(True)
    return np.asarray(pos, np.float32), np.asarray(valid, bool)


_CAND_POS, _CAND_VALID = _candidate_constants()
_N_CAND = _CAND_POS.shape[0]


def _topk_rows(ss, pos, extras, k):
    t = ss[0].shape[1]
    slot = lax.broadcasted_iota(jnp.int32, (k, t), 0)

    def body(it, carry):
        out = []
        for (s, vals, poss, exts), extra in zip(carry, extras):
            m = jnp.max(s, axis=0, keepdims=True)
            p = jnp.min(jnp.where(s == m, pos, 1e9), axis=0, keepdims=True)
            hit = pos == p
            vals = jnp.where(slot == it, m, vals)
            poss = jnp.where(slot == it, p, poss)
            if extra is not None:
                x = jnp.max(jnp.where(hit, extra, -1.0), axis=0, keepdims=True)
                exts = jnp.where(slot == it, x, exts)
            out.append((jnp.where(hit, -jnp.inf, s), vals, poss, exts))
        return tuple(out)

    z = jnp.zeros((k, t), F32)
    res = lax.fori_loop(0, k, body, tuple((s, z, z, z) for s in ss))
    return [(vals, poss, exts) for _, vals, poss, exts in res]


def _route_body(q_ref, keys_ref, cpos_ref, cvalid_ref, idx_ref, gate_ref):
    k = PEER_TOPK
    n_tiles = q_ref.shape[1] // LANES
    key_pos = lax.broadcasted_iota(jnp.int32, (PEER_N_KEYS, LANES), 0).astype(F32)
    cpos = cpos_ref[...]
    cvalid = cvalid_ref[...] > 0.5
    scores = [_nt(keys_ref[p], q_ref[p, pl.ds(lt * LANES, LANES), :].astype(BF16))
              for lt in range(n_tiles) for p in range(2)]
    tops = _topk_rows(scores, key_pos, [None] * len(scores), k)
    cands, experts = [], []
    for lt in range(n_tiles):
        (v0, i0, _), (v1, i1, _) = tops[2 * lt], tops[2 * lt + 1]
        vals = [v0[0:1] + v1]
        exps = [i0[0:1] * PEER_N_KEYS + i1]
        for i in range(1, 8):
            vals.append(v0[i:i + 1] + v1[0:8])
            exps.append(i0[i:i + 1] * PEER_N_KEYS + i1[0:8])
        vals.append(v0[8:k] + v1[0:1])
        exps.append(i0[8:k] * PEER_N_KEYS + i1[0:1])
        cands.append(jnp.where(cvalid, jnp.concatenate(vals, axis=0), -jnp.inf))
        experts.append(jnp.concatenate(exps, axis=0))
    picks = _topk_rows(cands, cpos, experts, k)
    for lt in range(n_tiles):
        best, _, expert = picks[lt]
        cols = pl.ds(lt * LANES, LANES)
        e = jnp.exp(best - jnp.max(best, axis=0, keepdims=True))
        gate_ref[:, cols] = e / jnp.sum(e, axis=0, keepdims=True)
        idx_ref[:, cols] = expert.astype(jnp.int32)


def _route(q3, keys, tok0, n):
    heads = q3.shape[0] // 2
    tb = 512 if tok0 % 512 == 0 and n % 512 == 0 else SHARE_GRANULE
    assert tok0 % tb == 0 and n % tb == 0
    cpos = jnp.broadcast_to(jnp.asarray(_CAND_POS)[:, None], (_N_CAND, LANES))
    cvalid = jnp.broadcast_to(jnp.asarray(_CAND_VALID, F32)[:, None], (_N_CAND, LANES))
    return pl.pallas_call(
        _route_body,
        grid=(n // tb, heads),
        in_specs=[pl.BlockSpec((2, tb, PEER_HALF), lambda i, h: (h, i + tok0 // tb, 0)),
                  pl.BlockSpec((2, PEER_N_KEYS, PEER_HALF), lambda i, h: (h, 0, 0)),
                  pl.BlockSpec((_N_CAND, LANES), lambda i, h: (0, 0)),
                  pl.BlockSpec((_N_CAND, LANES), lambda i, h: (0, 0))],
        out_specs=[pl.BlockSpec((PEER_TOPK, tb), lambda i, h: (h, i)),
                   pl.BlockSpec((PEER_TOPK, tb), lambda i, h: (h, i))],
        out_shape=[jax.ShapeDtypeStruct((heads * PEER_TOPK, n), jnp.int32),
                   jax.ShapeDtypeStruct((heads * PEER_TOPK, n), F32)],
        compiler_params=pltpu.CompilerParams(dimension_semantics=("parallel", "parallel"),
                                             vmem_limit_bytes=VMEM_LIMIT),
        name="route",
    )(q3, keys, cpos, cvalid)


def _sc_pipeline(idx_v, table_hbm, rows, sems, n_units, units_per_token, compute):
    parts = rows[0].shape[0] // SC_LANES
    assert n_units % 2 == 0

    def start(unit, slot):
        tl = unit // units_per_token
        g = unit % units_per_token
        for part in range(parts):
            iv = idx_v[tl, pl.ds((g * parts + part) * SC_LANES, SC_LANES)]
            pltpu.async_copy(table_hbm.at[iv], rows[slot].at[pl.ds(part * SC_LANES, SC_LANES)], sems[slot])

    def wait(slot):
        iv = idx_v[0, pl.ds(0, SC_LANES)]
        for part in range(parts):
            pltpu.make_async_copy(table_hbm.at[iv], rows[slot].at[pl.ds(part * SC_LANES, SC_LANES)],
                                  sems[slot]).wait()

    start(0, 0)

    def pair(i, carry):
        u0 = 2 * i
        start(u0 + 1, 1)
        wait(0)
        compute(u0 // units_per_token, u0 % units_per_token, rows[0])

        @pl.when(u0 + 2 < n_units)
        def _():
            start(u0 + 2, 0)

        wait(1)
        compute((u0 + 1) // units_per_token, (u0 + 1) % units_per_token, rows[1])
        return carry

    lax.fori_loop(0, n_units // 2, pair, 0)


def _sc_batch(tokens_per_worker):
    for tb in (40, 32, 24, 16, 8):
        if tokens_per_worker % tb == 0:
            return tb
    raise ValueError(f"tokens per SparseCore worker ({tokens_per_worker}) must be a multiple of 8")


def _peer_act(hn, idx, u, unit_rows=2 * SC_LANES):
    n, d = hn.shape
    kk = idx.shape[1]
    tpw = n // SC_WORKERS
    tb = _sc_batch(tpw)
    upt = kk // unit_rows
    parts = unit_rows // SC_LANES
    nchunk = d // SC_LANES
    mesh = plsc.VectorSubcoreMesh(core_axis_name="c", subcore_axis_name="s")

    def body(x_hbm, idx_hbm, u_hbm, out_hbm, x_v, idx_v, act_v, rows0, rows1, tmp_v, sem0, sem1):
        wid = lax.axis_index("s") * 2 + lax.axis_index("c")
        lane_row = lax.iota(jnp.int32, SC_LANES) * SC_LANES

        def compute(tl, g, r_ref):
            for part in range(parts):
                row0 = part * SC_LANES

                def cbody(c, accs, row0=row0):
                    xv = x_v[tl, pl.ds(c * SC_LANES, SC_LANES)]
                    return tuple(accs[r] + r_ref[row0 + r, pl.ds(c * SC_LANES, SC_LANES)] * xv
                                 for r in range(SC_LANES))
                accs = lax.fori_loop(0, nchunk, cbody,
                                     tuple(jnp.zeros((SC_LANES,), F32) for _ in range(SC_LANES)))
                for r in range(SC_LANES):
                    tmp_v[pl.ds(r * SC_LANES, SC_LANES)] = accs[r]
                tot = plsc.load_gather(tmp_v, [lane_row])
                for j in range(1, SC_LANES):
                    tot = tot + plsc.load_gather(tmp_v, [lane_row + j])
                act_v[tl, pl.ds((g * parts + part) * SC_LANES, SC_LANES)] = tot

        def batch(b, carry):
            t0 = wid * tpw + b * tb
            pltpu.sync_copy(x_hbm.at[pl.ds(t0, tb)], x_v)
            pltpu.sync_copy(idx_hbm.at[pl.ds(t0, tb)], idx_v)
            _sc_pipeline(idx_v, u_hbm, (rows0, rows1), (sem0, sem1), tb * upt, upt, compute)
            pltpu.sync_copy(act_v, out_hbm.at[pl.ds(t0, tb)])
            return carry

        lax.fori_loop(0, tpw // tb, batch, 0)

    return pl.kernel(
        body, mesh=mesh,
        compiler_params=pltpu.CompilerParams(needs_layout_passes=False),
        out_type=jax.ShapeDtypeStruct((n, kk), F32),
        scratch_types=[pltpu.VMEM((tb, d), F32), pltpu.VMEM((tb, kk), jnp.int32),
                       pltpu.VMEM((tb, kk), F32),
                       pltpu.VMEM((unit_rows, d), F32), pltpu.VMEM((unit_rows, d), F32),
                       pltpu.VMEM((SC_LANES * SC_LANES,), F32),
                       pltpu.SemaphoreType.DMA, pltpu.SemaphoreType.DMA],
        name="peer_act",
    )(hn, idx, u)


def _peer_out(w, idx, v, cb=16, unit_rows=2 * SC_LANES):
    n, kk = w.shape
    d = v.shape[1]
    tpw = n // SC_WORKERS
    tb = _sc_batch(tpw)
    upt = kk // unit_rows
    nchunk = d // SC_LANES
    mesh = plsc.VectorSubcoreMesh(core_axis_name="c", subcore_axis_name="s")

    def body(w_hbm, idx_hbm, v_hbm, out_hbm, w_v, idx_v, out_v, rows0, rows1, sem0, sem1):
        wid = lax.axis_index("s") * 2 + lax.axis_index("c")
        zero = jnp.zeros((SC_LANES,), F32)

        def compute(tl, g, r_ref):
            tls = jnp.full((SC_LANES,), tl, jnp.int32)

            def cb_body(ci, carry):
                c0 = ci * (cb * SC_LANES)
                accs = tuple(out_v[tl, pl.ds(c0 + j * SC_LANES, SC_LANES)] for j in range(cb))

                def rbody(r, accs):
                    wr = plsc.load_gather(w_v, [tls, jnp.full((SC_LANES,), g * unit_rows + r, jnp.int32)])
                    return tuple(accs[j] + wr * r_ref[r, pl.ds(c0 + j * SC_LANES, SC_LANES)]
                                 for j in range(cb))

                accs = lax.fori_loop(0, unit_rows, rbody, accs)
                for j in range(cb):
                    out_v[tl, pl.ds(c0 + j * SC_LANES, SC_LANES)] = accs[j]
                return carry

            lax.fori_loop(0, nchunk // cb, cb_body, 0)

        def batch(b, carry):
            t0 = wid * tpw + b * tb
            pltpu.sync_copy(w_hbm.at[pl.ds(t0, tb)], w_v)
            pltpu.sync_copy(idx_hbm.at[pl.ds(t0, tb)], idx_v)

            def zbody(i, carry):
                out_v[i // nchunk, pl.ds((i % nchunk) * SC_LANES, SC_LANES)] = zero
                return carry

            lax.fori_loop(0, tb * nchunk, zbody, 0)
            _sc_pipeline(idx_v, v_hbm, (rows0, rows1), (sem0, sem1), tb * upt, upt, compute)
            pltpu.sync_copy(out_v, out_hbm.at[pl.ds(t0, tb)])
            return carry

        lax.fori_loop(0, tpw // tb, batch, 0)

    return pl.kernel(
        body, mesh=mesh,
        compiler_params=pltpu.CompilerParams(needs_layout_passes=False),
        out_type=jax.ShapeDtypeStruct((n, d), F32),
        scratch_types=[pltpu.VMEM((tb, kk), F32), pltpu.VMEM((tb, kk), jnp.int32),
                       pltpu.VMEM((tb, d), F32),
                       pltpu.VMEM((unit_rows, d), F32), pltpu.VMEM((unit_rows, d), F32),
                       pltpu.SemaphoreType.DMA, pltpu.SemaphoreType.DMA],
        name="peer_out",
    )(w, idx, v)


SUBLANES = 8
TC_PEER_TOKENS = 8
ISSUE_UNROLL = 8


def _gelu(a):
    return a * (lax.erf(a / np.sqrt(2.0).astype(np.float32)) + 1.0) / 2.0


def _peer_tc_body(idx_ref, idx_next_ref, x_ref, g_ref, u_hbm, v_hbm, o_ref, ubuf, vbuf, sem_u, sem_v):
    tokens = x_ref.shape[0]
    kk = g_ref.shape[1]
    n_rows = tokens * kk
    step = pl.program_id(0)
    slot = step % 2

    def row_copy(table, buf, sem, expert, j):
        src = table.at[pl.ds(pl.multiple_of(expert * SUBLANES, SUBLANES), SUBLANES)]
        return pltpu.make_async_copy(src, buf.at[pl.ds(j * SUBLANES, SUBLANES)], sem)

    def issue_block(ids_ref, into):
        def issue(i, carry):
            for r in range(ISSUE_UNROLL):
                j = i * ISSUE_UNROLL + r
                expert = ids_ref[j]
                row_copy(u_hbm, ubuf.at[into], sem_u.at[into], expert, j).start()
                row_copy(v_hbm, vbuf.at[into], sem_v.at[into], expert, j).start()
            return carry

        lax.fori_loop(0, n_rows // ISSUE_UNROLL, issue, 0)

    @pl.when(step == 0)
    def _():
        issue_block(idx_ref, 0)

    @pl.when(step + 1 < pl.num_programs(0))
    def _():
        issue_block(idx_next_ref, 1 - slot)

    def wait_all(table, buf, sem):
        pltpu.make_async_copy(table.at[pl.ds(0, n_rows * SUBLANES)], buf.at[slot], sem.at[slot]).wait()

    def rows_of(buf, t, s):
        return buf[slot, pl.ds(t * kk * SUBLANES + s, kk, stride=SUBLANES), :]

    wait_all(u_hbm, ubuf, sem_u)
    acts = []
    for t in range(tokens):
        acc = rows_of(ubuf, t, 0) * x_ref[t:t + 1, 0:LANES]
        for s in range(1, SUBLANES):
            acc = acc + rows_of(ubuf, t, s) * x_ref[t:t + 1, s * LANES:(s + 1) * LANES]
        acts.append(jnp.sum(acc, axis=1, keepdims=True))
    w = g_ref[0] * _gelu(jnp.concatenate(acts, axis=1))

    wait_all(v_hbm, vbuf, sem_v)
    for t in range(tokens):
        wt = w[:, t:t + 1]
        for s in range(SUBLANES):
            o_ref[t:t + 1, s * LANES:(s + 1) * LANES] = jnp.sum(rows_of(vbuf, t, s) * wt, axis=0, keepdims=True)


def _peer_tc(hn, idx, gate_t, u_tiles, v_tiles):
    m, d = hn.shape
    kk = idx.shape[1]
    tb = TC_PEER_TOKENS
    assert d == SUBLANES * LANES and m % tb == 0
    steps = m // tb
    g3 = gate_t.reshape(kk, steps, tb).transpose(1, 0, 2)
    rows = tb * kk * SUBLANES
    ids = idx.reshape(m * kk)
    return pl.pallas_call(
        _peer_tc_body,
        grid=(steps,),
        in_specs=[pl.BlockSpec((tb * kk,), lambda i: (i,), memory_space=pltpu.SMEM),
                  pl.BlockSpec((tb * kk,), lambda i: (jnp.minimum(i + 1, steps - 1),), memory_space=pltpu.SMEM),
                  pl.BlockSpec((tb, d), lambda i: (i, 0)),
                  pl.BlockSpec((1, kk, tb), lambda i: (i, 0, 0)),
                  pl.BlockSpec(memory_space=pl.ANY),
                  pl.BlockSpec(memory_space=pl.ANY)],
        out_specs=pl.BlockSpec((tb, d), lambda i: (i, 0)),
        out_shape=jax.ShapeDtypeStruct((m, d), F32),
        scratch_shapes=[pltpu.VMEM((2, rows, LANES), F32), pltpu.VMEM((2, rows, LANES), F32),
                        pltpu.SemaphoreType.DMA((2,)), pltpu.SemaphoreType.DMA((2,))],
        compiler_params=pltpu.CompilerParams(dimension_semantics=("arbitrary",),
                                             vmem_limit_bytes=VMEM_LIMIT),
        name="peer_tc",
    )(ids, ids, hn, g3, u_tiles, v_tiles)


def _gelu_gate_body(a_ref, g_ref, o_ref):
    o_ref[...] = g_ref[...] * _gelu(a_ref[...])


def _gelu_gate(act, gate, tm=SHARE_GRANULE):
    n, kk = act.shape
    tm = min(tm, n)
    assert n % tm == 0
    spec = pl.BlockSpec((tm, kk), lambda i: (i, 0))
    return pl.pallas_call(
        _gelu_gate_body, grid=(n // tm,), in_specs=[spec, spec], out_specs=spec,
        out_shape=jax.ShapeDtypeStruct((n, kk), F32),
        compiler_params=pltpu.CompilerParams(dimension_semantics=("parallel",)),
        name="gelu_gate",
    )(act, gate)


def _final_body(h_ref, g_ref, *rest, starts, aliased):
    o_ref = rest[-1]
    parts = rest[:len(starts)]
    assert len(rest) == len(starts) + 1 + int(aliased)
    peer = parts[0][...]
    for start, p_ref in zip(starts[1:], parts[1:]):
        peer = jnp.where(pl.program_id(0) >= start, p_ref[...], peer)
    h = h_ref[...] + peer
    o_ref[...] = h * lax.rsqrt(jnp.mean(h * h, axis=-1, keepdims=True) + EPS) * g_ref[...]


def _final(h1, peer_parts, g, out, row0, n_total, tm=256):
    n, d = h1.shape
    sizes = [p.shape[0] for p in peer_parts]
    assert sum(sizes) == n and all(m > 0 and m % tm == 0 for m in sizes) and row0 % tm == 0
    starts = tuple(int(v) // tm for v in np.cumsum([0] + sizes[:-1]))

    def part_spec(start, size):
        return pl.BlockSpec((tm, d), lambda i: (jnp.clip(i - start, 0, size // tm - 1), 0))

    in_specs = [pl.BlockSpec((tm, d), lambda i: (i, 0)), pl.BlockSpec((1, d), lambda i: (0, 0))]
    in_specs += [part_spec(s, m) for s, m in zip(starts, sizes)]
    args = [h1, g, *peer_parts]
    aliases = {}
    if out is not None:
        in_specs.append(pl.BlockSpec(memory_space=pl.ANY))
        aliases = {len(args): 0}
        args.append(out)
    return pl.pallas_call(
        functools.partial(_final_body, starts=starts, aliased=out is not None), grid=(n // tm,),
        in_specs=in_specs,
        out_specs=pl.BlockSpec((tm, d), lambda i: (i + row0 // tm, 0)),
        out_shape=jax.ShapeDtypeStruct((n_total, d), F32),
        input_output_aliases=aliases,
        compiler_params=pltpu.CompilerParams(dimension_semantics=("parallel",)),
        name="final_norm",
    )(*args)


def _tc_share(n_tokens):
    m = int(n_tokens * TC_SHARE) // SHARE_GRANULE * SHARE_GRANULE
    return min(max(m, SHARE_GRANULE), n_tokens - SHARE_GRANULE)


def kernel(x, norm1_g, w_in, hg_lower_logits, hg_norm_g, gla_w_gate_up, gla_b_gate, gla_norm_g,
           w_out, norm2_g, peer_w_q, peer_sub_keys, peer_u, peer_v, norm_f_g):
    batch, seq, d = x.shape
    depth = w_in.shape[0]
    assert depth == 1, "single-layer block"
    layer = 0

    w = w_in[layer]
    pad = LANES - GLA_GATE_RANK
    assert w.shape == (d, IN_WIDTH_PADDED - pad) and d == SUBLANES * LANES
    w_r = jnp.concatenate(
        [w[:, :OFF_GLOW + GLA_GATE_RANK], jnp.zeros((d, pad), w.dtype),
         w[:, OFF_GLOW + GLA_GATE_RANK:]], axis=1).astype(BF16)
    wgu = jnp.concatenate([gla_w_gate_up[layer], jnp.zeros((pad, GLA_KEY), F32)], axis=0).astype(BF16)

    wo = w_out[layer].astype(BF16)
    wq = peer_w_q[layer].astype(BF16)
    keys = peer_sub_keys[layer].reshape(PEER_HEADS * 2, PEER_N_KEYS, PEER_HALF).astype(BF16)
    n_experts = peer_u.shape[1]
    u_tiles = peer_u[layer].reshape(n_experts * SUBLANES, LANES)
    v_tiles = peer_v[layer].reshape(n_experts * SUBLANES, LANES)

    groups = PIPELINE_GROUPS if batch % PIPELINE_GROUPS == 0 else 1
    bpg = batch // groups
    ng = bpg * seq
    m_tc = _tc_share(ng)
    gf = norm_f_g.reshape(1, d)
    x2 = x.reshape(batch * seq, d)

    def after(value, anchor):
        value, _ = lax.optimization_barrier((value, anchor))
        return value

    half = ng // 2
    h1s, tc_ins, sc_units = [], [], []
    for gi in range(groups):
        proj = _inproj(x2, gi * ng, ng, norm1_g[layer].reshape(1, d), w_r)
        mixed = _mixers(proj, bpg, seq, hg_lower_logits, hg_norm_g[layer].reshape(1, HEAD_V), wgu,
                        gla_b_gate[layer].reshape(1, GLA_KEY), gla_norm_g[layer].reshape(1, HEAD_V))
        h1, hn, q3 = _outproj(x2, gi * ng, mixed, wo, norm2_g[layer].reshape(1, d), wq)
        h1s.append(h1)
        m_g = m_tc - SHARE_GRANULE if 0 < gi == groups - 1 and m_tc > SHARE_GRANULE else m_tc
        bounds = (m_g, half, ng) if gi == 0 and m_g < half else (m_g, ng)
        units = []
        for lo, hi in zip(bounds[:-1], bounds[1:]):
            idx_t, gate_t = _route(q3, keys, lo, hi - lo)
            idx = idx_t.T
            units.append(dict(idx=idx, gate=gate_t.T, act=_peer_act(hn[lo:hi], idx, peer_u[layer])))
        sc_units.append(units)
        idx_t, gate_t = _route(q3, keys, 0, m_g)
        tc_ins.append((hn[:m_g], idx_t.T, gate_t))

    def sc_out(gi, anchor):
        outs = []
        for unit in sc_units[gi]:
            wts = _gelu_gate(after(unit["act"], anchor), unit["gate"])
            outs.append(_peer_out(wts, unit["idx"], peer_v[layer]))
        return wts, outs

    rows, peer_sc = [None] * groups, [None] * groups
    rows[0] = _peer_tc(*after(tc_ins[0], sc_units[-1][-1]["idx"]), u_tiles, v_tiles)
    anchor = rows[0]
    for gi in range(groups - 1):
        anchor, peer_sc[gi] = sc_out(gi, rows[0])
    if groups > 1:
        rows[1] = _peer_tc(*after(tc_ins[1], anchor), u_tiles, v_tiles)
        anchor = rows[1]
    anchor, peer_sc[groups - 1] = sc_out(groups - 1, anchor)
    out, finished = None, 0

    def finish(gi, out):
        return _final(h1s[gi], [rows[gi], *peer_sc[gi]], gf, out, gi * ng, batch * seq)

    for gi in range(2, groups):
        out = finish(gi - 2, out)
        finished = gi - 1
        rows[gi] = _peer_tc(*after(tc_ins[gi], (anchor, out)), u_tiles, v_tiles)
        anchor = rows[gi]
    for gi in range(finished, groups):
        out = finish(gi, out)
    return out.reshape(batch, seq, d)
```

```python
import functools

import jax
import jax.numpy as jnp
import numpy as np
from jax import lax
from jax.experimental import pallas as pl
from jax.experimental.pallas import tpu as pltpu
from jax.experimental.pallas import tpu_sc as plsc

F32 = jnp.float32
BF16 = jnp.bfloat16
EPS = 1e-6

HG_HEADS = 4
GLA_HEADS = 4
HEAD_V = 128
HG_KEY = 512
GLA_KEY = 256
GLA_HEAD_K = 64
GLA_GATE_RANK = 16
GLA_GATE_NORMALIZER = 16.0
CHUNK = 64
LANES = 128
PEER_HEADS = 8
PEER_N_KEYS = 128
PEER_HALF = 128
PEER_TOPK = 16

OFF_HQ, OFF_HF, OFF_HI, OFF_HGATE = 0, 512, 1024, 1536
OFF_GQ, OFF_GK, OFF_GV, OFF_GLOW, OFF_GGATE = 2048, 2304, 2560, 3072, 3200
IN_WIDTH_PADDED = 3712

VMEM_LIMIT = 48 * 1024 * 1024

SC_LANES = 16
SC_WORKERS = 32
PIPELINE_GROUPS = 4
TC_SHARE = 0.25
SHARE_GRANULE = 256


def _nt(a, b):
    return lax.dot_general(a, b, (((1,), (1,)), ((), ())), preferred_element_type=F32)


def _tn(a, b):
    return lax.dot_general(a, b, (((0,), (0,)), ((), ())), preferred_element_type=F32)


def _inproj_body(x_ref, g_ref, w_ref, o_ref):
    x = x_ref[...]
    xn = x * lax.rsqrt(jnp.mean(x * x, axis=-1, keepdims=True) + EPS) * g_ref[...]
    o_ref[...] = jnp.dot(xn.astype(BF16), w_ref[...], preferred_element_type=F32)


def _inproj(x2, row0, n, g, w, tm=256):
    d = x2.shape[1]
    wd = w.shape[1]
    assert row0 % tm == 0 and n % tm == 0
    return pl.pallas_call(
        _inproj_body,
        grid=(n // tm,),
        in_specs=[pl.BlockSpec((tm, d), lambda i: (i + row0 // tm, 0)),
                  pl.BlockSpec((1, d), lambda i: (0, 0)),
                  pl.BlockSpec((d, wd), lambda i: (0, 0))],
        out_specs=pl.BlockSpec((tm, wd), lambda i: (i, 0)),
        out_shape=jax.ShapeDtypeStruct((n, wd), F32),
        compiler_params=pltpu.CompilerParams(dimension_semantics=("parallel",),
                                             vmem_limit_bytes=VMEM_LIMIT),
        name="inproj",
    )(x2, g, w)


def _level_constants():
    c = CHUNK
    mats = [np.tril(np.ones((c, c), np.float32))]
    level = np.full((c, c), -1, np.int32)
    b, lvl = c // 2, 0
    while b >= 1:
        m = np.zeros((c, c), np.float32)
        for s in range(0, c, 2 * b):
            mid = s + b
            for i in range(mid, s + 2 * b):
                m[i, mid:i + 1] = 1.0
                level[i, s:mid] = lvl
            for j in range(s, mid):
                m[j, j + 1:mid] = 1.0
        mats.append(m)
        b //= 2
        lvl += 1
    level[np.arange(c), np.arange(c)] = lvl
    return np.concatenate(mats, axis=0), level, lvl


_SEG_MATS, _LEVEL_MAP, _N_LEVELS = _level_constants()


def _split3(a):
    hi = a.astype(BF16)
    r = a - hi.astype(F32)
    mid = r.astype(BF16)
    lo = (r - mid.astype(F32)).astype(BF16)
    return jnp.concatenate([hi, mid, lo], axis=1)


def _gla_chunk(q, k, g, v, st, seg, level):
    c = CHUNK
    ex3 = jnp.dot(seg, _split3(g), preferred_element_type=F32)
    ex = ex3[:, 0:LANES] + ex3[:, LANES:2 * LANES] + ex3[:, 2 * LANES:3 * LANES]
    cum = ex[0:c]
    scores = jnp.where(level == _N_LEVELS, _nt(q.astype(BF16), k.astype(BF16)), 0.0)
    for l in range(_N_LEVELS):
        e = jnp.exp(ex[c * (l + 1):c * (l + 2)])
        p = _nt((q * e).astype(BF16), (k * e).astype(BF16))
        scores = jnp.where(level == l, p, scores)
    last = cum[c - 1:c, :]
    qd = (q * jnp.exp(cum)).astype(BF16)
    kd = (k * jnp.exp(last - cum)).astype(BF16)
    vb = v.astype(BF16)
    o = jnp.dot(scores.astype(BF16), vb, preferred_element_type=F32) + _nt(qd, st.astype(BF16))
    st_new = st * jnp.exp(last) + _tn(vb, kd)
    return o, st_new


def _head_out(o, gain, gate):
    on = o * lax.rsqrt(jnp.mean(o * o, axis=-1, keepdims=True) + EPS) * gain
    return on * (gate * jax.nn.sigmoid(gate))


def _mixer_body(p_ref, seg_ref, level_ref, lbl_ref, hgn_ref, wgu_ref, bg_ref, ggn_ref,
                o_ref, st_ref, *, chunks):
    @pl.when(pl.program_id(1) == 0)
    def _():
        st_ref[...] = jnp.zeros_like(st_ref)

    seg = seg_ref[...]
    level = level_ref[...]
    logits = lbl_ref[...]
    ez = jnp.exp(logits - jnp.max(logits, axis=0, keepdims=True))
    lb = ez[0:1, :] / jnp.sum(ez, axis=0, keepdims=True)
    lane = lax.broadcasted_iota(jnp.int32, (CHUNK, LANES), 1)

    def chunk_body(ci, carry):
        rows = pl.ds(pl.multiple_of(ci * CHUNK, CHUNK), CHUNK)

        def col(off):
            return p_ref[rows, pl.ds(off, LANES)]

        for h in range(HG_HEADS):
            hq = col(OFF_HQ + h * LANES)
            q = hq * jax.nn.sigmoid(hq)
            lbh = lb[:, h * LANES:(h + 1) * LANES]
            forget = lbh + (1.0 - lbh) * jax.nn.sigmoid(col(OFF_HF + h * LANES))
            o, st = _gla_chunk(q, 1.0 - forget, jnp.log(forget), col(OFF_HI + h * LANES),
                               st_ref[h], seg, level)
            st_ref[h] = st
            o_ref[rows, pl.ds(h * HEAD_V, HEAD_V)] = _head_out(
                o, hgn_ref[...], col(OFF_HGATE + h * HEAD_V)).astype(o_ref.dtype)

        zg = jnp.dot(col(OFF_GLOW).astype(BF16), wgu_ref[...], preferred_element_type=F32) + bg_ref[...]
        log_g = (jnp.minimum(zg, 0.0) - jnp.log(1.0 + jnp.exp(-jnp.abs(zg)))) / GLA_GATE_NORMALIZER
        for h in range(GLA_HEADS):
            pair, half = h // 2, h % 2
            q = col(OFF_GQ + pair * LANES) * (GLA_HEAD_K ** -0.5)
            in_head = (lane >= half * GLA_HEAD_K) & (lane < (half + 1) * GLA_HEAD_K)
            k = jnp.where(in_head, col(OFF_GK + pair * LANES), 0.0)
            g = log_g[:, pair * LANES:(pair + 1) * LANES]
            o, st = _gla_chunk(q, k, g, col(OFF_GV + h * HEAD_V), st_ref[HG_HEADS + h], seg, level)
            st_ref[HG_HEADS + h] = st
            o_ref[rows, pl.ds((HG_HEADS + h) * HEAD_V, HEAD_V)] = _head_out(
                o, ggn_ref[...], col(OFF_GGATE + h * HEAD_V)).astype(o_ref.dtype)
        return carry

    lax.fori_loop(0, chunks, chunk_body, 0)


def _mixers(proj, batch, seq, lb_logits, hg_norm_g, wgu, bg, gla_norm_g, tt=512):
    n = batch * seq
    steps = seq // tt
    heads = HG_HEADS + GLA_HEADS
    const = lambda shape: pl.BlockSpec(shape, lambda b, t: (0,) * len(shape))
    return pl.pallas_call(
        functools.partial(_mixer_body, chunks=tt // CHUNK),
        grid=(batch, steps),
        in_specs=[pl.BlockSpec((tt, IN_WIDTH_PADDED), lambda b, t: (b * steps + t, 0)),
                  const(_SEG_MATS.shape), const(_LEVEL_MAP.shape),
                  const(lb_logits.shape), const((1, HEAD_V)),
                  const(wgu.shape), const(bg.shape), const((1, HEAD_V))],
        out_specs=pl.BlockSpec((tt, heads * HEAD_V), lambda b, t: (b * steps + t, 0)),
        out_shape=jax.ShapeDtypeStruct((n, heads * HEAD_V), BF16),
        scratch_shapes=[pltpu.VMEM((heads, HEAD_V, LANES), F32)],
        compiler_params=pltpu.CompilerParams(dimension_semantics=("parallel", "arbitrary"),
                                             vmem_limit_bytes=VMEM_LIMIT),
        name="mixers",
    )(proj, jnp.asarray(_SEG_MATS, BF16), jnp.asarray(_LEVEL_MAP), lb_logits, hg_norm_g,
      wgu, bg, gla_norm_g)


def _outproj_body(x_ref, m_ref, wo_ref, g_ref, wq_ref, h_ref, hn_ref, q_ref):
    h = x_ref[...] + jnp.dot(m_ref[...], wo_ref[...], preferred_element_type=F32)
    h_ref[...] = h
    hn = h * lax.rsqrt(jnp.mean(h * h, axis=-1, keepdims=True) + EPS) * g_ref[...]
    hn_ref[...] = hn
    q = jnp.dot(hn.astype(BF16), wq_ref[...], preferred_element_type=F32)
    for j in range(q_ref.shape[0]):
        q_ref[j] = q[:, j * PEER_HALF:(j + 1) * PEER_HALF]


def _outproj(x2, row0, mixed, wo, g2, wq, tm=256):
    n, d = mixed.shape[0], x2.shape[1]
    nq = wq.shape[1] // PEER_HALF
    assert row0 % tm == 0 and n % tm == 0
    return pl.pallas_call(
        _outproj_body,
        grid=(n // tm,),
        in_specs=[pl.BlockSpec((tm, d), lambda i: (i + row0 // tm, 0)),
                  pl.BlockSpec((tm, mixed.shape[1]), lambda i: (i, 0)),
                  pl.BlockSpec(wo.shape, lambda i: (0, 0)),
                  pl.BlockSpec((1, d), lambda i: (0, 0)),
                  pl.BlockSpec(wq.shape, lambda i: (0, 0))],
        out_specs=[pl.BlockSpec((tm, d), lambda i: (i, 0)),
                   pl.BlockSpec((tm, d), lambda i: (i, 0)),
                   pl.BlockSpec((nq, tm, PEER_HALF), lambda i: (0, i, 0))],
        out_shape=[jax.ShapeDtypeStruct((n, d), F32),
                   jax.ShapeDtypeStruct((n, d), F32),
                   jax.ShapeDtypeStruct((nq, n, PEER_HALF), F32)],
        compiler_params=pltpu.CompilerParams(dimension_semantics=("parallel",),
                                             vmem_limit_bytes=VMEM_LIMIT),
        name="outproj",
    )(x2, mixed, wo, g2, wq)


def _candidate_constants():
    k = PEER_TOPK
    pos, valid = [], []
    for j in range(k):
        pos.append(j); valid.append(True)
    for i in range(1, 8):
        for j in range(8):
            pos.append(i * k + j); valid.append((i + 1) * (j + 1) <= k)
    for i in range(8, k):
        pos.append(i * k); valid.append(True)
    return np.asarray(pos, np.float32), np.asarray(valid, bool)


_CAND_POS, _CAND_VALID = _candidate_constants()
_N_CAND = _CAND_POS.shape[0]


def _topk_rows(ss, pos, extras, k):
    t = ss[0].shape[1]
    slot = lax.broadcasted_iota(jnp.int32, (k, t), 0)

    def body(it, carry):
        out = []
        for (s, vals, poss, exts), extra in zip(carry, extras):
            m = jnp.max(s, axis=0, keepdims=True)
            p = jnp.min(jnp.where(s == m, pos, 1e9), axis=0, keepdims=True)
            hit = pos == p
            vals = jnp.where(slot == it, m, vals)
            poss = jnp.where(slot == it, p, poss)
            if extra is not None:
                x = jnp.max(jnp.where(hit, extra, -1.0), axis=0, keepdims=True)
                exts = jnp.where(slot == it, x, exts)
            out.append((jnp.where(hit, -jnp.inf, s), vals, poss, exts))
        return tuple(out)

    z = jnp.zeros((k, t), F32)
    res = lax.fori_loop(0, k, body, tuple((s, z, z, z) for s in ss))
    return [(vals, poss, exts) for _, vals, poss, exts in res]


def _route_body(q_ref, keys_ref, cpos_ref, cvalid_ref, idx_ref, gate_ref):
    k = PEER_TOPK
    n_tiles = q_ref.shape[1] // LANES
    key_pos = lax.broadcasted_iota(jnp.int32, (PEER_N_KEYS, LANES), 0).astype(F32)
    cpos = cpos_ref[...]
    cvalid = cvalid_ref[...] > 0.5
    scores = [_nt(keys_ref[p], q_ref[p, pl.ds(lt * LANES, LANES), :].astype(BF16))
              for lt in range(n_tiles) for p in range(2)]
    tops = _topk_rows(scores, key_pos, [None] * len(scores), k)
    cands, experts = [], []
    for lt in range(n_tiles):
        (v0, i0, _), (v1, i1, _) = tops[2 * lt], tops[2 * lt + 1]
        vals = [v0[0:1] + v1]
        exps = [i0[0:1] * PEER_N_KEYS + i1]
        for i in range(1, 8):
            vals.append(v0[i:i + 1] + v1[0:8])
            exps.append(i0[i:i + 1] * PEER_N_KEYS + i1[0:8])
        vals.append(v0[8:k] + v1[0:1])
        exps.append(i0[8:k] * PEER_N_KEYS + i1[0:1])
        cands.append(jnp.where(cvalid, jnp.concatenate(vals, axis=0), -jnp.inf))
        experts.append(jnp.concatenate(exps, axis=0))
    picks = _topk_rows(cands, cpos, experts, k)
    for lt in range(n_tiles):
        best, _, expert = picks[lt]
        cols = pl.ds(lt * LANES, LANES)
        e = jnp.exp(best - jnp.max(best, axis=0, keepdims=True))
        gate_ref[:, cols] = e / jnp.sum(e, axis=0, keepdims=True)
        idx_ref[:, cols] = expert.astype(jnp.int32)


def _route(q3, keys, tok0, n):
    heads = q3.shape[0] // 2
    tb = 512 if tok0 % 512 == 0 and n % 512 == 0 else SHARE_GRANULE
    assert tok0 % tb == 0 and n % tb == 0
    cpos = jnp.broadcast_to(jnp.asarray(_CAND_POS)[:, None], (_N_CAND, LANES))
    cvalid = jnp.broadcast_to(jnp.asarray(_CAND_VALID, F32)[:, None], (_N_CAND, LANES))
    return pl.pallas_call(
        _route_body,
        grid=(n // tb, heads),
        in_specs=[pl.BlockSpec((2, tb, PEER_HALF), lambda i, h: (h, i + tok0 // tb, 0)),
                  pl.BlockSpec((2, PEER_N_KEYS, PEER_HALF), lambda i, h: (h, 0, 0)),
                  pl.BlockSpec((_N_CAND, LANES), lambda i, h: (0, 0)),
                  pl.BlockSpec((_N_CAND, LANES), lambda i, h: (0, 0))],
        out_specs=[pl.BlockSpec((PEER_TOPK, tb), lambda i, h: (h, i)),
                   pl.BlockSpec((PEER_TOPK, tb), lambda i, h: (h, i))],
        out_shape=[jax.ShapeDtypeStruct((heads * PEER_TOPK, n), jnp.int32),
                   jax.ShapeDtypeStruct((heads * PEER_TOPK, n), F32)],
        compiler_params=pltpu.CompilerParams(dimension_semantics=("parallel", "parallel"),
                                             vmem_limit_bytes=VMEM_LIMIT),
        name="route",
    )(q3, keys, cpos, cvalid)


def _sc_pipeline(idx_v, table_hbm, rows, sems, n_units, units_per_token, compute):
    parts = rows[0].shape[0] // SC_LANES
    assert n_units % 2 == 0

    def start(unit, slot):
        tl = unit // units_per_token
        g = unit % units_per_token
        for part in range(parts):
            iv = idx_v[tl, pl.ds((g * parts + part) * SC_LANES, SC_LANES)]
            pltpu.async_copy(table_hbm.at[iv], rows[slot].at[pl.ds(part * SC_LANES, SC_LANES)], sems[slot])

    def wait(slot):
        iv = idx_v[0, pl.ds(0, SC_LANES)]
        for part in range(parts):
            pltpu.make_async_copy(table_hbm.at[iv], rows[slot].at[pl.ds(part * SC_LANES, SC_LANES)],
                                  sems[slot]).wait()

    start(0, 0)

    def pair(i, carry):
        u0 = 2 * i
        start(u0 + 1, 1)
        wait(0)
        compute(u0 // units_per_token, u0 % units_per_token, rows[0])

        @pl.when(u0 + 2 < n_units)
        def _():
            start(u0 + 2, 0)

        wait(1)
        compute((u0 + 1) // units_per_token, (u0 + 1) % units_per_token, rows[1])
        return carry

    lax.fori_loop(0, n_units // 2, pair, 0)


def _sc_batch(tokens_per_worker):
    for tb in (40, 32, 24, 16, 8):
        if tokens_per_worker % tb == 0:
            return tb
    raise ValueError(f"tokens per SparseCore worker ({tokens_per_worker}) must be a multiple of 8")


def _peer_act(hn, idx, u, unit_rows=2 * SC_LANES):
    n, d = hn.shape
    kk = idx.shape[1]
    tpw = n // SC_WORKERS
    tb = _sc_batch(tpw)
    upt = kk // unit_rows
    parts = unit_rows // SC_LANES
    nchunk = d // SC_LANES
    mesh = plsc.VectorSubcoreMesh(core_axis_name="c", subcore_axis_name="s")

    def body(x_hbm, idx_hbm, u_hbm, out_hbm, x_v, idx_v, act_v, rows0, rows1, tmp_v, sem0, sem1):
        wid = lax.axis_index("s") * 2 + lax.axis_index("c")
        lane_row = lax.iota(jnp.int32, SC_LANES) * SC_LANES

        def compute(tl, g, r_ref):
            for part in range(parts):
                row0 = part * SC_LANES

                def cbody(c, accs, row0=row0):
                    xv = x_v[tl, pl.ds(c * SC_LANES, SC_LANES)]
                    return tuple(accs[r] + r_ref[row0 + r, pl.ds(c * SC_LANES, SC_LANES)] * xv
                                 for r in range(SC_LANES))
                accs = lax.fori_loop(0, nchunk, cbody,
                                     tuple(jnp.zeros((SC_LANES,), F32) for _ in range(SC_LANES)))
                for r in range(SC_LANES):
                    tmp_v[pl.ds(r * SC_LANES, SC_LANES)] = accs[r]
                tot = plsc.load_gather(tmp_v, [lane_row])
                for j in range(1, SC_LANES):
                    tot = tot + plsc.load_gather(tmp_v, [lane_row + j])
                act_v[tl, pl.ds((g * parts + part) * SC_LANES, SC_LANES)] = tot

        def batch(b, carry):
            t0 = wid * tpw + b * tb
            pltpu.sync_copy(x_hbm.at[pl.ds(t0, tb)], x_v)
            pltpu.sync_copy(idx_hbm.at[pl.ds(t0, tb)], idx_v)
            _sc_pipeline(idx_v, u_hbm, (rows0, rows1), (sem0, sem1), tb * upt, upt, compute)
            pltpu.sync_copy(act_v, out_hbm.at[pl.ds(t0, tb)])
            return carry

        lax.fori_loop(0, tpw // tb, batch, 0)

    return pl.kernel(
        body, mesh=mesh,
        compiler_params=pltpu.CompilerParams(needs_layout_passes=False),
        out_type=jax.ShapeDtypeStruct((n, kk), F32),
        scratch_types=[pltpu.VMEM((tb, d), F32), pltpu.VMEM((tb, kk), jnp.int32),
                       pltpu.VMEM((tb, kk), F32),
                       pltpu.VMEM((unit_rows, d), F32), pltpu.VMEM((unit_rows, d), F32),
                       pltpu.VMEM((SC_LANES * SC_LANES,), F32),
                       pltpu.SemaphoreType.DMA, pltpu.SemaphoreType.DMA],
        name="peer_act",
    )(hn, idx, u)


def _peer_out(w, idx, v, cb=16, unit_rows=2 * SC_LANES):
    n, kk = w.shape
    d = v.shape[1]
    tpw = n // SC_WORKERS
    tb = _sc_batch(tpw)
    upt = kk // unit_rows
    nchunk = d // SC_LANES
    mesh = plsc.VectorSubcoreMesh(core_axis_name="c", subcore_axis_name="s")

    def body(w_hbm, idx_hbm, v_hbm, out_hbm, w_v, idx_v, out_v, rows0, rows1, sem0, sem1):
        wid = lax.axis_index("s") * 2 + lax.axis_index("c")
        zero = jnp.zeros((SC_LANES,), F32)

        def compute(tl, g, r_ref):
            tls = jnp.full((SC_LANES,), tl, jnp.int32)

            def cb_body(ci, carry):
                c0 = ci * (cb * SC_LANES)
                accs = tuple(out_v[tl, pl.ds(c0 + j * SC_LANES, SC_LANES)] for j in range(cb))

                def rbody(r, accs):
                    wr = plsc.load_gather(w_v, [tls, jnp.full((SC_LANES,), g * unit_rows + r, jnp.int32)])
                    return tuple(accs[j] + wr * r_ref[r, pl.ds(c0 + j * SC_LANES, SC_LANES)]
                                 for j in range(cb))

                accs = lax.fori_loop(0, unit_rows, rbody, accs)
                for j in range(cb):
                    out_v[tl, pl.ds(c0 + j * SC_LANES, SC_LANES)] = accs[j]
                return carry

            lax.fori_loop(0, nchunk // cb, cb_body, 0)

        def batch(b, carry):
            t0 = wid * tpw + b * tb
            pltpu.sync_copy(w_hbm.at[pl.ds(t0, tb)], w_v)
            pltpu.sync_copy(idx_hbm.at[pl.ds(t0, tb)], idx_v)

            def zbody(i, carry):
                out_v[i // nchunk, pl.ds((i % nchunk) * SC_LANES, SC_LANES)] = zero
                return carry

            lax.fori_loop(0, tb * nchunk, zbody, 0)
            _sc_pipeline(idx_v, v_hbm, (rows0, rows1), (sem0, sem1), tb * upt, upt, compute)
            pltpu.sync_copy(out_v, out_hbm.at[pl.ds(t0, tb)])
            return carry

        lax.fori_loop(0, tpw // tb, batch, 0)

    return pl.kernel(
        body, mesh=mesh,
        compiler_params=pltpu.CompilerParams(needs_layout_passes=False),
        out_type=jax.ShapeDtypeStruct((n, d), F32),
        scratch_types=[pltpu.VMEM((tb, kk), F32), pltpu.VMEM((tb, kk), jnp.int32),
                       pltpu.VMEM((tb, d), F32),
                       pltpu.VMEM((unit_rows, d), F32), pltpu.VMEM((unit_rows, d), F32),
                       pltpu.SemaphoreType.DMA, pltpu.SemaphoreType.DMA],
        name="peer_out",
    )(w, idx, v)


SUBLANES = 8
TC_PEER_TOKENS = 16
ISSUE_UNROLL = 8


def _gelu(a):
    return a * (lax.erf(a / np.sqrt(2.0).astype(np.float32)) + 1.0) / 2.0


def _peer_tc_body(idx_ref, idx_next_ref, x_ref, g_ref, u_hbm, v_hbm, o_ref, ubuf, vbuf, sem_u, sem_v):
    tokens = x_ref.shape[0]
    kk = g_ref.shape[1]
    n_rows = tokens * kk
    step = pl.program_id(0)
    slot = step % 2

    def row_copy(table, buf, sem, expert, j):
        src = table.at[pl.ds(pl.multiple_of(expert * SUBLANES, SUBLANES), SUBLANES)]
        return pltpu.make_async_copy(src, buf.at[pl.ds(j * SUBLANES, SUBLANES)], sem)

    def issue_block(ids_ref, into):
        def issue(i, carry):
            for r in range(ISSUE_UNROLL):
                j = i * ISSUE_UNROLL + r
                expert = ids_ref[j]
                row_copy(u_hbm, ubuf.at[into], sem_u.at[into], expert, j).start()
                row_copy(v_hbm, vbuf.at[into], sem_v.at[into], expert, j).start()
            return carry

        lax.fori_loop(0, n_rows // ISSUE_UNROLL, issue, 0)

    @pl.when(step == 0)
    def _():
        issue_block(idx_ref, 0)

    @pl.when(step + 1 < pl.num_programs(0))
    def _():
        issue_block(idx_next_ref, 1 - slot)

    def wait_all(table, buf, sem):
        pltpu.make_async_copy(table.at[pl.ds(0, n_rows * SUBLANES)], buf.at[slot], sem.at[slot]).wait()

    def rows_of(buf, t, s):
        return buf[slot, pl.ds(t * kk * SUBLANES + s, kk, stride=SUBLANES), :]

    wait_all(u_hbm, ubuf, sem_u)
    acts = []
    for t in range(tokens):
        acc = rows_of(ubuf, t, 0) * x_ref[t:t + 1, 0:LANES]
        for s in range(1, SUBLANES):
            acc = acc + rows_of(ubuf, t, s) * x_ref[t:t + 1, s * LANES:(s + 1) * LANES]
        acts.append(jnp.sum(acc, axis=1, keepdims=True))
    w = g_ref[0] * _gelu(jnp.concatenate(acts, axis=1))

    wait_all(v_hbm, vbuf, sem_v)
    for t in range(tokens):
        wt = w[:, t:t + 1]
        for s in range(SUBLANES):
            o_ref[t:t + 1, s * LANES:(s + 1) * LANES] = jnp.sum(rows_of(vbuf, t, s) * wt, axis=0, keepdims=True)


def _peer_tc(hn, idx, gate_t, u_tiles, v_tiles):
    m, d = hn.shape
    kk = idx.shape[1]
    tb = TC_PEER_TOKENS
    assert d == SUBLANES * LANES and m % tb == 0
    steps = m // tb
    g3 = gate_t.reshape(kk, steps, tb).transpose(1, 0, 2)
    rows = tb * kk * SUBLANES
    ids = idx.reshape(m * kk)
    return pl.pallas_call(
        _peer_tc_body,
        grid=(steps,),
        in_specs=[pl.BlockSpec((tb * kk,), lambda i: (i,), memory_space=pltpu.SMEM),
                  pl.BlockSpec((tb * kk,), lambda i: (jnp.minimum(i + 1, steps - 1),), memory_space=pltpu.SMEM),
                  pl.BlockSpec((tb, d), lambda i: (i, 0)),
                  pl.BlockSpec((1, kk, tb), lambda i: (i, 0, 0)),
                  pl.BlockSpec(memory_space=pl.ANY),
                  pl.BlockSpec(memory_space=pl.ANY)],
        out_specs=pl.BlockSpec((tb, d), lambda i: (i, 0)),
        out_shape=jax.ShapeDtypeStruct((m, d), F32),
        scratch_shapes=[pltpu.VMEM((2, rows, LANES), F32), pltpu.VMEM((2, rows, LANES), F32),
                        pltpu.SemaphoreType.DMA((2,)), pltpu.SemaphoreType.DMA((2,))],
        compiler_params=pltpu.CompilerParams(dimension_semantics=("arbitrary",),
                                             vmem_limit_bytes=VMEM_LIMIT),
        name="peer_tc",
    )(ids, ids, hn, g3, u_tiles, v_tiles)


def _gelu_gate_body(a_ref, g_ref, o_ref):
    o_ref[...] = g_ref[...] * _gelu(a_ref[...])


def _gelu_gate(act, gate, tm=SHARE_GRANULE):
    n, kk = act.shape
    tm = min(tm, n)
    assert n % tm == 0
    spec = pl.BlockSpec((tm, kk), lambda i: (i, 0))
    return pl.pallas_call(
        _gelu_gate_body, grid=(n // tm,), in_specs=[spec, spec], out_specs=spec,
        out_shape=jax.ShapeDtypeStruct((n, kk), F32),
        compiler_params=pltpu.CompilerParams(dimension_semantics=("parallel",)),
        name="gelu_gate",
    )(act, gate)


def _final_body(h_ref, g_ref, *rest, starts, aliased):
    o_ref = rest[-1]
    parts = rest[:len(starts)]
    assert len(rest) == len(starts) + 1 + int(aliased)
    peer = parts[0][...]
    for start, p_ref in zip(starts[1:], parts[1:]):
        peer = jnp.where(pl.program_id(0) >= start, p_ref[...], peer)
    h = h_ref[...] + peer
    o_ref[...] = h * lax.rsqrt(jnp.mean(h * h, axis=-1, keepdims=True) + EPS) * g_ref[...]


def _final(h1, peer_parts, g, out, row0, n_total, tm=256):
    n, d = h1.shape
    sizes = [p.shape[0] for p in peer_parts]
    assert sum(sizes) == n and all(m > 0 and m % tm == 0 for m in sizes) and row0 % tm == 0
    starts = tuple(int(v) // tm for v in np.cumsum([0] + sizes[:-1]))

    def part_spec(start, size):
        return pl.BlockSpec((tm, d), lambda i: (jnp.clip(i - start, 0, size // tm - 1), 0))

    in_specs = [pl.BlockSpec((tm, d), lambda i: (i, 0)), pl.BlockSpec((1, d), lambda i: (0, 0))]
    in_specs += [part_spec(s, m) for s, m in zip(starts, sizes)]
    args = [h1, g, *peer_parts]
    aliases = {}
    if out is not None:
        in_specs.append(pl.BlockSpec(memory_space=pl.ANY))
        aliases = {len(args): 0}
        args.append(out)
    return pl.pallas_call(
        functools.partial(_final_body, starts=starts, aliased=out is not None), grid=(n // tm,),
        in_specs=in_specs,
        out_specs=pl.BlockSpec((tm, d), lambda i: (i + row0 // tm, 0)),
        out_shape=jax.ShapeDtypeStruct((n_total, d), F32),
        input_output_aliases=aliases,
        compiler_params=pltpu.CompilerParams(dimension_semantics=("parallel",)),
        name="final_norm",
    )(*args)


def _tc_share(n_tokens):
    m = int(n_tokens * TC_SHARE) // SHARE_GRANULE * SHARE_GRANULE
    return min(max(m, SHARE_GRANULE), n_tokens - SHARE_GRANULE)


def kernel(x, norm1_g, w_in, hg_lower_logits, hg_norm_g, gla_w_gate_up, gla_b_gate, gla_norm_g,
           w_out, norm2_g, peer_w_q, peer_sub_keys, peer_u, peer_v, norm_f_g):
    batch, seq, d = x.shape
    depth = w_in.shape[0]
    assert depth == 1, "single-layer block"
    layer = 0

    w = w_in[layer]
    pad = LANES - GLA_GATE_RANK
    assert w.shape == (d, IN_WIDTH_PADDED - pad) and d == SUBLANES * LANES
    w_r = jnp.concatenate(
        [w[:, :OFF_GLOW + GLA_GATE_RANK], jnp.zeros((d, pad), w.dtype),
         w[:, OFF_GLOW + GLA_GATE_RANK:]], axis=1).astype(BF16)
    wgu = jnp.concatenate([gla_w_gate_up[layer], jnp.zeros((pad, GLA_KEY), F32)], axis=0).astype(BF16)

    wo = w_out[layer].astype(BF16)
    wq = peer_w_q[layer].astype(BF16)
    keys = peer_sub_keys[layer].reshape(PEER_HEADS * 2, PEER_N_KEYS, PEER_HALF).astype(BF16)
    n_experts = peer_u.shape[1]
    u_tiles = peer_u[layer].reshape(n_experts * SUBLANES, LANES)
    v_tiles = peer_v[layer].reshape(n_experts * SUBLANES, LANES)

    groups = PIPELINE_GROUPS if batch % PIPELINE_GROUPS == 0 else 1
    bpg = batch // groups
    ng = bpg * seq
    m_tc = _tc_share(ng)
    gf = norm_f_g.reshape(1, d)
    x2 = x.reshape(batch * seq, d)

    def after(value, anchor):
        value, _ = lax.optimization_barrier((value, anchor))
        return value

    half = ng // 2
    h1s, tc_ins, sc_units = [], [], []
    for gi in range(groups):
        proj = _inproj(x2, gi * ng, ng, norm1_g[layer].reshape(1, d), w_r)
        mixed = _mixers(proj, bpg, seq, hg_lower_logits, hg_norm_g[layer].reshape(1, HEAD_V), wgu,
                        gla_b_gate[layer].reshape(1, GLA_KEY), gla_norm_g[layer].reshape(1, HEAD_V))
        h1, hn, q3 = _outproj(x2, gi * ng, mixed, wo, norm2_g[layer].reshape(1, d), wq)
        h1s.append(h1)
        m_g = m_tc - SHARE_GRANULE if 0 < gi == groups - 1 and m_tc > SHARE_GRANULE else m_tc
        bounds = (m_g, half, ng) if gi == 0 and m_g < half else (m_g, ng)
        units = []
        for lo, hi in zip(bounds[:-1], bounds[1:]):
            idx_t, gate_t = _route(q3, keys, lo, hi - lo)
            idx = idx_t.T
            units.append(dict(idx=idx, gate=gate_t.T, act=_peer_act(hn[lo:hi], idx, peer_u[layer])))
        sc_units.append(units)
        idx_t, gate_t = _route(q3, keys, 0, m_g)
        tc_ins.append((hn[:m_g], idx_t.T, gate_t))

    def sc_out(gi, anchor):
        outs = []
        for unit in sc_units[gi]:
            wts = _gelu_gate(after(unit["act"], anchor), unit["gate"])
            outs.append(_peer_out(wts, unit["idx"], peer_v[layer]))
        return wts, outs

    rows, peer_sc = [None] * groups, [None] * groups
    rows[0] = _peer_tc(*after(tc_ins[0], sc_units[-1][-1]["idx"]), u_tiles, v_tiles)
    anchor = rows[0]
    for gi in range(groups - 1):
        anchor, peer_sc[gi] = sc_out(gi, rows[0])
    if groups > 1:
        rows[1] = _peer_tc(*after(tc_ins[1], anchor), u_tiles, v_tiles)
        anchor = rows[1]
    anchor, peer_sc[groups - 1] = sc_out(groups - 1, anchor)
    out, finished = None, 0

    def finish(gi, out):
        return _final(h1s[gi], [rows[gi], *peer_sc[gi]], gf, out, gi * ng, batch * seq)

    for gi in range(2, groups):
        out = finish(gi - 2, out)
        finished = gi - 1
        rows[gi] = _peer_tc(*after(tc_ins[gi], (anchor, out)), u_tiles, v_tiles)
        anchor = rows[gi]
    for gi in range(finished, groups):
        out = finish(gi, out)
    return out.reshape(batch, seq, d)
```

```python
import functools

import jax
import jax.numpy as jnp
import numpy as np
from jax import lax
from jax.experimental import pallas as pl
from jax.experimental.pallas import tpu as pltpu
from jax.experimental.pallas import tpu_sc as plsc

F32 = jnp.float32
BF16 = jnp.bfloat16
EPS = 1e-6

HG_HEADS = 4
GLA_HEADS = 4
HEAD_V = 128
HG_KEY = 512
GLA_KEY = 256
GLA_HEAD_K = 64
GLA_GATE_RANK = 16
GLA_GATE_NORMALIZER = 16.0
CHUNK = 64
LANES = 128
PEER_HEADS = 8
PEER_N_KEYS = 128
PEER_HALF = 128
PEER_TOPK = 16

OFF_HQ, OFF_HF, OFF_HI, OFF_HGATE = 0, 512, 1024, 1536
OFF_GQ, OFF_GK, OFF_GV, OFF_GLOW, OFF_GGATE = 2048, 2304, 2560, 3072, 3200
IN_WIDTH_PADDED = 3712

VMEM_LIMIT = 48 * 1024 * 1024

SC_LANES = 16
SC_WORKERS = 32
PIPELINE_GROUPS = 4
TC_SHARE = 0.25
SHARE_GRANULE = 256


def _nt(a, b):
    return lax.dot_general(a, b, (((1,), (1,)), ((), ())), preferred_element_type=F32)


def _tn(a, b):
    return lax.dot_general(a, b, (((0,), (0,)), ((), ())), preferred_element_type=F32)


def _inproj_body(x_ref, g_ref, w_ref, o_ref):
    x = x_ref[...]
    xn = x * lax.rsqrt(jnp.mean(x * x, axis=-1, keepdims=True) + EPS) * g_ref[...]
    o_ref[...] = jnp.dot(xn.astype(BF16), w_ref[...], preferred_element_type=F32)


def _inproj(x2, row0, n, g, w, tm=256):
    d = x2.shape[1]
    wd = w.shape[1]
    assert row0 % tm == 0 and n % tm == 0
    return pl.pallas_call(
        _inproj_body,
        grid=(n // tm,),
        in_specs=[pl.BlockSpec((tm, d), lambda i: (i + row0 // tm, 0)),
                  pl.BlockSpec((1, d), lambda i: (0, 0)),
                  pl.BlockSpec((d, wd), lambda i: (0, 0))],
        out_specs=pl.BlockSpec((tm, wd), lambda i: (i, 0)),
        out_shape=jax.ShapeDtypeStruct((n, wd), F32),
        compiler_params=pltpu.CompilerParams(dimension_semantics=("parallel",),
                                             vmem_limit_bytes=VMEM_LIMIT),
        name="inproj",
    )(x2, g, w)


def _level_constants():
    c = CHUNK
    mats = [np.tril(np.ones((c, c), np.float32))]
    level = np.full((c, c), -1, np.int32)
    b, lvl = c // 2, 0
    while b >= 1:
        m = np.zeros((c, c), np.float32)
        for s in range(0, c, 2 * b):
            mid = s + b
            for i in range(mid, s + 2 * b):
                m[i, mid:i + 1] = 1.0
                level[i, s:mid] = lvl
            for j in range(s, mid):
                m[j, j + 1:mid] = 1.0
        mats.append(m)
        b //= 2
        lvl += 1
    level[np.arange(c), np.arange(c)] = lvl
    return np.concatenate(mats, axis=0), level, lvl


_SEG_MATS, _LEVEL_MAP, _N_LEVELS = _level_constants()


def _split3(a):
    hi = a.astype(BF16)
    r = a - hi.astype(F32)
    mid = r.astype(BF16)
    lo = (r - mid.astype(F32)).astype(BF16)
    return jnp.concatenate([hi, mid, lo], axis=1)


def _gla_chunk(q, k, g, v, st, seg, level):
    c = CHUNK
    ex3 = jnp.dot(seg, _split3(g), preferred_element_type=F32)
    ex = ex3[:, 0:LANES] + ex3[:, LANES:2 * LANES] + ex3[:, 2 * LANES:3 * LANES]
    cum = ex[0:c]
    scores = jnp.where(level == _N_LEVELS, _nt(q.astype(BF16), k.astype(BF16)), 0.0)
    for l in range(_N_LEVELS):
        e = jnp.exp(ex[c * (l + 1):c * (l + 2)])
        p = _nt((q * e).astype(BF16), (k * e).astype(BF16))
        scores = jnp.where(level == l, p, scores)
    last = cum[c - 1:c, :]
    qd = (q * jnp.exp(cum)).astype(BF16)
    kd = (k * jnp.exp(last - cum)).astype(BF16)
    vb = v.astype(BF16)
    o = jnp.dot(scores.astype(BF16), vb, preferred_element_type=F32) + _nt(qd, st.astype(BF16))
    st_new = st * jnp.exp(last) + _tn(vb, kd)
    return o, st_new


def _head_out(o, gain, gate):
    on = o * lax.rsqrt(jnp.mean(o * o, axis=-1, keepdims=True) + EPS) * gain
    return on * (gate * jax.nn.sigmoid(gate))


def _mixer_body(p_ref, seg_ref, level_ref, lbl_ref, hgn_ref, wgu_ref, bg_ref, ggn_ref,
                o_ref, st_ref, *, chunks):
    @pl.when(pl.program_id(1) == 0)
    def _():
        st_ref[...] = jnp.zeros_like(st_ref)

    seg = seg_ref[...]
    level = level_ref[...]
    logits = lbl_ref[...]
    ez = jnp.exp(logits - jnp.max(logits, axis=0, keepdims=True))
    lb = ez[0:1, :] / jnp.sum(ez, axis=0, keepdims=True)
    lane = lax.broadcasted_iota(jnp.int32, (CHUNK, LANES), 1)

    def chunk_body(ci, carry):
        rows = pl.ds(pl.multiple_of(ci * CHUNK, CHUNK), CHUNK)

        def col(off):
            return p_ref[rows, pl.ds(off, LANES)]

        for h in range(HG_HEADS):
            hq = col(OFF_HQ + h * LANES)
            q = hq * jax.nn.sigmoid(hq)
            lbh = lb[:, h * LANES:(h + 1) * LANES]
            forget = lbh + (1.0 - lbh) * jax.nn.sigmoid(col(OFF_HF + h * LANES))
            o, st = _gla_chunk(q, 1.0 - forget, jnp.log(forget), col(OFF_HI + h * LANES),
                               st_ref[h], seg, level)
            st_ref[h] = st
            o_ref[rows, pl.ds(h * HEAD_V, HEAD_V)] = _head_out(
                o, hgn_ref[...], col(OFF_HGATE + h * HEAD_V)).astype(o_ref.dtype)

        zg = jnp.dot(col(OFF_GLOW).astype(BF16), wgu_ref[...], preferred_element_type=F32) + bg_ref[...]
        log_g = (jnp.minimum(zg, 0.0) - jnp.log(1.0 + jnp.exp(-jnp.abs(zg)))) / GLA_GATE_NORMALIZER
        for h in range(GLA_HEADS):
            pair, half = h // 2, h % 2
            q = col(OFF_GQ + pair * LANES) * (GLA_HEAD_K ** -0.5)
            in_head = (lane >= half * GLA_HEAD_K) & (lane < (half + 1) * GLA_HEAD_K)
            k = jnp.where(in_head, col(OFF_GK + pair * LANES), 0.0)
            g = log_g[:, pair * LANES:(pair + 1) * LANES]
            o, st = _gla_chunk(q, k, g, col(OFF_GV + h * HEAD_V), st_ref[HG_HEADS + h], seg, level)
            st_ref[HG_HEADS + h] = st
            o_ref[rows, pl.ds((HG_HEADS + h) * HEAD_V, HEAD_V)] = _head_out(
                o, ggn_ref[...], col(OFF_GGATE + h * HEAD_V)).astype(o_ref.dtype)
        return carry

    lax.fori_loop(0, chunks, chunk_body, 0)


def _mixers(proj, batch, seq, lb_logits, hg_norm_g, wgu, bg, gla_norm_g, tt=512):
    n = batch * seq
    steps = seq // tt
    heads = HG_HEADS + GLA_HEADS
    const = lambda shape: pl.BlockSpec(shape, lambda b, t: (0,) * len(shape))
    return pl.pallas_call(
        functools.partial(_mixer_body, chunks=tt // CHUNK),
        grid=(batch, steps),
        in_specs=[pl.BlockSpec((tt, IN_WIDTH_PADDED), lambda b, t: (b * steps + t, 0)),
                  const(_SEG_MATS.shape), const(_LEVEL_MAP.shape),
                  const(lb_logits.shape), const((1, HEAD_V)),
                  const(wgu.shape), const(bg.shape), const((1, HEAD_V))],
        out_specs=pl.BlockSpec((tt, heads * HEAD_V), lambda b, t: (b * steps + t, 0)),
        out_shape=jax.ShapeDtypeStruct((n, heads * HEAD_V), BF16),
        scratch_shapes=[pltpu.VMEM((heads, HEAD_V, LANES), F32)],
        compiler_params=pltpu.CompilerParams(dimension_semantics=("parallel", "arbitrary"),
                                             vmem_limit_bytes=VMEM_LIMIT),
        name="mixers",
    )(proj, jnp.asarray(_SEG_MATS, BF16), jnp.asarray(_LEVEL_MAP), lb_logits, hg_norm_g,
      wgu, bg, gla_norm_g)


def _outproj_body(x_ref, m_ref, wo_ref, g_ref, wq_ref, h_ref, hn_ref, q_ref):
    h = x_ref[...] + jnp.dot(m_ref[...], wo_ref[...], preferred_element_type=F32)
    h_ref[...] = h
    hn = h * lax.rsqrt(jnp.mean(h * h, axis=-1, keepdims=True) + EPS) * g_ref[...]
    hn_ref[...] = hn
    q = jnp.dot(hn.astype(BF16), wq_ref[...], preferred_element_type=F32)
    for j in range(q_ref.shape[0]):
        q_ref[j] = q[:, j * PEER_HALF:(j + 1) * PEER_HALF]


def _outproj(x2, row0, mixed, wo, g2, wq, tm=256):
    n, d = mixed.shape[0], x2.shape[1]
    nq = wq.shape[1] // PEER_HALF
    assert row0 % tm == 0 and n % tm == 0
    return pl.pallas_call(
        _outproj_body,
        grid=(n // tm,),
        in_specs=[pl.BlockSpec((tm, d), lambda i: (i + row0 // tm, 0)),
                  pl.BlockSpec((tm, mixed.shape[1]), lambda i: (i, 0)),
                  pl.BlockSpec(wo.shape, lambda i: (0, 0)),
                  pl.BlockSpec((1, d), lambda i: (0, 0)),
                  pl.BlockSpec(wq.shape, lambda i: (0, 0))],
        out_specs=[pl.BlockSpec((tm, d), lambda i: (i, 0)),
                   pl.BlockSpec((tm, d), lambda i: (i, 0)),
                   pl.BlockSpec((nq, tm, PEER_HALF), lambda i: (0, i, 0))],
        out_shape=[jax.ShapeDtypeStruct((n, d), F32),
                   jax.ShapeDtypeStruct((n, d), F32),
                   jax.ShapeDtypeStruct((nq, n, PEER_HALF), F32)],
        compiler_params=pltpu.CompilerParams(dimension_semantics=("parallel",),
                                             vmem_limit_bytes=VMEM_LIMIT),
        name="outproj",
    )(x2, mixed, wo, g2, wq)


def _candidate_constants():
    k = PEER_TOPK
    pos, valid = [], []
    for j in range(k):
        pos.append(j); valid.append(True)
    for i in range(1, 8):
        for j in range(8):
            pos.append(i * k + j); valid.append((i + 1) * (j + 1) <= k)
    for i in range(8, k):
        pos.append(i * k); valid.append(True)
    return np.asarray(pos, np.float32), np.asarray(valid, bool)


_CAND_POS, _CAND_VALID = _candidate_constants()
_N_CAND = _CAND_POS.shape[0]


def _topk_rows(ss, pos, extras, k):
    t = ss[0].shape[1]
    slot = lax.broadcasted_iota(jnp.int32, (k, t), 0)

    def body(it, carry):
        out = []
        for (s, vals, poss, exts), extra in zip(carry, extras):
            m = jnp.max(s, axis=0, keepdims=True)
            p = jnp.min(jnp.where(s == m, pos, 1e9), axis=0, keepdims=True)
            hit = pos == p
            vals = jnp.where(slot == it, m, vals)
            poss = jnp.where(slot == it, p, poss)
            if extra is not None:
                x = jnp.max(jnp.where(hit, extra, -1.0), axis=0, keepdims=True)
                exts = jnp.where(slot == it, x, exts)
            out.append((jnp.where(hit, -jnp.inf, s), vals, poss, exts))
        return tuple(out)

    z = jnp.zeros((k, t), F32)
    res = lax.fori_loop(0, k, body, tuple((s, z, z, z) for s in ss))
    return [(vals, poss, exts) for _, vals, poss, exts in res]


def _route_body(q_ref, keys_ref, cpos_ref, cvalid_ref, idx_ref, gate_ref):
    k = PEER_TOPK
    n_tiles = q_ref.shape[1] // LANES
    key_pos = lax.broadcasted_iota(jnp.int32, (PEER_N_KEYS, LANES), 0).astype(F32)
    cpos = cpos_ref[...]
    cvalid = cvalid_ref[...] > 0.5
    scores = [_nt(keys_ref[p], q_ref[p, pl.ds(lt * LANES, LANES), :].astype(BF16))
              for lt in range(n_tiles) for p in range(2)]
    tops = _topk_rows(scores, key_pos, [None] * len(scores), k)
    cands, experts = [], []
    for lt in range(n_tiles):
        (v0, i0, _), (v1, i1, _) = tops[2 * lt], tops[2 * lt + 1]
        vals = [v0[0:1] + v1]
        exps = [i0[0:1] * PEER_N_KEYS + i1]
        for i in range(1, 8):
            vals.append(v0[i:i + 1] + v1[0:8])
            exps.append(i0[i:i + 1] * PEER_N_KEYS + i1[0:8])
        vals.append(v0[8:k] + v1[0:1])
        exps.append(i0[8:k] * PEER_N_KEYS + i1[0:1])
        cands.append(jnp.where(cvalid, jnp.concatenate(vals, axis=0), -jnp.inf))
        experts.append(jnp.concatenate(exps, axis=0))
    picks = _topk_rows(cands, cpos, experts, k)
    for lt in range(n_tiles):
        best, _, expert = picks[lt]
        cols = pl.ds(lt * LANES, LANES)
        e = jnp.exp(best - jnp.max(best, axis=0, keepdims=True))
        gate_ref[:, cols] = e / jnp.sum(e, axis=0, keepdims=True)
        idx_ref[:, cols] = expert.astype(jnp.int32)


def _route(q3, keys, tok0, n):
    heads = q3.shape[0] // 2
    tb = 512 if tok0 % 512 == 0 and n % 512 == 0 else SHARE_GRANULE
    assert tok0 % tb == 0 and n % tb == 0
    cpos = jnp.broadcast_to(jnp.asarray(_CAND_POS)[:, None], (_N_CAND, LANES))
    cvalid = jnp.broadcast_to(jnp.asarray(_CAND_VALID, F32)[:, None], (_N_CAND, LANES))
    return pl.pallas_call(
        _route_body,
        grid=(n // tb, heads),
        in_specs=[pl.BlockSpec((2, tb, PEER_HALF), lambda i, h: (h, i + tok0 // tb, 0)),
                  pl.BlockSpec((2, PEER_N_KEYS, PEER_HALF), lambda i, h: (h, 0, 0)),
                  pl.BlockSpec((_N_CAND, LANES), lambda i, h: (0, 0)),
                  pl.BlockSpec((_N_CAND, LANES), lambda i, h: (0, 0))],
        out_specs=[pl.BlockSpec((PEER_TOPK, tb), lambda i, h: (h, i)),
                   pl.BlockSpec((PEER_TOPK, tb), lambda i, h: (h, i))],
        out_shape=[jax.ShapeDtypeStruct((heads * PEER_TOPK, n), jnp.int32),
                   jax.ShapeDtypeStruct((heads * PEER_TOPK, n), F32)],
        compiler_params=pltpu.CompilerParams(dimension_semantics=("parallel", "parallel"),
                                             vmem_limit_bytes=VMEM_LIMIT),
        name="route",
    )(q3, keys, cpos, cvalid)


def _sc_pipeline(idx_v, table_hbm, rows, sems, n_units, units_per_token, compute):
    parts = rows[0].shape[0] // SC_LANES
    assert n_units % 2 == 0

    def start(unit, slot):
        tl = unit // units_per_token
        g = unit % units_per_token
        for part in range(parts):
            iv = idx_v[tl, pl.ds((g * parts + part) * SC_LANES, SC_LANES)]
            pltpu.async_copy(table_hbm.at[iv], rows[slot].at[pl.ds(part * SC_LANES, SC_LANES)], sems[slot])

    def wait(slot):
        iv = idx_v[0, pl.ds(0, SC_LANES)]
        for part in range(parts):
            pltpu.make_async_copy(table_hbm.at[iv], rows[slot].at[pl.ds(part * SC_LANES, SC_LANES)],
                                  sems[slot]).wait()

    start(0, 0)

    def pair(i, carry):
        u0 = 2 * i
        start(u0 + 1, 1)
        wait(0)
        compute(u0 // units_per_token, u0 % units_per_token, rows[0])

        @pl.when(u0 + 2 < n_units)
        def _():
            start(u0 + 2, 0)

        wait(1)
        compute((u0 + 1) // units_per_token, (u0 + 1) % units_per_token, rows[1])
        return carry

    lax.fori_loop(0, n_units // 2, pair, 0)


def _sc_batch(tokens_per_worker):
    for tb in (40, 32, 24, 16, 8):
        if tokens_per_worker % tb == 0:
            return tb
    raise ValueError(f"tokens per SparseCore worker ({tokens_per_worker}) must be a multiple of 8")


def _peer_act(hn, idx, u, unit_rows=2 * SC_LANES):
    n, d = hn.shape
    kk = idx.shape[1]
    tpw = n // SC_WORKERS
    tb = _sc_batch(tpw)
    upt = kk // unit_rows
    parts = unit_rows // SC_LANES
    nchunk = d // SC_LANES
    mesh = plsc.VectorSubcoreMesh(core_axis_name="c", subcore_axis_name="s")

    def body(x_hbm, idx_hbm, u_hbm, out_hbm, x_v, idx_v, act_v, rows0, rows1, tmp_v, sem0, sem1):
        wid = lax.axis_index("s") * 2 + lax.axis_index("c")
        lane_row = lax.iota(jnp.int32, SC_LANES) * SC_LANES

        def compute(tl, g, r_ref):
            for part in range(parts):
                row0 = part * SC_LANES

                def cbody(c, accs, row0=row0):
                    xv = x_v[tl, pl.ds(c * SC_LANES, SC_LANES)]
                    return tuple(accs[r] + r_ref[row0 + r, pl.ds(c * SC_LANES, SC_LANES)] * xv
                                 for r in range(SC_LANES))
                accs = lax.fori_loop(0, nchunk, cbody,
                                     tuple(jnp.zeros((SC_LANES,), F32) for _ in range(SC_LANES)))
                for r in range(SC_LANES):
                    tmp_v[pl.ds(r * SC_LANES, SC_LANES)] = accs[r]
                tot = plsc.load_gather(tmp_v, [lane_row])
                for j in range(1, SC_LANES):
                    tot = tot + plsc.load_gather(tmp_v, [lane_row + j])
                act_v[tl, pl.ds((g * parts + part) * SC_LANES, SC_LANES)] = tot

        def batch(b, carry):
            t0 = wid * tpw + b * tb
            pltpu.sync_copy(x_hbm.at[pl.ds(t0, tb)], x_v)
            pltpu.sync_copy(idx_hbm.at[pl.ds(t0, tb)], idx_v)
            _sc_pipeline(idx_v, u_hbm, (rows0, rows1), (sem0, sem1), tb * upt, upt, compute)
            pltpu.sync_copy(act_v, out_hbm.at[pl.ds(t0, tb)])
            return carry

        lax.fori_loop(0, tpw // tb, batch, 0)

    return pl.kernel(
        body, mesh=mesh,
        compiler_params=pltpu.CompilerParams(needs_layout_passes=False),
        out_type=jax.ShapeDtypeStruct((n, kk), F32),
        scratch_types=[pltpu.VMEM((tb, d), F32), pltpu.VMEM((tb, kk), jnp.int32),
                       pltpu.VMEM((tb, kk), F32),
                       pltpu.VMEM((unit_rows, d), F32), pltpu.VMEM((unit_rows, d), F32),
                       pltpu.VMEM((SC_LANES * SC_LANES,), F32),
                       pltpu.SemaphoreType.DMA, pltpu.SemaphoreType.DMA],
        name="peer_act",
    )(hn, idx, u)


def _peer_out(w, idx, v, cb=16, unit_rows=2 * SC_LANES):
    n, kk = w.shape
    d = v.shape[1]
    tpw = n // SC_WORKERS
    tb = _sc_batch(tpw)
    upt = kk // unit_rows
    nchunk = d // SC_LANES
    mesh = plsc.VectorSubcoreMesh(core_axis_name="c", subcore_axis_name="s")

    def body(w_hbm, idx_hbm, v_hbm, out_hbm, w_v, idx_v, out_v, rows0, rows1, sem0, sem1):
        wid = lax.axis_index("s") * 2 + lax.axis_index("c")
        zero = jnp.zeros((SC_LANES,), F32)

        def compute(tl, g, r_ref):
            tls = jnp.full((SC_LANES,), tl, jnp.int32)

            def cb_body(ci, carry):
                c0 = ci * (cb * SC_LANES)
                accs = tuple(out_v[tl, pl.ds(c0 + j * SC_LANES, SC_LANES)] for j in range(cb))

                def rbody(r, accs):
                    wr = plsc.load_gather(w_v, [tls, jnp.full((SC_LANES,), g * unit_rows + r, jnp.int32)])
                    return tuple(accs[j] + wr * r_ref[r, pl.ds(c0 + j * SC_LANES, SC_LANES)]
                                 for j in range(cb))

                accs = lax.fori_loop(0, unit_rows, rbody, accs)
                for j in range(cb):
                    out_v[tl, pl.ds(c0 + j * SC_LANES, SC_LANES)] = accs[j]
                return carry

            lax.fori_loop(0, nchunk // cb, cb_body, 0)

        def batch(b, carry):
            t0 = wid * tpw + b * tb
            pltpu.sync_copy(w_hbm.at[pl.ds(t0, tb)], w_v)
            pltpu.sync_copy(idx_hbm.at[pl.ds(t0, tb)], idx_v)

            def zbody(i, carry):
                out_v[i // nchunk, pl.ds((i % nchunk) * SC_LANES, SC_LANES)] = zero
                return carry

            lax.fori_loop(0, tb * nchunk, zbody, 0)
            _sc_pipeline(idx_v, v_hbm, (rows0, rows1), (sem0, sem1), tb * upt, upt, compute)
            pltpu.sync_copy(out_v, out_hbm.at[pl.ds(t0, tb)])
            return carry

        lax.fori_loop(0, tpw // tb, batch, 0)

    return pl.kernel(
        body, mesh=mesh,
        compiler_params=pltpu.CompilerParams(needs_layout_passes=False),
        out_type=jax.ShapeDtypeStruct((n, d), F32),
        scratch_types=[pltpu.VMEM((tb, kk), F32), pltpu.VMEM((tb, kk), jnp.int32),
                       pltpu.VMEM((tb, d), F32),
                       pltpu.VMEM((unit_rows, d), F32), pltpu.VMEM((unit_rows, d), F32),
                       pltpu.SemaphoreType.DMA, pltpu.SemaphoreType.DMA],
        name="peer_out",
    )(w, idx, v)


SUBLANES = 8
TC_PEER_TOKENS = 8
ISSUE_UNROLL = 8


def _gelu(a):
    return a * (lax.erf(a / np.sqrt(2.0).astype(np.float32)) + 1.0) / 2.0


def _peer_tc_body(idx_ref, idx_next_ref, x_ref, g_ref, u_hbm, v_hbm, o_ref, ubuf, vbuf, sem_u, sem_v):
    tokens = x_ref.shape[0]
    kk = g_ref.shape[1]
    n_rows = tokens * kk
    step = pl.program_id(0)
    slot = step % 2

    def row_copy(table, buf, sem, expert, j):
        src = table.at[pl.ds(pl.multiple_of(expert * SUBLANES, SUBLANES), SUBLANES)]
        return pltpu.make_async_copy(src, buf.at[pl.ds(j * SUBLANES, SUBLANES)], sem)

    def issue_block(ids_ref, into):
        def issue(i, carry):
            for r in range(ISSUE_UNROLL):
                j = i * ISSUE_UNROLL + r
                expert = ids_ref[j]
                row_copy(u_hbm, ubuf.at[into], sem_u.at[into], expert, j).start()
                row_copy(v_hbm, vbuf.at[into], sem_v.at[into], expert, j).start(priority=1)
            return carry

        lax.fori_loop(0, n_rows // ISSUE_UNROLL, issue, 0)

    @pl.when(step == 0)
    def _():
        issue_block(idx_ref, 0)

    @pl.when(step + 1 < pl.num_programs(0))
    def _():
        issue_block(idx_next_ref, 1 - slot)

    def wait_all(table, buf, sem):
        pltpu.make_async_copy(table.at[pl.ds(0, n_rows * SUBLANES)], buf.at[slot], sem.at[slot]).wait()

    def rows_of(buf, t, s):
        return buf[slot, pl.ds(t * kk * SUBLANES + s, kk, stride=SUBLANES), :]

    wait_all(u_hbm, ubuf, sem_u)
    acts = []
    for t in range(tokens):
        acc = rows_of(ubuf, t, 0) * x_ref[t:t + 1, 0:LANES]
        for s in range(1, SUBLANES):
            acc = acc + rows_of(ubuf, t, s) * x_ref[t:t + 1, s * LANES:(s + 1) * LANES]
        acts.append(jnp.sum(acc, axis=1, keepdims=True))
    w = g_ref[0] * _gelu(jnp.concatenate(acts, axis=1))

    wait_all(v_hbm, vbuf, sem_v)
    for t in range(tokens):
        wt = w[:, t:t + 1]
        for s in range(SUBLANES):
            o_ref[t:t + 1, s * LANES:(s + 1) * LANES] = jnp.sum(rows_of(vbuf, t, s) * wt, axis=0, keepdims=True)


def _peer_tc(hn, idx, gate_t, u_tiles, v_tiles):
    m, d = hn.shape
    kk = idx.shape[1]
    tb = TC_PEER_TOKENS
    assert d == SUBLANES * LANES and m % tb == 0
    steps = m // tb
    g3 = gate_t.reshape(kk, steps, tb).transpose(1, 0, 2)
    rows = tb * kk * SUBLANES
    ids = idx.reshape(m * kk)
    return pl.pallas_call(
        _peer_tc_body,
        grid=(steps,),
        in_specs=[pl.BlockSpec((tb * kk,), lambda i: (i,), memory_space=pltpu.SMEM),
                  pl.BlockSpec((tb * kk,), lambda i: (jnp.minimum(i + 1, steps - 1),), memory_space=pltpu.SMEM),
                  pl.BlockSpec((tb, d), lambda i: (i, 0)),
                  pl.BlockSpec((1, kk, tb), lambda i: (i, 0, 0)),
                  pl.BlockSpec(memory_space=pl.ANY),
                  pl.BlockSpec(memory_space=pl.ANY)],
        out_specs=pl.BlockSpec((tb, d), lambda i: (i, 0)),
        out_shape=jax.ShapeDtypeStruct((m, d), F32),
        scratch_shapes=[pltpu.VMEM((2, rows, LANES), F32), pltpu.VMEM((2, rows, LANES), F32),
                        pltpu.SemaphoreType.DMA((2,)), pltpu.SemaphoreType.DMA((2,))],
        compiler_params=pltpu.CompilerParams(dimension_semantics=("arbitrary",),
                                             vmem_limit_bytes=VMEM_LIMIT),
        name="peer_tc",
    )(ids, ids, hn, g3, u_tiles, v_tiles)


def _gelu_gate_body(a_ref, g_ref, o_ref):
    o_ref[...] = g_ref[...] * _gelu(a_ref[...])


def _gelu_gate(act, gate, tm=SHARE_GRANULE):
    n, kk = act.shape
    tm = min(tm, n)
    assert n % tm == 0
    spec = pl.BlockSpec((tm, kk), lambda i: (i, 0))
    return pl.pallas_call(
        _gelu_gate_body, grid=(n // tm,), in_specs=[spec, spec], out_specs=spec,
        out_shape=jax.ShapeDtypeStruct((n, kk), F32),
        compiler_params=pltpu.CompilerParams(dimension_semantics=("parallel",)),
        name="gelu_gate",
    )(act, gate)


def _final_body(h_ref, g_ref, *rest, starts, aliased):
    o_ref = rest[-1]
    parts = rest[:len(starts)]
    assert len(rest) == len(starts) + 1 + int(aliased)
    peer = parts[0][...]
    for start, p_ref in zip(starts[1:], parts[1:]):
        peer = jnp.where(pl.program_id(0) >= start, p_ref[...], peer)
    h = h_ref[...] + peer
    o_ref[...] = h * lax.rsqrt(jnp.mean(h * h, axis=-1, keepdims=True) + EPS) * g_ref[...]


def _final(h1, peer_parts, g, out, row0, n_total, tm=256):
    n, d = h1.shape
    sizes = [p.shape[0] for p in peer_parts]
    assert sum(sizes) == n and all(m > 0 and m % tm == 0 for m in sizes) and row0 % tm == 0
    starts = tuple(int(v) // tm for v in np.cumsum([0] + sizes[:-1]))

    def part_spec(start, size):
        return pl.BlockSpec((tm, d), lambda i: (jnp.clip(i - start, 0, size // tm - 1), 0))

    in_specs = [pl.BlockSpec((tm, d), lambda i: (i, 0)), pl.BlockSpec((1, d), lambda i: (0, 0))]
    in_specs += [part_spec(s, m) for s, m in zip(starts, sizes)]
    args = [h1, g, *peer_parts]
    aliases = {}
    if out is not None:
        in_specs.append(pl.BlockSpec(memory_space=pl.ANY))
        aliases = {len(args): 0}
        args.append(out)
    return pl.pallas_call(
        functools.partial(_final_body, starts=starts, aliased=out is not None), grid=(n // tm,),
        in_specs=in_specs,
        out_specs=pl.BlockSpec((tm, d), lambda i: (i + row0 // tm, 0)),
        out_shape=jax.ShapeDtypeStruct((n_total, d), F32),
        input_output_aliases=aliases,
        compiler_params=pltpu.CompilerParams(dimension_semantics=("parallel",)),
        name="final_norm",
    )(*args)


def _tc_share(n_tokens):
    m = int(n_tokens * TC_SHARE) // SHARE_GRANULE * SHARE_GRANULE
    return min(max(m, SHARE_GRANULE), n_tokens - SHARE_GRANULE)


def kernel(x, norm1_g, w_in, hg_lower_logits, hg_norm_g, gla_w_gate_up, gla_b_gate, gla_norm_g,
           w_out, norm2_g, peer_w_q, peer_sub_keys, peer_u, peer_v, norm_f_g):
    batch, seq, d = x.shape
    depth = w_in.shape[0]
    assert depth == 1, "single-layer block"
    layer = 0

    w = w_in[layer]
    pad = LANES - GLA_GATE_RANK
    assert w.shape == (d, IN_WIDTH_PADDED - pad) and d == SUBLANES * LANES
    w_r = jnp.concatenate(
        [w[:, :OFF_GLOW + GLA_GATE_RANK], jnp.zeros((d, pad), w.dtype),
         w[:, OFF_GLOW + GLA_GATE_RANK:]], axis=1).astype(BF16)
    wgu = jnp.concatenate([gla_w_gate_up[layer], jnp.zeros((pad, GLA_KEY), F32)], axis=0).astype(BF16)

    wo = w_out[layer].astype(BF16)
    wq = peer_w_q[layer].astype(BF16)
    keys = peer_sub_keys[layer].reshape(PEER_HEADS * 2, PEER_N_KEYS, PEER_HALF).astype(BF16)
    n_experts = peer_u.shape[1]
    u_tiles = peer_u[layer].reshape(n_experts * SUBLANES, LANES)
    v_tiles = peer_v[layer].reshape(n_experts * SUBLANES, LANES)

    groups = PIPELINE_GROUPS if batch % PIPELINE_GROUPS == 0 else 1
    bpg = batch // groups
    ng = bpg * seq
    m_tc = _tc_share(ng)
    gf = norm_f_g.reshape(1, d)
    x2 = x.reshape(batch * seq, d)

    def after(value, anchor):
        value, _ = lax.optimization_barrier((value, anchor))
        return value

    half = ng // 2
    h1s, tc_ins, sc_units = [], [], []
    for gi in range(groups):
        proj = _inproj(x2, gi * ng, ng, norm1_g[layer].reshape(1, d), w_r)
        mixed = _mixers(proj, bpg, seq, hg_lower_logits, hg_norm_g[layer].reshape(1, HEAD_V), wgu,
                        gla_b_gate[layer].reshape(1, GLA_KEY), gla_norm_g[layer].reshape(1, HEAD_V))
        h1, hn, q3 = _outproj(x2, gi * ng, mixed, wo, norm2_g[layer].reshape(1, d), wq)
        h1s.append(h1)
        m_g = m_tc - SHARE_GRANULE if 0 < gi == groups - 1 and m_tc > SHARE_GRANULE else m_tc
        bounds = (m_g, half, ng) if gi == 0 and m_g < half else (m_g, ng)
        units = []
        for lo, hi in zip(bounds[:-1], bounds[1:]):
            idx_t, gate_t = _route(q3, keys, lo, hi - lo)
            idx = idx_t.T
            units.append(dict(idx=idx, gate=gate_t.T, act=_peer_act(hn[lo:hi], idx, peer_u[layer])))
        sc_units.append(units)
        idx_t, gate_t = _route(q3, keys, 0, m_g)
        tc_ins.append((hn[:m_g], idx_t.T, gate_t))

    def sc_out(gi, anchor):
        outs = []
        for unit in sc_units[gi]:
            wts = _gelu_gate(after(unit["act"], anchor), unit["gate"])
            outs.append(_peer_out(wts, unit["idx"], peer_v[layer]))
        return wts, outs

    rows, peer_sc = [None] * groups, [None] * groups
    rows[0] = _peer_tc(*after(tc_ins[0], sc_units[-1][-1]["idx"]), u_tiles, v_tiles)
    anchor = rows[0]
    for gi in range(groups - 1):
        anchor, peer_sc[gi] = sc_out(gi, rows[0])
    if groups > 1:
        rows[1] = _peer_tc(*after(tc_ins[1], anchor), u_tiles, v_tiles)
        anchor = rows[1]
    anchor, peer_sc[groups - 1] = sc_out(groups - 1, anchor)
    out, finished = None, 0

    def finish(gi, out):
        return _final(h1s[gi], [rows[gi], *peer_sc[gi]], gf, out, gi * ng, batch * seq)

    for gi in range(2, groups):
        out = finish(gi - 2, out)
        finished = gi - 1
        rows[gi] = _peer_tc(*after(tc_ins[gi], (anchor, out)), u_tiles, v_tiles)
        anchor = rows[gi]
    for gi in range(finished, groups):
        out = finish(gi, out)
    return out.reshape(batch, seq, d)
```

```python
import functools

import jax
import jax.numpy as jnp
import numpy as np
from jax import lax
from jax.experimental import pallas as pl
from jax.experimental.pallas import tpu as pltpu
from jax.experimental.pallas import tpu_sc as plsc

F32 = jnp.float32
BF16 = jnp.bfloat16
EPS = 1e-6

HG_HEADS = 4
GLA_HEADS = 4
HEAD_V = 128
HG_KEY = 512
GLA_KEY = 256
GLA_HEAD_K = 64
GLA_GATE_RANK = 16
GLA_GATE_NORMALIZER = 16.0
CHUNK = 64
LANES = 128
PEER_HEADS = 8
PEER_N_KEYS = 128
PEER_HALF = 128
PEER_TOPK = 16

OFF_HQ, OFF_HF, OFF_HI, OFF_HGATE = 0, 512, 1024, 1536
OFF_GQ, OFF_GK, OFF_GV, OFF_GLOW, OFF_GGATE = 2048, 2304, 2560, 3072, 3200
IN_WIDTH_PADDED = 3712

VMEM_LIMIT = 48 * 1024 * 1024

SC_LANES = 16
SC_WORKERS = 32
PIPELINE_GROUPS = 4
TC_SHARE = 0.3125
SHARE_GRANULE = 256


def _nt(a, b):
    return lax.dot_general(a, b, (((1,), (1,)), ((), ())), preferred_element_type=F32)


def _tn(a, b):
    return lax.dot_general(a, b, (((0,), (0,)), ((), ())), preferred_element_type=F32)


def _inproj_body(x_ref, g_ref, w_ref, o_ref):
    x = x_ref[...]
    xn = x * lax.rsqrt(jnp.mean(x * x, axis=-1, keepdims=True) + EPS) * g_ref[...]
    o_ref[...] = jnp.dot(xn.astype(BF16), w_ref[...], preferred_element_type=F32)


def _inproj(x2, row0, n, g, w, tm=256):
    d = x2.shape[1]
    wd = w.shape[1]
    assert row0 % tm == 0 and n % tm == 0
    return pl.pallas_call(
        _inproj_body,
        grid=(n // tm,),
        in_specs=[pl.BlockSpec((tm, d), lambda i: (i + row0 // tm, 0)),
                  pl.BlockSpec((1, d), lambda i: (0, 0)),
                  pl.BlockSpec((d, wd), lambda i: (0, 0))],
        out_specs=pl.BlockSpec((tm, wd), lambda i: (i, 0)),
        out_shape=jax.ShapeDtypeStruct((n, wd), F32),
        compiler_params=pltpu.CompilerParams(dimension_semantics=("parallel",),
                                             vmem_limit_bytes=VMEM_LIMIT),
        name="inproj",
    )(x2, g, w)


def _level_constants():
    c = CHUNK
    mats = [np.tril(np.ones((c, c), np.float32))]
    level = np.full((c, c), -1, np.int32)
    b, lvl = c // 2, 0
    while b >= 1:
        m = np.zeros((c, c), np.float32)
        for s in range(0, c, 2 * b):
            mid = s + b
            for i in range(mid, s + 2 * b):
                m[i, mid:i + 1] = 1.0
                level[i, s:mid] = lvl
            for j in range(s, mid):
                m[j, j + 1:mid] = 1.0
        mats.append(m)
        b //= 2
        lvl += 1
    level[np.arange(c), np.arange(c)] = lvl
    return np.concatenate(mats, axis=0), level, lvl


_SEG_MATS, _LEVEL_MAP, _N_LEVELS = _level_constants()


def _split3(a):
    hi = a.astype(BF16)
    r = a - hi.astype(F32)
    mid = r.astype(BF16)
    lo = (r - mid.astype(F32)).astype(BF16)
    return jnp.concatenate([hi, mid, lo], axis=1)


def _gla_chunk(q, k, g, v, st, seg, level):
    c = CHUNK
    ex3 = jnp.dot(seg, _split3(g), preferred_element_type=F32)
    ex = ex3[:, 0:LANES] + ex3[:, LANES:2 * LANES] + ex3[:, 2 * LANES:3 * LANES]
    cum = ex[0:c]
    scores = jnp.where(level == _N_LEVELS, _nt(q.astype(BF16), k.astype(BF16)), 0.0)
    for l in range(_N_LEVELS):
        e = jnp.exp(ex[c * (l + 1):c * (l + 2)])
        p = _nt((q * e).astype(BF16), (k * e).astype(BF16))
        scores = jnp.where(level == l, p, scores)
    last = cum[c - 1:c, :]
    qd = (q * jnp.exp(cum)).astype(BF16)
    kd = (k * jnp.exp(last - cum)).astype(BF16)
    vb = v.astype(BF16)
    o = jnp.dot(scores.astype(BF16), vb, preferred_element_type=F32) + _nt(qd, st.astype(BF16))
    st_new = st * jnp.exp(last) + _tn(vb, kd)
    return o, st_new


def _head_out(o, gain, gate):
    on = o * lax.rsqrt(jnp.mean(o * o, axis=-1, keepdims=True) + EPS) * gain
    return on * (gate * jax.nn.sigmoid(gate))


def _mixer_body(p_ref, seg_ref, level_ref, lbl_ref, hgn_ref, wgu_ref, bg_ref, ggn_ref,
                o_ref, st_ref, *, chunks):
    @pl.when(pl.program_id(1) == 0)
    def _():
        st_ref[...] = jnp.zeros_like(st_ref)

    seg = seg_ref[...]
    level = level_ref[...]
    logits = lbl_ref[...]
    ez = jnp.exp(logits - jnp.max(logits, axis=0, keepdims=True))
    lb = ez[0:1, :] / jnp.sum(ez, axis=0, keepdims=True)
    lane = lax.broadcasted_iota(jnp.int32, (CHUNK, LANES), 1)

    def chunk_body(ci, carry):
        rows = pl.ds(pl.multiple_of(ci * CHUNK, CHUNK), CHUNK)

        def col(off):
            return p_ref[rows, pl.ds(off, LANES)]

        for h in range(HG_HEADS):
            hq = col(OFF_HQ + h * LANES)
            q = hq * jax.nn.sigmoid(hq)
            lbh = lb[:, h * LANES:(h + 1) * LANES]
            forget = lbh + (1.0 - lbh) * jax.nn.sigmoid(col(OFF_HF + h * LANES))
            o, st = _gla_chunk(q, 1.0 - forget, jnp.log(forget), col(OFF_HI + h * LANES),
                               st_ref[h], seg, level)
            st_ref[h] = st
            o_ref[rows, pl.ds(h * HEAD_V, HEAD_V)] = _head_out(
                o, hgn_ref[...], col(OFF_HGATE + h * HEAD_V)).astype(o_ref.dtype)

        zg = jnp.dot(col(OFF_GLOW).astype(BF16), wgu_ref[...], preferred_element_type=F32) + bg_ref[...]
        log_g = (jnp.minimum(zg, 0.0) - jnp.log(1.0 + jnp.exp(-jnp.abs(zg)))) / GLA_GATE_NORMALIZER
        for h in range(GLA_HEADS):
            pair, half = h // 2, h % 2
            q = col(OFF_GQ + pair * LANES) * (GLA_HEAD_K ** -0.5)
            in_head = (lane >= half * GLA_HEAD_K) & (lane < (half + 1) * GLA_HEAD_K)
            k = jnp.where(in_head, col(OFF_GK + pair * LANES), 0.0)
            g = log_g[:, pair * LANES:(pair + 1) * LANES]
            o, st = _gla_chunk(q, k, g, col(OFF_GV + h * HEAD_V), st_ref[HG_HEADS + h], seg, level)
            st_ref[HG_HEADS + h] = st
            o_ref[rows, pl.ds((HG_HEADS + h) * HEAD_V, HEAD_V)] = _head_out(
                o, ggn_ref[...], col(OFF_GGATE + h * HEAD_V)).astype(o_ref.dtype)
        return carry

    lax.fori_loop(0, chunks, chunk_body, 0)


def _mixers(proj, batch, seq, lb_logits, hg_norm_g, wgu, bg, gla_norm_g, tt=512):
    n = batch * seq
    steps = seq // tt
    heads = HG_HEADS + GLA_HEADS
    const = lambda shape: pl.BlockSpec(shape, lambda b, t: (0,) * len(shape))
    return pl.pallas_call(
        functools.partial(_mixer_body, chunks=tt // CHUNK),
        grid=(batch, steps),
        in_specs=[pl.BlockSpec((tt, IN_WIDTH_PADDED), lambda b, t: (b * steps + t, 0)),
                  const(_SEG_MATS.shape), const(_LEVEL_MAP.shape),
                  const(lb_logits.shape), const((1, HEAD_V)),
                  const(wgu.shape), const(bg.shape), const((1, HEAD_V))],
        out_specs=pl.BlockSpec((tt, heads * HEAD_V), lambda b, t: (b * steps + t, 0)),
        out_shape=jax.ShapeDtypeStruct((n, heads * HEAD_V), BF16),
        scratch_shapes=[pltpu.VMEM((heads, HEAD_V, LANES), F32)],
        compiler_params=pltpu.CompilerParams(dimension_semantics=("parallel", "arbitrary"),
                                             vmem_limit_bytes=VMEM_LIMIT),
        name="mixers",
    )(proj, jnp.asarray(_SEG_MATS, BF16), jnp.asarray(_LEVEL_MAP), lb_logits, hg_norm_g,
      wgu, bg, gla_norm_g)


def _outproj_body(x_ref, m_ref, wo_ref, g_ref, wq_ref, h_ref, hn_ref, q_ref):
    h = x_ref[...] + jnp.dot(m_ref[...], wo_ref[...], preferred_element_type=F32)
    h_ref[...] = h
    hn = h * lax.rsqrt(jnp.mean(h * h, axis=-1, keepdims=True) + EPS) * g_ref[...]
    hn_ref[...] = hn
    q = jnp.dot(hn.astype(BF16), wq_ref[...], preferred_element_type=F32)
    for j in range(q_ref.shape[0]):
        q_ref[j] = q[:, j * PEER_HALF:(j + 1) * PEER_HALF]


def _outproj(x2, row0, mixed, wo, g2, wq, tm=256):
    n, d = mixed.shape[0], x2.shape[1]
    nq = wq.shape[1] // PEER_HALF
    assert row0 % tm == 0 and n % tm == 0
    return pl.pallas_call(
        _outproj_body,
        grid=(n // tm,),
        in_specs=[pl.BlockSpec((tm, d), lambda i: (i + row0 // tm, 0)),
                  pl.BlockSpec((tm, mixed.shape[1]), lambda i: (i, 0)),
                  pl.BlockSpec(wo.shape, lambda i: (0, 0)),
                  pl.BlockSpec((1, d), lambda i: (0, 0)),
                  pl.BlockSpec(wq.shape, lambda i: (0, 0))],
        out_specs=[pl.BlockSpec((tm, d), lambda i: (i, 0)),
                   pl.BlockSpec((tm, d), lambda i: (i, 0)),
                   pl.BlockSpec((nq, tm, PEER_HALF), lambda i: (0, i, 0))],
        out_shape=[jax.ShapeDtypeStruct((n, d), F32),
                   jax.ShapeDtypeStruct((n, d), F32),
                   jax.ShapeDtypeStruct((nq, n, PEER_HALF), F32)],
        compiler_params=pltpu.CompilerParams(dimension_semantics=("parallel",),
                                             vmem_limit_bytes=VMEM_LIMIT),
        name="outproj",
    )(x2, mixed, wo, g2, wq)


def _candidate_constants():
    k = PEER_TOPK
    pos, valid = [], []
    for j in range(k):
        pos.append(j); valid.append(True)
    for i in range(1, 8):
        for j in range(8):
            pos.append(i * k + j); valid.append((i + 1) * (j + 1) <= k)
    for i in range(8, k):
        pos.append(i * k); valid.append(True)
    return np.asarray(pos, np.float32), np.asarray(valid, bool)


_CAND_POS, _CAND_VALID = _candidate_constants()
_N_CAND = _CAND_POS.shape[0]


def _topk_rows(ss, pos, extras, k):
    t = ss[0].shape[1]
    slot = lax.broadcasted_iota(jnp.int32, (k, t), 0)

    def body(it, carry):
        out = []
        for (s, vals, poss, exts), extra in zip(carry, extras):
            m = jnp.max(s, axis=0, keepdims=True)
            p = jnp.min(jnp.where(s == m, pos, 1e9), axis=0, keepdims=True)
            hit = pos == p
            vals = jnp.where(slot == it, m, vals)
            poss = jnp.where(slot == it, p, poss)
            if extra is not None:
                x = jnp.max(jnp.where(hit, extra, -1.0), axis=0, keepdims=True)
                exts = jnp.where(slot == it, x, exts)
            out.append((jnp.where(hit, -jnp.inf, s), vals, poss, exts))
        return tuple(out)

    z = jnp.zeros((k, t), F32)
    res = lax.fori_loop(0, k, body, tuple((s, z, z, z) for s in ss))
    return [(vals, poss, exts) for _, vals, poss, exts in res]


def _route_body(q_ref, keys_ref, cpos_ref, cvalid_ref, idx_ref, gate_ref):
    k = PEER_TOPK
    n_tiles = q_ref.shape[1] // LANES
    key_pos = lax.broadcasted_iota(jnp.int32, (PEER_N_KEYS, LANES), 0).astype(F32)
    cpos = cpos_ref[...]
    cvalid = cvalid_ref[...] > 0.5
    scores = [_nt(keys_ref[p], q_ref[p, pl.ds(lt * LANES, LANES), :].astype(BF16))
              for lt in range(n_tiles) for p in range(2)]
    tops = _topk_rows(scores, key_pos, [None] * len(scores), k)
    cands, experts = [], []
    for lt in range(n_tiles):
        (v0, i0, _), (v1, i1, _) = tops[2 * lt], tops[2 * lt + 1]
        vals = [v0[0:1] + v1]
        exps = [i0[0:1] * PEER_N_KEYS + i1]
        for i in range(1, 8):
            vals.append(v0[i:i + 1] + v1[0:8])
            exps.append(i0[i:i + 1] * PEER_N_KEYS + i1[0:8])
        vals.append(v0[8:k] + v1[0:1])
        exps.append(i0[8:k] * PEER_N_KEYS + i1[0:1])
        cands.append(jnp.where(cvalid, jnp.concatenate(vals, axis=0), -jnp.inf))
        experts.append(jnp.concatenate(exps, axis=0))
    picks = _topk_rows(cands, cpos, experts, k)
    for lt in range(n_tiles):
        best, _, expert = picks[lt]
        cols = pl.ds(lt * LANES, LANES)
        e = jnp.exp(best - jnp.max(best, axis=0, keepdims=True))
        gate_ref[:, cols] = e / jnp.sum(e, axis=0, keepdims=True)
        idx_ref[:, cols] = expert.astype(jnp.int32)


def _route(q3, keys, tok0, n):
    heads = q3.shape[0] // 2
    tb = 512 if tok0 % 512 == 0 and n % 512 == 0 else SHARE_GRANULE
    assert tok0 % tb == 0 and n % tb == 0
    cpos = jnp.broadcast_to(jnp.asarray(_CAND_POS)[:, None], (_N_CAND, LANES))
    cvalid = jnp.broadcast_to(jnp.asarray(_CAND_VALID, F32)[:, None], (_N_CAND, LANES))
    return pl.pallas_call(
        _route_body,
        grid=(n // tb, heads),
        in_specs=[pl.BlockSpec((2, tb, PEER_HALF), lambda i, h: (h, i + tok0 // tb, 0)),
                  pl.BlockSpec((2, PEER_N_KEYS, PEER_HALF), lambda i, h: (h, 0, 0)),
                  pl.BlockSpec((_N_CAND, LANES), lambda i, h: (0, 0)),
                  pl.BlockSpec((_N_CAND, LANES), lambda i, h: (0, 0))],
        out_specs=[pl.BlockSpec((PEER_TOPK, tb), lambda i, h: (h, i)),
                   pl.BlockSpec((PEER_TOPK, tb), lambda i, h: (h, i))],
        out_shape=[jax.ShapeDtypeStruct((heads * PEER_TOPK, n), jnp.int32),
                   jax.ShapeDtypeStruct((heads * PEER_TOPK, n), F32)],
        compiler_params=pltpu.CompilerParams(dimension_semantics=("parallel", "parallel"),
                                             vmem_limit_bytes=VMEM_LIMIT),
        name="route",
    )(q3, keys, cpos, cvalid)


def _sc_pipeline(idx_v, table_hbm, rows, sems, n_units, units_per_token, compute):
    parts = rows[0].shape[0] // SC_LANES
    assert n_units % 2 == 0

    def start(unit, slot):
        tl = unit // units_per_token
        g = unit % units_per_token
        for part in range(parts):
            iv = idx_v[tl, pl.ds((g * parts + part) * SC_LANES, SC_LANES)]
            pltpu.async_copy(table_hbm.at[iv], rows[slot].at[pl.ds(part * SC_LANES, SC_LANES)], sems[slot])

    def wait(slot):
        iv = idx_v[0, pl.ds(0, SC_LANES)]
        for part in range(parts):
            pltpu.make_async_copy(table_hbm.at[iv], rows[slot].at[pl.ds(part * SC_LANES, SC_LANES)],
                                  sems[slot]).wait()

    start(0, 0)

    def pair(i, carry):
        u0 = 2 * i
        start(u0 + 1, 1)
        wait(0)
        compute(u0 // units_per_token, u0 % units_per_token, rows[0])

        @pl.when(u0 + 2 < n_units)
        def _():
            start(u0 + 2, 0)

        wait(1)
        compute((u0 + 1) // units_per_token, (u0 + 1) % units_per_token, rows[1])
        return carry

    lax.fori_loop(0, n_units // 2, pair, 0)


def _sc_batch(tokens_per_worker):
    for tb in (40, 32, 24, 16, 8):
        if tokens_per_worker % tb == 0:
            return tb
    raise ValueError(f"tokens per SparseCore worker ({tokens_per_worker}) must be a multiple of 8")


def _peer_act(hn, idx, u, unit_rows=2 * SC_LANES):
    n, d = hn.shape
    kk = idx.shape[1]
    tpw = n // SC_WORKERS
    tb = _sc_batch(tpw)
    upt = kk // unit_rows
    parts = unit_rows // SC_LANES
    nchunk = d // SC_LANES
    mesh = plsc.VectorSubcoreMesh(core_axis_name="c", subcore_axis_name="s")

    def body(x_hbm, idx_hbm, u_hbm, out_hbm, x_v, idx_v, act_v, rows0, rows1, tmp_v, sem0, sem1):
        wid = lax.axis_index("s") * 2 + lax.axis_index("c")
        lane_row = lax.iota(jnp.int32, SC_LANES) * SC_LANES

        def compute(tl, g, r_ref):
            for part in range(parts):
                row0 = part * SC_LANES

                def cbody(c, accs, row0=row0):
                    xv = x_v[tl, pl.ds(c * SC_LANES, SC_LANES)]
                    return tuple(accs[r] + r_ref[row0 + r, pl.ds(c * SC_LANES, SC_LANES)] * xv
                                 for r in range(SC_LANES))
                accs = lax.fori_loop(0, nchunk, cbody,
                                     tuple(jnp.zeros((SC_LANES,), F32) for _ in range(SC_LANES)))
                for r in range(SC_LANES):
                    tmp_v[pl.ds(r * SC_LANES, SC_LANES)] = accs[r]
                tot = plsc.load_gather(tmp_v, [lane_row])
                for j in range(1, SC_LANES):
                    tot = tot + plsc.load_gather(tmp_v, [lane_row + j])
                act_v[tl, pl.ds((g * parts + part) * SC_LANES, SC_LANES)] = tot

        def batch(b, carry):
            t0 = wid * tpw + b * tb
            pltpu.sync_copy(x_hbm.at[pl.ds(t0, tb)], x_v)
            pltpu.sync_copy(idx_hbm.at[pl.ds(t0, tb)], idx_v)
            _sc_pipeline(idx_v, u_hbm, (rows0, rows1), (sem0, sem1), tb * upt, upt, compute)
            pltpu.sync_copy(act_v, out_hbm.at[pl.ds(t0, tb)])
            return carry

        lax.fori_loop(0, tpw // tb, batch, 0)

    return pl.kernel(
        body, mesh=mesh,
        compiler_params=pltpu.CompilerParams(needs_layout_passes=False),
        out_type=jax.ShapeDtypeStruct((n, kk), F32),
        scratch_types=[pltpu.VMEM((tb, d), F32), pltpu.VMEM((tb, kk), jnp.int32),
                       pltpu.VMEM((tb, kk), F32),
                       pltpu.VMEM((unit_rows, d), F32), pltpu.VMEM((unit_rows, d), F32),
                       pltpu.VMEM((SC_LANES * SC_LANES,), F32),
                       pltpu.SemaphoreType.DMA, pltpu.SemaphoreType.DMA],
        name="peer_act",
    )(hn, idx, u)


def _peer_out(w, idx, v, cb=16, unit_rows=2 * SC_LANES):
    n, kk = w.shape
    d = v.shape[1]
    tpw = n // SC_WORKERS
    tb = _sc_batch(tpw)
    upt = kk // unit_rows
    nchunk = d // SC_LANES
    mesh = plsc.VectorSubcoreMesh(core_axis_name="c", subcore_axis_name="s")

    def body(w_hbm, idx_hbm, v_hbm, out_hbm, w_v, idx_v, out_v, rows0, rows1, sem0, sem1):
        wid = lax.axis_index("s") * 2 + lax.axis_index("c")
        zero = jnp.zeros((SC_LANES,), F32)

        def compute(tl, g, r_ref):
            tls = jnp.full((SC_LANES,), tl, jnp.int32)

            def cb_body(ci, carry):
                c0 = ci * (cb * SC_LANES)
                accs = tuple(out_v[tl, pl.ds(c0 + j * SC_LANES, SC_LANES)] for j in range(cb))

                def rbody(r, accs):
                    wr = plsc.load_gather(w_v, [tls, jnp.full((SC_LANES,), g * unit_rows + r, jnp.int32)])
                    return tuple(accs[j] + wr * r_ref[r, pl.ds(c0 + j * SC_LANES, SC_LANES)]
                                 for j in range(cb))

                accs = lax.fori_loop(0, unit_rows, rbody, accs)
                for j in range(cb):
                    out_v[tl, pl.ds(c0 + j * SC_LANES, SC_LANES)] = accs[j]
                return carry

            lax.fori_loop(0, nchunk // cb, cb_body, 0)

        def batch(b, carry):
            t0 = wid * tpw + b * tb
            pltpu.sync_copy(w_hbm.at[pl.ds(t0, tb)], w_v)
            pltpu.sync_copy(idx_hbm.at[pl.ds(t0, tb)], idx_v)

            def zbody(i, carry):
                out_v[i // nchunk, pl.ds((i % nchunk) * SC_LANES, SC_LANES)] = zero
                return carry

            lax.fori_loop(0, tb * nchunk, zbody, 0)
            _sc_pipeline(idx_v, v_hbm, (rows0, rows1), (sem0, sem1), tb * upt, upt, compute)
            pltpu.sync_copy(out_v, out_hbm.at[pl.ds(t0, tb)])
            return carry

        lax.fori_loop(0, tpw // tb, batch, 0)

    return pl.kernel(
        body, mesh=mesh,
        compiler_params=pltpu.CompilerParams(needs_layout_passes=False),
        out_type=jax.ShapeDtypeStruct((n, d), F32),
        scratch_types=[pltpu.VMEM((tb, kk), F32), pltpu.VMEM((tb, kk), jnp.int32),
                       pltpu.VMEM((tb, d), F32),
                       pltpu.VMEM((unit_rows, d), F32), pltpu.VMEM((unit_rows, d), F32),
                       pltpu.SemaphoreType.DMA, pltpu.SemaphoreType.DMA],
        name="peer_out",
    )(w, idx, v)


SUBLANES = 8
TC_PEER_TOKENS = 8
ISSUE_UNROLL = 8


def _gelu(a):
    return a * (lax.erf(a / np.sqrt(2.0).astype(np.float32)) + 1.0) / 2.0


def _peer_tc_body(idx_ref, idx_next_ref, x_ref, g_ref, u_hbm, v_hbm, o_ref, ubuf, vbuf, sem_u, sem_v):
    tokens = x_ref.shape[0]
    kk = g_ref.shape[1]
    n_rows = tokens * kk
    step = pl.program_id(0)
    slot = step % 2

    def row_copy(table, buf, sem, expert, j):
        src = table.at[pl.ds(pl.multiple_of(expert * SUBLANES, SUBLANES), SUBLANES)]
        return pltpu.make_async_copy(src, buf.at[pl.ds(j * SUBLANES, SUBLANES)], sem)

    def issue_block(ids_ref, into):
        def issue(i, carry):
            for r in range(ISSUE_UNROLL):
                j = i * ISSUE_UNROLL + r
                expert = ids_ref[j]
                row_copy(u_hbm, ubuf.at[into], sem_u.at[into], expert, j).start()
                row_copy(v_hbm, vbuf.at[into], sem_v.at[into], expert, j).start(priority=1)
            return carry

        lax.fori_loop(0, n_rows // ISSUE_UNROLL, issue, 0)

    @pl.when(step == 0)
    def _():
        issue_block(idx_ref, 0)

    @pl.when(step + 1 < pl.num_programs(0))
    def _():
        issue_block(idx_next_ref, 1 - slot)

    def wait_all(table, buf, sem):
        pltpu.make_async_copy(table.at[pl.ds(0, n_rows * SUBLANES)], buf.at[slot], sem.at[slot]).wait()

    def rows_of(buf, t, s):
        return buf[slot, pl.ds(t * kk * SUBLANES + s, kk, stride=SUBLANES), :]

    wait_all(u_hbm, ubuf, sem_u)
    acts = []
    for t in range(tokens):
        acc = rows_of(ubuf, t, 0) * x_ref[t:t + 1, 0:LANES]
        for s in range(1, SUBLANES):
            acc = acc + rows_of(ubuf, t, s) * x_ref[t:t + 1, s * LANES:(s + 1) * LANES]
        acts.append(jnp.sum(acc, axis=1, keepdims=True))
    w = g_ref[0] * _gelu(jnp.concatenate(acts, axis=1))

    wait_all(v_hbm, vbuf, sem_v)
    for t in range(tokens):
        wt = w[:, t:t + 1]
        for s in range(SUBLANES):
            o_ref[t:t + 1, s * LANES:(s + 1) * LANES] = jnp.sum(rows_of(vbuf, t, s) * wt, axis=0, keepdims=True)


def _peer_tc(hn, idx, gate_t, u_tiles, v_tiles):
    m, d = hn.shape
    kk = idx.shape[1]
    tb = TC_PEER_TOKENS
    assert d == SUBLANES * LANES and m % tb == 0
    steps = m // tb
    g3 = gate_t.reshape(kk, steps, tb).transpose(1, 0, 2)
    rows = tb * kk * SUBLANES
    ids = idx.reshape(m * kk)
    return pl.pallas_call(
        _peer_tc_body,
        grid=(steps,),
        in_specs=[pl.BlockSpec((tb * kk,), lambda i: (i,), memory_space=pltpu.SMEM),
                  pl.BlockSpec((tb * kk,), lambda i: (jnp.minimum(i + 1, steps - 1),), memory_space=pltpu.SMEM),
                  pl.BlockSpec((tb, d), lambda i: (i, 0)),
                  pl.BlockSpec((1, kk, tb), lambda i: (i, 0, 0)),
                  pl.BlockSpec(memory_space=pl.ANY),
                  pl.BlockSpec(memory_space=pl.ANY)],
        out_specs=pl.BlockSpec((tb, d), lambda i: (i, 0)),
        out_shape=jax.ShapeDtypeStruct((m, d), F32),
        scratch_shapes=[pltpu.VMEM((2, rows, LANES), F32), pltpu.VMEM((2, rows, LANES), F32),
                        pltpu.SemaphoreType.DMA((2,)), pltpu.SemaphoreType.DMA((2,))],
        compiler_params=pltpu.CompilerParams(dimension_semantics=("arbitrary",),
                                             vmem_limit_bytes=VMEM_LIMIT),
        name="peer_tc",
    )(ids, ids, hn, g3, u_tiles, v_tiles)


def _gelu_gate_body(a_ref, g_ref, o_ref):
    o_ref[...] = g_ref[...] * _gelu(a_ref[...])


def _gelu_gate(act, gate, tm=SHARE_GRANULE):
    n, kk = act.shape
    tm = min(tm, n)
    assert n % tm == 0
    spec = pl.BlockSpec((tm, kk), lambda i: (i, 0))
    return pl.pallas_call(
        _gelu_gate_body, grid=(n // tm,), in_specs=[spec, spec], out_specs=spec,
        out_shape=jax.ShapeDtypeStruct((n, kk), F32),
        compiler_params=pltpu.CompilerParams(dimension_semantics=("parallel",)),
        name="gelu_gate",
    )(act, gate)


def _final_body(h_ref, g_ref, *rest, starts, aliased):
    o_ref = rest[-1]
    parts = rest[:len(starts)]
    assert len(rest) == len(starts) + 1 + int(aliased)
    peer = parts[0][...]
    for start, p_ref in zip(starts[1:], parts[1:]):
        peer = jnp.where(pl.program_id(0) >= start, p_ref[...], peer)
    h = h_ref[...] + peer
    o_ref[...] = h * lax.rsqrt(jnp.mean(h * h, axis=-1, keepdims=True) + EPS) * g_ref[...]


def _final(h1, peer_parts, g, out, row0, n_total, tm=256):
    n, d = h1.shape
    sizes = [p.shape[0] for p in peer_parts]
    assert sum(sizes) == n and all(m > 0 and m % tm == 0 for m in sizes) and row0 % tm == 0
    starts = tuple(int(v) // tm for v in np.cumsum([0] + sizes[:-1]))

    def part_spec(start, size):
        return pl.BlockSpec((tm, d), lambda i: (jnp.clip(i - start, 0, size // tm - 1), 0))

    in_specs = [pl.BlockSpec((tm, d), lambda i: (i, 0)), pl.BlockSpec((1, d), lambda i: (0, 0))]
    in_specs += [part_spec(s, m) for s, m in zip(starts, sizes)]
    args = [h1, g, *peer_parts]
    aliases = {}
    if out is not None:
        in_specs.append(pl.BlockSpec(memory_space=pl.ANY))
        aliases = {len(args): 0}
        args.append(out)
    return pl.pallas_call(
        functools.partial(_final_body, starts=starts, aliased=out is not None), grid=(n // tm,),
        in_specs=in_specs,
        out_specs=pl.BlockSpec((tm, d), lambda i: (i + row0 // tm, 0)),
        out_shape=jax.ShapeDtypeStruct((n_total, d), F32),
        input_output_aliases=aliases,
        compiler_params=pltpu.CompilerParams(dimension_semantics=("parallel",)),
        name="final_norm",
    )(*args)


def _tc_share(n_tokens):
    m = int(n_tokens * TC_SHARE) // SHARE_GRANULE * SHARE_GRANULE
    return min(max(m, SHARE_GRANULE), n_tokens - SHARE_GRANULE)


def kernel(x, norm1_g, w_in, hg_lower_logits, hg_norm_g, gla_w_gate_up, gla_b_gate, gla_norm_g,
           w_out, norm2_g, peer_w_q, peer_sub_keys, peer_u, peer_v, norm_f_g):
    batch, seq, d = x.shape
    depth = w_in.shape[0]
    assert depth == 1, "single-layer block"
    layer = 0

    w = w_in[layer]
    pad = LANES - GLA_GATE_RANK
    assert w.shape == (d, IN_WIDTH_PADDED - pad) and d == SUBLANES * LANES
    w_r = jnp.concatenate(
        [w[:, :OFF_GLOW + GLA_GATE_RANK], jnp.zeros((d, pad), w.dtype),
         w[:, OFF_GLOW + GLA_GATE_RANK:]], axis=1).astype(BF16)
    wgu = jnp.concatenate([gla_w_gate_up[layer], jnp.zeros((pad, GLA_KEY), F32)], axis=0).astype(BF16)

    wo = w_out[layer].astype(BF16)
    wq = peer_w_q[layer].astype(BF16)
    keys = peer_sub_keys[layer].reshape(PEER_HEADS * 2, PEER_N_KEYS, PEER_HALF).astype(BF16)
    n_experts = peer_u.shape[1]
    u_tiles = peer_u[layer].reshape(n_experts * SUBLANES, LANES)
    v_tiles = peer_v[layer].reshape(n_experts * SUBLANES, LANES)

    groups = PIPELINE_GROUPS if batch % PIPELINE_GROUPS == 0 else 1
    bpg = batch // groups
    ng = bpg * seq
    m_tc = _tc_share(ng)
    gf = norm_f_g.reshape(1, d)
    x2 = x.reshape(batch * seq, d)

    def after(value, anchor):
        value, _ = lax.optimization_barrier((value, anchor))
        return value

    half = ng // 2
    h1s, tc_ins, sc_units = [], [], []
    for gi in range(groups):
        proj = _inproj(x2, gi * ng, ng, norm1_g[layer].reshape(1, d), w_r)
        mixed = _mixers(proj, bpg, seq, hg_lower_logits, hg_norm_g[layer].reshape(1, HEAD_V), wgu,
                        gla_b_gate[layer].reshape(1, GLA_KEY), gla_norm_g[layer].reshape(1, HEAD_V))
        h1, hn, q3 = _outproj(x2, gi * ng, mixed, wo, norm2_g[layer].reshape(1, d), wq)
        h1s.append(h1)
        m_g = m_tc
        bounds = (m_g, half, ng) if gi == 0 and m_g < half else (m_g, ng)
        units = []
        for lo, hi in zip(bounds[:-1], bounds[1:]):
            idx_t, gate_t = _route(q3, keys, lo, hi - lo)
            idx = idx_t.T
            units.append(dict(idx=idx, gate=gate_t.T, act=_peer_act(hn[lo:hi], idx, peer_u[layer])))
        sc_units.append(units)
        idx_t, gate_t = _route(q3, keys, 0, m_g)
        tc_ins.append((hn[:m_g], idx_t.T, gate_t))

    def sc_out(gi, anchor):
        outs = []
        for unit in sc_units[gi]:
            wts = _gelu_gate(after(unit["act"], anchor), unit["gate"])
            outs.append(_peer_out(wts, unit["idx"], peer_v[layer]))
        return wts, outs

    rows, peer_sc = [None] * groups, [None] * groups
    rows[0] = _peer_tc(*after(tc_ins[0], sc_units[-1][-1]["idx"]), u_tiles, v_tiles)
    anchor = rows[0]
    for gi in range(groups - 1):
        anchor, peer_sc[gi] = sc_out(gi, rows[0])
    if groups > 1:
        rows[1] = _peer_tc(*after(tc_ins[1], anchor), u_tiles, v_tiles)
        anchor = rows[1]
    anchor, peer_sc[groups - 1] = sc_out(groups - 1, anchor)
    out, finished = None, 0

    def finish(gi, out):
        return _final(h1s[gi], [rows[gi], *peer_sc[gi]], gf, out, gi * ng, batch * seq)

    for gi in range(2, groups):
        out = finish(gi - 2, out)
        finished = gi - 1
        rows[gi] = _peer_tc(*after(tc_ins[gi], (anchor, out)), u_tiles, v_tiles)
        anchor = rows[gi]
    for gi in range(finished, groups):
        out = finish(gi, out)
    return out.reshape(batch, seq, d)
```

```python
import functools

import jax
import jax.numpy as jnp
import numpy as np
from jax import lax
from jax.experimental import pallas as pl
from jax.experimental.pallas import tpu as pltpu
from jax.experimental.pallas import tpu_sc as plsc

F32 = jnp.float32
BF16 = jnp.bfloat16
EPS = 1e-6

HG_HEADS = 4
GLA_HEADS = 4
HEAD_V = 128
HG_KEY = 512
GLA_KEY = 256
GLA_HEAD_K = 64
GLA_GATE_RANK = 16
GLA_GATE_NORMALIZER = 16.0
CHUNK = 64
LANES = 128
PEER_HEADS = 8
PEER_N_KEYS = 128
PEER_HALF = 128
PEER_TOPK = 16

OFF_HQ, OFF_HF, OFF_HI, OFF_HGATE = 0, 512, 1024, 1536
OFF_GQ, OFF_GK, OFF_GV, OFF_GLOW, OFF_GGATE = 2048, 2304, 2560, 3072, 3200
IN_WIDTH_PADDED = 3712

VMEM_LIMIT = 48 * 1024 * 1024

SC_LANES = 16
SC_WORKERS = 32
PIPELINE_GROUPS = 4
TC_SHARE = 0.3125
SHARE_GRANULE = 256
LAST_GROUP_EXTRA_GRANULES = 1


def _nt(a, b):
    return lax.dot_general(a, b, (((1,), (1,)), ((), ())), preferred_element_type=F32)


def _tn(a, b):
    return lax.dot_general(a, b, (((0,), (0,)), ((), ())), preferred_element_type=F32)


def _inproj_body(x_ref, g_ref, w_ref, o_ref):
    x = x_ref[...]
    xn = x * lax.rsqrt(jnp.mean(x * x, axis=-1, keepdims=True) + EPS) * g_ref[...]
    o_ref[...] = jnp.dot(xn.astype(BF16), w_ref[...], preferred_element_type=F32)


def _inproj(x2, row0, n, g, w, tm=256):
    d = x2.shape[1]
    wd = w.shape[1]
    assert row0 % tm == 0 and n % tm == 0
    return pl.pallas_call(
        _inproj_body,
        grid=(n // tm,),
        in_specs=[pl.BlockSpec((tm, d), lambda i: (i + row0 // tm, 0)),
                  pl.BlockSpec((1, d), lambda i: (0, 0)),
                  pl.BlockSpec((d, wd), lambda i: (0, 0))],
        out_specs=pl.BlockSpec((tm, wd), lambda i: (i, 0)),
        out_shape=jax.ShapeDtypeStruct((n, wd), F32),
        compiler_params=pltpu.CompilerParams(dimension_semantics=("parallel",),
                                             vmem_limit_bytes=VMEM_LIMIT),
        name="inproj",
    )(x2, g, w)


def _level_constants():
    c = CHUNK
    mats = [np.tril(np.ones((c, c), np.float32))]
    level = np.full((c, c), -1, np.int32)
    b, lvl = c // 2, 0
    while b >= 1:
        m = np.zeros((c, c), np.float32)
        for s in range(0, c, 2 * b):
            mid = s + b
            for i in range(mid, s + 2 * b):
                m[i, mid:i + 1] = 1.0
                level[i, s:mid] = lvl
            for j in range(s, mid):
                m[j, j + 1:mid] = 1.0
        mats.append(m)
        b //= 2
        lvl += 1
    level[np.arange(c), np.arange(c)] = lvl
    return np.concatenate(mats, axis=0), level, lvl


_SEG_MATS, _LEVEL_MAP, _N_LEVELS = _level_constants()


def _split3(a):
    hi = a.astype(BF16)
    r = a - hi.astype(F32)
    mid = r.astype(BF16)
    lo = (r - mid.astype(F32)).astype(BF16)
    return jnp.concatenate([hi, mid, lo], axis=1)


def _gla_chunk(q, k, g, v, st, seg, level):
    c = CHUNK
    ex3 = jnp.dot(seg, _split3(g), preferred_element_type=F32)
    ex = ex3[:, 0:LANES] + ex3[:, LANES:2 * LANES] + ex3[:, 2 * LANES:3 * LANES]
    cum = ex[0:c]
    scores = jnp.where(level == _N_LEVELS, _nt(q.astype(BF16), k.astype(BF16)), 0.0)
    for l in range(_N_LEVELS):
        e = jnp.exp(ex[c * (l + 1):c * (l + 2)])
        p = _nt((q * e).astype(BF16), (k * e).astype(BF16))
        scores = jnp.where(level == l, p, scores)
    last = cum[c - 1:c, :]
    qd = (q * jnp.exp(cum)).astype(BF16)
    kd = (k * jnp.exp(last - cum)).astype(BF16)
    vb = v.astype(BF16)
    o = jnp.dot(scores.astype(BF16), vb, preferred_element_type=F32) + _nt(qd, st.astype(BF16))
    st_new = st * jnp.exp(last) + _tn(vb, kd)
    return o, st_new


def _head_out(o, gain, gate):
    on = o * lax.rsqrt(jnp.mean(o * o, axis=-1, keepdims=True) + EPS) * gain
    return on * (gate * jax.nn.sigmoid(gate))


def _mixer_body(p_ref, seg_ref, level_ref, lbl_ref, hgn_ref, wgu_ref, bg_ref, ggn_ref,
                o_ref, st_ref, *, chunks):
    @pl.when(pl.program_id(1) == 0)
    def _():
        st_ref[...] = jnp.zeros_like(st_ref)

    seg = seg_ref[...]
    level = level_ref[...]
    logits = lbl_ref[...]
    ez = jnp.exp(logits - jnp.max(logits, axis=0, keepdims=True))
    lb = ez[0:1, :] / jnp.sum(ez, axis=0, keepdims=True)
    lane = lax.broadcasted_iota(jnp.int32, (CHUNK, LANES), 1)

    def chunk_body(ci, carry):
        rows = pl.ds(pl.multiple_of(ci * CHUNK, CHUNK), CHUNK)

        def col(off):
            return p_ref[rows, pl.ds(off, LANES)]

        for h in range(HG_HEADS):
            hq = col(OFF_HQ + h * LANES)
            q = hq * jax.nn.sigmoid(hq)
            lbh = lb[:, h * LANES:(h + 1) * LANES]
            forget = lbh + (1.0 - lbh) * jax.nn.sigmoid(col(OFF_HF + h * LANES))
            o, st = _gla_chunk(q, 1.0 - forget, jnp.log(forget), col(OFF_HI + h * LANES),
                               st_ref[h], seg, level)
            st_ref[h] = st
            o_ref[rows, pl.ds(h * HEAD_V, HEAD_V)] = _head_out(
                o, hgn_ref[...], col(OFF_HGATE + h * HEAD_V)).astype(o_ref.dtype)

        zg = jnp.dot(col(OFF_GLOW).astype(BF16), wgu_ref[...], preferred_element_type=F32) + bg_ref[...]
        log_g = (jnp.minimum(zg, 0.0) - jnp.log(1.0 + jnp.exp(-jnp.abs(zg)))) / GLA_GATE_NORMALIZER
        for h in range(GLA_HEADS):
            pair, half = h // 2, h % 2
            q = col(OFF_GQ + pair * LANES) * (GLA_HEAD_K ** -0.5)
            in_head = (lane >= half * GLA_HEAD_K) & (lane < (half + 1) * GLA_HEAD_K)
            k = jnp.where(in_head, col(OFF_GK + pair * LANES), 0.0)
            g = log_g[:, pair * LANES:(pair + 1) * LANES]
            o, st = _gla_chunk(q, k, g, col(OFF_GV + h * HEAD_V), st_ref[HG_HEADS + h], seg, level)
            st_ref[HG_HEADS + h] = st
            o_ref[rows, pl.ds((HG_HEADS + h) * HEAD_V, HEAD_V)] = _head_out(
                o, ggn_ref[...], col(OFF_GGATE + h * HEAD_V)).astype(o_ref.dtype)
        return carry

    lax.fori_loop(0, chunks, chunk_body, 0)


def _mixers(proj, batch, seq, lb_logits, hg_norm_g, wgu, bg, gla_norm_g, tt=512):
    n = batch * seq
    steps = seq // tt
    heads = HG_HEADS + GLA_HEADS
    const = lambda shape: pl.BlockSpec(shape, lambda b, t: (0,) * len(shape))
    return pl.pallas_call(
        functools.partial(_mixer_body, chunks=tt // CHUNK),
        grid=(batch, steps),
        in_specs=[pl.BlockSpec((tt, IN_WIDTH_PADDED), lambda b, t: (b * steps + t, 0)),
                  const(_SEG_MATS.shape), const(_LEVEL_MAP.shape),
                  const(lb_logits.shape), const((1, HEAD_V)),
                  const(wgu.shape), const(bg.shape), const((1, HEAD_V))],
        out_specs=pl.BlockSpec((tt, heads * HEAD_V), lambda b, t: (b * steps + t, 0)),
        out_shape=jax.ShapeDtypeStruct((n, heads * HEAD_V), BF16),
        scratch_shapes=[pltpu.VMEM((heads, HEAD_V, LANES), F32)],
        compiler_params=pltpu.CompilerParams(dimension_semantics=("parallel", "arbitrary"),
                                             vmem_limit_bytes=VMEM_LIMIT),
        name="mixers",
    )(proj, jnp.asarray(_SEG_MATS, BF16), jnp.asarray(_LEVEL_MAP), lb_logits, hg_norm_g,
      wgu, bg, gla_norm_g)


def _outproj_body(x_ref, m_ref, wo_ref, g_ref, wq_ref, h_ref, hn_ref, q_ref):
    h = x_ref[...] + jnp.dot(m_ref[...], wo_ref[...], preferred_element_type=F32)
    h_ref[...] = h
    hn = h * lax.rsqrt(jnp.mean(h * h, axis=-1, keepdims=True) + EPS) * g_ref[...]
    hn_ref[...] = hn
    q = jnp.dot(hn.astype(BF16), wq_ref[...], preferred_element_type=F32)
    for j in range(q_ref.shape[0]):
        q_ref[j] = q[:, j * PEER_HALF:(j + 1) * PEER_HALF]


def _outproj(x2, row0, mixed, wo, g2, wq, tm=256):
    n, d = mixed.shape[0], x2.shape[1]
    nq = wq.shape[1] // PEER_HALF
    assert row0 % tm == 0 and n % tm == 0
    return pl.pallas_call(
        _outproj_body,
        grid=(n // tm,),
        in_specs=[pl.BlockSpec((tm, d), lambda i: (i + row0 // tm, 0)),
                  pl.BlockSpec((tm, mixed.shape[1]), lambda i: (i, 0)),
                  pl.BlockSpec(wo.shape, lambda i: (0, 0)),
                  pl.BlockSpec((1, d), lambda i: (0, 0)),
                  pl.BlockSpec(wq.shape, lambda i: (0, 0))],
        out_specs=[pl.BlockSpec((tm, d), lambda i: (i, 0)),
                   pl.BlockSpec((tm, d), lambda i: (i, 0)),
                   pl.BlockSpec((nq, tm, PEER_HALF), lambda i: (0, i, 0))],
        out_shape=[jax.ShapeDtypeStruct((n, d), F32),
                   jax.ShapeDtypeStruct((n, d), F32),
                   jax.ShapeDtypeStruct((nq, n, PEER_HALF), F32)],
        compiler_params=pltpu.CompilerParams(dimension_semantics=("parallel",),
                                             vmem_limit_bytes=VMEM_LIMIT),
        name="outproj",
    )(x2, mixed, wo, g2, wq)


def _candidate_constants():
    k = PEER_TOPK
    pos, valid = [], []
    for j in range(k):
        pos.append(j); valid.append(True)
    for i in range(1, 8):
        for j in range(8):
            pos.append(i * k + j); valid.append((i + 1) * (j + 1) <= k)
    for i in range(8, k):
        pos.append(i * k); valid.append(True)
    return np.asarray(pos, np.float32), np.asarray(valid, bool)


_CAND_POS, _CAND_VALID = _candidate_constants()
_N_CAND = _CAND_POS.shape[0]


def _topk_rows(ss, pos, extras, k):
    t = ss[0].shape[1]
    slot = lax.broadcasted_iota(jnp.int32, (k, t), 0)

    def body(it, carry):
        out = []
        for (s, vals, poss, exts), extra in zip(carry, extras):
            m = jnp.max(s, axis=0, keepdims=True)
            p = jnp.min(jnp.where(s == m, pos, 1e9), axis=0, keepdims=True)
            hit = pos == p
            vals = jnp.where(slot == it, m, vals)
            poss = jnp.where(slot == it, p, poss)
            if extra is not None:
                x = jnp.max(jnp.where(hit, extra, -1.0), axis=0, keepdims=True)
                exts = jnp.where(slot == it, x, exts)
            out.append((jnp.where(hit, -jnp.inf, s), vals, poss, exts))
        return tuple(out)

    z = jnp.zeros((k, t), F32)
    res = lax.fori_loop(0, k, body, tuple((s, z, z, z) for s in ss))
    return [(vals, poss, exts) for _, vals, poss, exts in res]


def _route_body(q_ref, keys_ref, cpos_ref, cvalid_ref, idx_ref, gate_ref):
    k = PEER_TOPK
    n_tiles = q_ref.shape[1] // LANES
    key_pos = lax.broadcasted_iota(jnp.int32, (PEER_N_KEYS, LANES), 0).astype(F32)
    cpos = cpos_ref[...]
    cvalid = cvalid_ref[...] > 0.5
    scores = [_nt(keys_ref[p], q_ref[p, pl.ds(lt * LANES, LANES), :].astype(BF16))
              for lt in range(n_tiles) for p in range(2)]
    tops = _topk_rows(scores, key_pos, [None] * len(scores), k)
    cands, experts = [], []
    for lt in range(n_tiles):
        (v0, i0, _), (v1, i1, _) = tops[2 * lt], tops[2 * lt + 1]
        vals = [v0[0:1] + v1]
        exps = [i0[0:1] * PEER_N_KEYS + i1]
        for i in range(1, 8):
            vals.append(v0[i:i + 1] + v1[0:8])
            exps.append(i0[i:i + 1] * PEER_N_KEYS + i1[0:8])
        vals.append(v0[8:k] + v1[0:1])
        exps.append(i0[8:k] * PEER_N_KEYS + i1[0:1])
        cands.append(jnp.where(cvalid, jnp.concatenate(vals, axis=0), -jnp.inf))
        experts.append(jnp.concatenate(exps, axis=0))
    picks = _topk_rows(cands, cpos, experts, k)
    for lt in range(n_tiles):
        best, _, expert = picks[lt]
        cols = pl.ds(lt * LANES, LANES)
        e = jnp.exp(best - jnp.max(best, axis=0, keepdims=True))
        gate_ref[:, cols] = e / jnp.sum(e, axis=0, keepdims=True)
        idx_ref[:, cols] = expert.astype(jnp.int32)


def _route(q3, keys, tok0, n):
    heads = q3.shape[0] // 2
    tb = 512 if tok0 % 512 == 0 and n % 512 == 0 else SHARE_GRANULE
    assert tok0 % tb == 0 and n % tb == 0
    cpos = jnp.broadcast_to(jnp.asarray(_CAND_POS)[:, None], (_N_CAND, LANES))
    cvalid = jnp.broadcast_to(jnp.asarray(_CAND_VALID, F32)[:, None], (_N_CAND, LANES))
    return pl.pallas_call(
        _route_body,
        grid=(n // tb, heads),
        in_specs=[pl.BlockSpec((2, tb, PEER_HALF), lambda i, h: (h, i + tok0 // tb, 0)),
                  pl.BlockSpec((2, PEER_N_KEYS, PEER_HALF), lambda i, h: (h, 0, 0)),
                  pl.BlockSpec((_N_CAND, LANES), lambda i, h: (0, 0)),
                  pl.BlockSpec((_N_CAND, LANES), lambda i, h: (0, 0))],
        out_specs=[pl.BlockSpec((PEER_TOPK, tb), lambda i, h: (h, i)),
                   pl.BlockSpec((PEER_TOPK, tb), lambda i, h: (h, i))],
        out_shape=[jax.ShapeDtypeStruct((heads * PEER_TOPK, n), jnp.int32),
                   jax.ShapeDtypeStruct((heads * PEER_TOPK, n), F32)],
        compiler_params=pltpu.CompilerParams(dimension_semantics=("parallel", "parallel"),
                                             vmem_limit_bytes=VMEM_LIMIT),
        name="route",
    )(q3, keys, cpos, cvalid)


def _sc_pipeline(idx_v, table_hbm, rows, sems, n_units, units_per_token, compute):
    parts = rows[0].shape[0] // SC_LANES
    assert n_units % 2 == 0

    def start(unit, slot):
        tl = unit // units_per_token
        g = unit % units_per_token
        for part in range(parts):
            iv = idx_v[tl, pl.ds((g * parts + part) * SC_LANES, SC_LANES)]
            pltpu.async_copy(table_hbm.at[iv], rows[slot].at[pl.ds(part * SC_LANES, SC_LANES)], sems[slot])

    def wait(slot):
        iv = idx_v[0, pl.ds(0, SC_LANES)]
        for part in range(parts):
            pltpu.make_async_copy(table_hbm.at[iv], rows[slot].at[pl.ds(part * SC_LANES, SC_LANES)],
                                  sems[slot]).wait()

    start(0, 0)

    def pair(i, carry):
        u0 = 2 * i
        start(u0 + 1, 1)
        wait(0)
        compute(u0 // units_per_token, u0 % units_per_token, rows[0])

        @pl.when(u0 + 2 < n_units)
        def _():
            start(u0 + 2, 0)

        wait(1)
        compute((u0 + 1) // units_per_token, (u0 + 1) % units_per_token, rows[1])
        return carry

    lax.fori_loop(0, n_units // 2, pair, 0)


def _sc_batch(tokens_per_worker):
    for tb in (40, 32, 24, 16, 8):
        if tokens_per_worker % tb == 0:
            return tb
    raise ValueError(f"tokens per SparseCore worker ({tokens_per_worker}) must be a multiple of 8")


def _peer_act(hn, idx, u, unit_rows=2 * SC_LANES):
    n, d = hn.shape
    kk = idx.shape[1]
    tpw = n // SC_WORKERS
    tb = _sc_batch(tpw)
    upt = kk // unit_rows
    parts = unit_rows // SC_LANES
    nchunk = d // SC_LANES
    mesh = plsc.VectorSubcoreMesh(core_axis_name="c", subcore_axis_name="s")

    def body(x_hbm, idx_hbm, u_hbm, out_hbm, x_v, idx_v, act_v, rows0, rows1, tmp_v, sem0, sem1):
        wid = lax.axis_index("s") * 2 + lax.axis_index("c")
        lane_row = lax.iota(jnp.int32, SC_LANES) * SC_LANES

        def compute(tl, g, r_ref):
            for part in range(parts):
                row0 = part * SC_LANES

                def cbody(c, accs, row0=row0):
                    xv = x_v[tl, pl.ds(c * SC_LANES, SC_LANES)]
                    return tuple(accs[r] + r_ref[row0 + r, pl.ds(c * SC_LANES, SC_LANES)] * xv
                                 for r in range(SC_LANES))
                accs = lax.fori_loop(0, nchunk, cbody,
                                     tuple(jnp.zeros((SC_LANES,), F32) for _ in range(SC_LANES)))
                for r in range(SC_LANES):
                    tmp_v[pl.ds(r * SC_LANES, SC_LANES)] = accs[r]
                tot = plsc.load_gather(tmp_v, [lane_row])
                for j in range(1, SC_LANES):
                    tot = tot + plsc.load_gather(tmp_v, [lane_row + j])
                act_v[tl, pl.ds((g * parts + part) * SC_LANES, SC_LANES)] = tot

        def batch(b, carry):
            t0 = wid * tpw + b * tb
            pltpu.sync_copy(x_hbm.at[pl.ds(t0, tb)], x_v)
            pltpu.sync_copy(idx_hbm.at[pl.ds(t0, tb)], idx_v)
            _sc_pipeline(idx_v, u_hbm, (rows0, rows1), (sem0, sem1), tb * upt, upt, compute)
            pltpu.sync_copy(act_v, out_hbm.at[pl.ds(t0, tb)])
            return carry

        lax.fori_loop(0, tpw // tb, batch, 0)

    return pl.kernel(
        body, mesh=mesh,
        compiler_params=pltpu.CompilerParams(needs_layout_passes=False),
        out_type=jax.ShapeDtypeStruct((n, kk), F32),
        scratch_types=[pltpu.VMEM((tb, d), F32), pltpu.VMEM((tb, kk), jnp.int32),
                       pltpu.VMEM((tb, kk), F32),
                       pltpu.VMEM((unit_rows, d), F32), pltpu.VMEM((unit_rows, d), F32),
                       pltpu.VMEM((SC_LANES * SC_LANES,), F32),
                       pltpu.SemaphoreType.DMA, pltpu.SemaphoreType.DMA],
        name="peer_act",
    )(hn, idx, u)


def _peer_out(w, idx, v, cb=16, unit_rows=2 * SC_LANES):
    n, kk = w.shape
    d = v.shape[1]
    tpw = n // SC_WORKERS
    tb = _sc_batch(tpw)
    upt = kk // unit_rows
    nchunk = d // SC_LANES
    mesh = plsc.VectorSubcoreMesh(core_axis_name="c", subcore_axis_name="s")

    def body(w_hbm, idx_hbm, v_hbm, out_hbm, w_v, idx_v, out_v, rows0, rows1, sem0, sem1):
        wid = lax.axis_index("s") * 2 + lax.axis_index("c")
        zero = jnp.zeros((SC_LANES,), F32)

        def compute(tl, g, r_ref):
            tls = jnp.full((SC_LANES,), tl, jnp.int32)

            def cb_body(ci, carry):
                c0 = ci * (cb * SC_LANES)
                accs = tuple(out_v[tl, pl.ds(c0 + j * SC_LANES, SC_LANES)] for j in range(cb))

                def rbody(r, accs):
                    wr = plsc.load_gather(w_v, [tls, jnp.full((SC_LANES,), g * unit_rows + r, jnp.int32)])
                    return tuple(accs[j] + wr * r_ref[r, pl.ds(c0 + j * SC_LANES, SC_LANES)]
                                 for j in range(cb))

                accs = lax.fori_loop(0, unit_rows, rbody, accs)
                for j in range(cb):
                    out_v[tl, pl.ds(c0 + j * SC_LANES, SC_LANES)] = accs[j]
                return carry

            lax.fori_loop(0, nchunk // cb, cb_body, 0)

        def batch(b, carry):
            t0 = wid * tpw + b * tb
            pltpu.sync_copy(w_hbm.at[pl.ds(t0, tb)], w_v)
            pltpu.sync_copy(idx_hbm.at[pl.ds(t0, tb)], idx_v)

            def zbody(i, carry):
                out_v[i // nchunk, pl.ds((i % nchunk) * SC_LANES, SC_LANES)] = zero
                return carry

            lax.fori_loop(0, tb * nchunk, zbody, 0)
            _sc_pipeline(idx_v, v_hbm, (rows0, rows1), (sem0, sem1), tb * upt, upt, compute)
            pltpu.sync_copy(out_v, out_hbm.at[pl.ds(t0, tb)])
            return carry

        lax.fori_loop(0, tpw // tb, batch, 0)

    return pl.kernel(
        body, mesh=mesh,
        compiler_params=pltpu.CompilerParams(needs_layout_passes=False),
        out_type=jax.ShapeDtypeStruct((n, d), F32),
        scratch_types=[pltpu.VMEM((tb, kk), F32), pltpu.VMEM((tb, kk), jnp.int32),
                       pltpu.VMEM((tb, d), F32),
                       pltpu.VMEM((unit_rows, d), F32), pltpu.VMEM((unit_rows, d), F32),
                       pltpu.SemaphoreType.DMA, pltpu.SemaphoreType.DMA],
        name="peer_out",
    )(w, idx, v)


SUBLANES = 8
TC_PEER_TOKENS = 8
ISSUE_UNROLL = 8


def _gelu(a):
    return a * (lax.erf(a / np.sqrt(2.0).astype(np.float32)) + 1.0) / 2.0


def _peer_tc_body(idx_ref, idx_next_ref, x_ref, g_ref, u_hbm, v_hbm, o_ref, ubuf, vbuf, sem_u, sem_v):
    tokens = x_ref.shape[0]
    kk = g_ref.shape[1]
    n_rows = tokens * kk
    step = pl.program_id(0)
    slot = step % 2

    def row_copy(table, buf, sem, expert, j):
        src = table.at[pl.ds(pl.multiple_of(expert * SUBLANES, SUBLANES), SUBLANES)]
        return pltpu.make_async_copy(src, buf.at[pl.ds(j * SUBLANES, SUBLANES)], sem)

    def issue_block(ids_ref, into):
        def issue(i, carry):
            for r in range(ISSUE_UNROLL):
                j = i * ISSUE_UNROLL + r
                expert = ids_ref[j]
                row_copy(u_hbm, ubuf.at[into], sem_u.at[into], expert, j).start()
                row_copy(v_hbm, vbuf.at[into], sem_v.at[into], expert, j).start(priority=1)
            return carry

        lax.fori_loop(0, n_rows // ISSUE_UNROLL, issue, 0)

    @pl.when(step == 0)
    def _():
        issue_block(idx_ref, 0)

    @pl.when(step + 1 < pl.num_programs(0))
    def _():
        issue_block(idx_next_ref, 1 - slot)

    def wait_all(table, buf, sem):
        pltpu.make_async_copy(table.at[pl.ds(0, n_rows * SUBLANES)], buf.at[slot], sem.at[slot]).wait()

    def rows_of(buf, t, s):
        return buf[slot, pl.ds(t * kk * SUBLANES + s, kk, stride=SUBLANES), :]

    wait_all(u_hbm, ubuf, sem_u)
    acts = []
    for t in range(tokens):
        acc = rows_of(ubuf, t, 0) * x_ref[t:t + 1, 0:LANES]
        for s in range(1, SUBLANES):
            acc = acc + rows_of(ubuf, t, s) * x_ref[t:t + 1, s * LANES:(s + 1) * LANES]
        acts.append(jnp.sum(acc, axis=1, keepdims=True))
    w = g_ref[0] * _gelu(jnp.concatenate(acts, axis=1))

    wait_all(v_hbm, vbuf, sem_v)
    for t in range(tokens):
        wt = w[:, t:t + 1]
        for s in range(SUBLANES):
            o_ref[t:t + 1, s * LANES:(s + 1) * LANES] = jnp.sum(rows_of(vbuf, t, s) * wt, axis=0, keepdims=True)


def _peer_tc(hn, idx, gate_t, u_tiles, v_tiles):
    m, d = hn.shape
    kk = idx.shape[1]
    tb = TC_PEER_TOKENS
    assert d == SUBLANES * LANES and m % tb == 0
    steps = m // tb
    g3 = gate_t.reshape(kk, steps, tb).transpose(1, 0, 2)
    rows = tb * kk * SUBLANES
    ids = idx.reshape(m * kk)
    return pl.pallas_call(
        _peer_tc_body,
        grid=(steps,),
        in_specs=[pl.BlockSpec((tb * kk,), lambda i: (i,), memory_space=pltpu.SMEM),
                  pl.BlockSpec((tb * kk,), lambda i: (jnp.minimum(i + 1, steps - 1),), memory_space=pltpu.SMEM),
                  pl.BlockSpec((tb, d), lambda i: (i, 0)),
                  pl.BlockSpec((1, kk, tb), lambda i: (i, 0, 0)),
                  pl.BlockSpec(memory_space=pl.ANY),
                  pl.BlockSpec(memory_space=pl.ANY)],
        out_specs=pl.BlockSpec((tb, d), lambda i: (i, 0)),
        out_shape=jax.ShapeDtypeStruct((m, d), F32),
        scratch_shapes=[pltpu.VMEM((2, rows, LANES), F32), pltpu.VMEM((2, rows, LANES), F32),
                        pltpu.SemaphoreType.DMA((2,)), pltpu.SemaphoreType.DMA((2,))],
        compiler_params=pltpu.CompilerParams(dimension_semantics=("arbitrary",),
                                             vmem_limit_bytes=VMEM_LIMIT),
        name="peer_tc",
    )(ids, ids, hn, g3, u_tiles, v_tiles)


def _gelu_gate_body(a_ref, g_ref, o_ref):
    o_ref[...] = g_ref[...] * _gelu(a_ref[...])


def _gelu_gate(act, gate, tm=SHARE_GRANULE):
    n, kk = act.shape
    tm = min(tm, n)
    assert n % tm == 0
    spec = pl.BlockSpec((tm, kk), lambda i: (i, 0))
    return pl.pallas_call(
        _gelu_gate_body, grid=(n // tm,), in_specs=[spec, spec], out_specs=spec,
        out_shape=jax.ShapeDtypeStruct((n, kk), F32),
        compiler_params=pltpu.CompilerParams(dimension_semantics=("parallel",)),
        name="gelu_gate",
    )(act, gate)


def _final_body(h_ref, g_ref, *rest, starts, aliased):
    o_ref = rest[-1]
    parts = rest[:len(starts)]
    assert len(rest) == len(starts) + 1 + int(aliased)
    peer = parts[0][...]
    for start, p_ref in zip(starts[1:], parts[1:]):
        peer = jnp.where(pl.program_id(0) >= start, p_ref[...], peer)
    h = h_ref[...] + peer
    o_ref[...] = h * lax.rsqrt(jnp.mean(h * h, axis=-1, keepdims=True) + EPS) * g_ref[...]


def _final(h1, peer_parts, g, out, row0, n_total, tm=256):
    n, d = h1.shape
    sizes = [p.shape[0] for p in peer_parts]
    assert sum(sizes) == n and all(m > 0 and m % tm == 0 for m in sizes) and row0 % tm == 0
    starts = tuple(int(v) // tm for v in np.cumsum([0] + sizes[:-1]))

    def part_spec(start, size):
        return pl.BlockSpec((tm, d), lambda i: (jnp.clip(i - start, 0, size // tm - 1), 0))

    in_specs = [pl.BlockSpec((tm, d), lambda i: (i, 0)), pl.BlockSpec((1, d), lambda i: (0, 0))]
    in_specs += [part_spec(s, m) for s, m in zip(starts, sizes)]
    args = [h1, g, *peer_parts]
    aliases = {}
    if out is not None:
        in_specs.append(pl.BlockSpec(memory_space=pl.ANY))
        aliases = {len(args): 0}
        args.append(out)
    return pl.pallas_call(
        functools.partial(_final_body, starts=starts, aliased=out is not None), grid=(n // tm,),
        in_specs=in_specs,
        out_specs=pl.BlockSpec((tm, d), lambda i: (i + row0 // tm, 0)),
        out_shape=jax.ShapeDtypeStruct((n_total, d), F32),
        input_output_aliases=aliases,
        compiler_params=pltpu.CompilerParams(dimension_semantics=("parallel",)),
        name="final_norm",
    )(*args)


def _tc_share(n_tokens):
    m = int(n_tokens * TC_SHARE) // SHARE_GRANULE * SHARE_GRANULE
    return min(max(m, SHARE_GRANULE), n_tokens - SHARE_GRANULE)


def kernel(x, norm1_g, w_in, hg_lower_logits, hg_norm_g, gla_w_gate_up, gla_b_gate, gla_norm_g,
           w_out, norm2_g, peer_w_q, peer_sub_keys, peer_u, peer_v, norm_f_g):
    batch, seq, d = x.shape
    depth = w_in.shape[0]
    assert depth == 1, "single-layer block"
    layer = 0

    w = w_in[layer]
    pad = LANES - GLA_GATE_RANK
    assert w.shape == (d, IN_WIDTH_PADDED - pad) and d == SUBLANES * LANES
    w_r = jnp.concatenate(
        [w[:, :OFF_GLOW + GLA_GATE_RANK], jnp.zeros((d, pad), w.dtype),
         w[:, OFF_GLOW + GLA_GATE_RANK:]], axis=1).astype(BF16)
    wgu = jnp.concatenate([gla_w_gate_up[layer], jnp.zeros((pad, GLA_KEY), F32)], axis=0).astype(BF16)

    wo = w_out[layer].astype(BF16)
    wq = peer_w_q[layer].astype(BF16)
    keys = peer_sub_keys[layer].reshape(PEER_HEADS * 2, PEER_N_KEYS, PEER_HALF).astype(BF16)
    n_experts = peer_u.shape[1]
    u_tiles = peer_u[layer].reshape(n_experts * SUBLANES, LANES)
    v_tiles = peer_v[layer].reshape(n_experts * SUBLANES, LANES)

    groups = PIPELINE_GROUPS if batch % PIPELINE_GROUPS == 0 else 1
    bpg = batch // groups
    ng = bpg * seq
    m_tc = _tc_share(ng)
    gf = norm_f_g.reshape(1, d)
    x2 = x.reshape(batch * seq, d)

    def after(value, anchor):
        value, _ = lax.optimization_barrier((value, anchor))
        return value

    half = ng // 2
    h1s, tc_ins, sc_units = [], [], []
    for gi in range(groups):
        proj = _inproj(x2, gi * ng, ng, norm1_g[layer].reshape(1, d), w_r)
        mixed = _mixers(proj, bpg, seq, hg_lower_logits, hg_norm_g[layer].reshape(1, HEAD_V), wgu,
                        gla_b_gate[layer].reshape(1, GLA_KEY), gla_norm_g[layer].reshape(1, HEAD_V))
        h1, hn, q3 = _outproj(x2, gi * ng, mixed, wo, norm2_g[layer].reshape(1, d), wq)
        h1s.append(h1)
        extra = SHARE_GRANULE * LAST_GROUP_EXTRA_GRANULES if 0 < gi == groups - 1 else 0
        m_g = min(m_tc + extra, ng - SHARE_GRANULE)
        bounds = (m_g, half, ng) if gi == 0 and m_g < half else (m_g, ng)
        units = []
        for lo, hi in zip(bounds[:-1], bounds[1:]):
            idx_t, gate_t = _route(q3, keys, lo, hi - lo)
            idx = idx_t.T
            units.append(dict(idx=idx, gate=gate_t.T, act=_peer_act(hn[lo:hi], idx, peer_u[layer])))
        sc_units.append(units)
        idx_t, gate_t = _route(q3, keys, 0, m_g)
        tc_ins.append((hn[:m_g], idx_t.T, gate_t))

    def sc_out(gi, anchor):
        outs = []
        for unit in sc_units[gi]:
            wts = _gelu_gate(after(unit["act"], anchor), unit["gate"])
            outs.append(_peer_out(wts, unit["idx"], peer_v[layer]))
        return wts, outs

    rows, peer_sc = [None] * groups, [None] * groups
    rows[0] = _peer_tc(*after(tc_ins[0], sc_units[-1][-1]["idx"]), u_tiles, v_tiles)
    anchor = rows[0]
    for gi in range(groups - 1):
        anchor, peer_sc[gi] = sc_out(gi, rows[0])
    if groups > 1:
        rows[1] = _peer_tc(*after(tc_ins[1], anchor), u_tiles, v_tiles)
        anchor = rows[1]
    anchor, peer_sc[groups - 1] = sc_out(groups - 1, anchor)
    out, finished = None, 0

    def finish(gi, out):
        return _final(h1s[gi], [rows[gi], *peer_sc[gi]], gf, out, gi * ng, batch * seq)

    for gi in range(2, groups):
        out = finish(gi - 2, out)
        finished = gi - 1
        rows[gi] = _peer_tc(*after(tc_ins[gi], (anchor, out)), u_tiles, v_tiles)
        anchor = rows[gi]
    for gi in range(finished, groups):
        out = finish(gi, out)
    return out.reshape(batch, seq, d)
```

```python
import functools

import jax
import jax.numpy as jnp
import numpy as np
from jax import lax
from jax.experimental import pallas as pl
from jax.experimental.pallas import tpu as pltpu
from jax.experimental.pallas import tpu_sc as plsc

F32 = jnp.float32
BF16 = jnp.bfloat16
EPS = 1e-6

HG_HEADS = 4
GLA_HEADS = 4
HEAD_V = 128
HG_KEY = 512
GLA_KEY = 256
GLA_HEAD_K = 64
GLA_GATE_RANK = 16
GLA_GATE_NORMALIZER = 16.0
CHUNK = 64
LANES = 128
PEER_HEADS = 8
PEER_N_KEYS = 128
PEER_HALF = 128
PEER_TOPK = 16

OFF_HQ, OFF_HF, OFF_HI, OFF_HGATE = 0, 512, 1024, 1536
OFF_GQ, OFF_GK, OFF_GV, OFF_GLOW, OFF_GGATE = 2048, 2304, 2560, 3072, 3200
IN_WIDTH_PADDED = 3712

VMEM_LIMIT = 48 * 1024 * 1024

SC_LANES = 16
SC_WORKERS = 32
PIPELINE_GROUPS = 4
TC_SHARE = 0.3125
SHARE_GRANULE = 256
LAST_GROUP_EXTRA_GRANULES = 1


def _nt(a, b):
    return lax.dot_general(a, b, (((1,), (1,)), ((), ())), preferred_element_type=F32)


def _tn(a, b):
    return lax.dot_general(a, b, (((0,), (0,)), ((), ())), preferred_element_type=F32)


def _inproj_body(x_ref, g_ref, w_ref, o_ref):
    x = x_ref[...]
    xn = x * lax.rsqrt(jnp.mean(x * x, axis=-1, keepdims=True) + EPS) * g_ref[...]
    o_ref[...] = jnp.dot(xn.astype(BF16), w_ref[...], preferred_element_type=F32)


def _inproj(x2, row0, n, g, w, tm=256):
    d = x2.shape[1]
    wd = w.shape[1]
    assert row0 % tm == 0 and n % tm == 0
    return pl.pallas_call(
        _inproj_body,
        grid=(n // tm,),
        in_specs=[pl.BlockSpec((tm, d), lambda i: (i + row0 // tm, 0)),
                  pl.BlockSpec((1, d), lambda i: (0, 0)),
                  pl.BlockSpec((d, wd), lambda i: (0, 0))],
        out_specs=pl.BlockSpec((tm, wd), lambda i: (i, 0)),
        out_shape=jax.ShapeDtypeStruct((n, wd), F32),
        compiler_params=pltpu.CompilerParams(dimension_semantics=("parallel",),
                                             vmem_limit_bytes=VMEM_LIMIT),
        name="inproj",
    )(x2, g, w)


def _level_constants():
    c = CHUNK
    mats = [np.tril(np.ones((c, c), np.float32))]
    level = np.full((c, c), -1, np.int32)
    b, lvl = c // 2, 0
    while b >= 1:
        m = np.zeros((c, c), np.float32)
        for s in range(0, c, 2 * b):
            mid = s + b
            for i in range(mid, s + 2 * b):
                m[i, mid:i + 1] = 1.0
                level[i, s:mid] = lvl
            for j in range(s, mid):
                m[j, j + 1:mid] = 1.0
        mats.append(m)
        b //= 2
        lvl += 1
    level[np.arange(c), np.arange(c)] = lvl
    return np.concatenate(mats, axis=0), level, lvl


_SEG_MATS, _LEVEL_MAP, _N_LEVELS = _level_constants()


def _split3(a):
    hi = a.astype(BF16)
    r = a - hi.astype(F32)
    mid = r.astype(BF16)
    lo = (r - mid.astype(F32)).astype(BF16)
    return jnp.concatenate([hi, mid, lo], axis=1)


def _gla_chunk(q, k, g, v, st, seg, level):
    c = CHUNK
    ex3 = jnp.dot(seg, _split3(g), preferred_element_type=F32)
    ex = ex3[:, 0:LANES] + ex3[:, LANES:2 * LANES] + ex3[:, 2 * LANES:3 * LANES]
    cum = ex[0:c]
    scores = jnp.where(level == _N_LEVELS, _nt(q.astype(BF16), k.astype(BF16)), 0.0)
    for l in range(_N_LEVELS):
        e = jnp.exp(ex[c * (l + 1):c * (l + 2)])
        p = _nt((q * e).astype(BF16), (k * e).astype(BF16))
        scores = jnp.where(level == l, p, scores)
    last = cum[c - 1:c, :]
    qd = (q * jnp.exp(cum)).astype(BF16)
    kd = (k * jnp.exp(last - cum)).astype(BF16)
    vb = v.astype(BF16)
    o = jnp.dot(scores.astype(BF16), vb, preferred_element_type=F32) + _nt(qd, st.astype(BF16))
    st_new = st * jnp.exp(last) + _tn(vb, kd)
    return o, st_new


def _head_out(o, gain, gate):
    on = o * lax.rsqrt(jnp.mean(o * o, axis=-1, keepdims=True) + EPS) * gain
    return on * (gate * jax.nn.sigmoid(gate))


def _mixer_body(p_ref, seg_ref, level_ref, lbl_ref, hgn_ref, wgu_ref, bg_ref, ggn_ref,
                o_ref, st_ref, *, chunks):
    @pl.when(pl.program_id(1) == 0)
    def _():
        st_ref[...] = jnp.zeros_like(st_ref)

    seg = seg_ref[...]
    level = level_ref[...]
    logits = lbl_ref[...]
    ez = jnp.exp(logits - jnp.max(logits, axis=0, keepdims=True))
    lb = ez[0:1, :] / jnp.sum(ez, axis=0, keepdims=True)
    lane = lax.broadcasted_iota(jnp.int32, (CHUNK, LANES), 1)

    def chunk_body(ci, carry):
        rows = pl.ds(pl.multiple_of(ci * CHUNK, CHUNK), CHUNK)

        def col(off):
            return p_ref[rows, pl.ds(off, LANES)]

        for h in range(HG_HEADS):
            hq = col(OFF_HQ + h * LANES)
            q = hq * jax.nn.sigmoid(hq)
            lbh = lb[:, h * LANES:(h + 1) * LANES]
            forget = lbh + (1.0 - lbh) * jax.nn.sigmoid(col(OFF_HF + h * LANES))
            o, st = _gla_chunk(q, 1.0 - forget, jnp.log(forget), col(OFF_HI + h * LANES),
                               st_ref[h], seg, level)
            st_ref[h] = st
            o_ref[rows, pl.ds(h * HEAD_V, HEAD_V)] = _head_out(
                o, hgn_ref[...], col(OFF_HGATE + h * HEAD_V)).astype(o_ref.dtype)

        zg = jnp.dot(col(OFF_GLOW).astype(BF16), wgu_ref[...], preferred_element_type=F32) + bg_ref[...]
        log_g = (jnp.minimum(zg, 0.0) - jnp.log(1.0 + jnp.exp(-jnp.abs(zg)))) / GLA_GATE_NORMALIZER
        for h in range(GLA_HEADS):
            pair, half = h // 2, h % 2
            q = col(OFF_GQ + pair * LANES) * (GLA_HEAD_K ** -0.5)
            in_head = (lane >= half * GLA_HEAD_K) & (lane < (half + 1) * GLA_HEAD_K)
            k = jnp.where(in_head, col(OFF_GK + pair * LANES), 0.0)
            g = log_g[:, pair * LANES:(pair + 1) * LANES]
            o, st = _gla_chunk(q, k, g, col(OFF_GV + h * HEAD_V), st_ref[HG_HEADS + h], seg, level)
            st_ref[HG_HEADS + h] = st
            o_ref[rows, pl.ds((HG_HEADS + h) * HEAD_V, HEAD_V)] = _head_out(
                o, ggn_ref[...], col(OFF_GGATE + h * HEAD_V)).astype(o_ref.dtype)
        return carry

    lax.fori_loop(0, chunks, chunk_body, 0)


def _mixers(proj, batch, seq, lb_logits, hg_norm_g, wgu, bg, gla_norm_g, tt=512):
    n = batch * seq
    steps = seq // tt
    heads = HG_HEADS + GLA_HEADS
    const = lambda shape: pl.BlockSpec(shape, lambda b, t: (0,) * len(shape))
    return pl.pallas_call(
        functools.partial(_mixer_body, chunks=tt // CHUNK),
        grid=(batch, steps),
        in_specs=[pl.BlockSpec((tt, IN_WIDTH_PADDED), lambda b, t: (b * steps + t, 0)),
                  const(_SEG_MATS.shape), const(_LEVEL_MAP.shape),
                  const(lb_logits.shape), const((1, HEAD_V)),
                  const(wgu.shape), const(bg.shape), const((1, HEAD_V))],
        out_specs=pl.BlockSpec((tt, heads * HEAD_V), lambda b, t: (b * steps + t, 0)),
        out_shape=jax.ShapeDtypeStruct((n, heads * HEAD_V), BF16),
        scratch_shapes=[pltpu.VMEM((heads, HEAD_V, LANES), F32)],
        compiler_params=pltpu.CompilerParams(dimension_semantics=("parallel", "arbitrary"),
                                             vmem_limit_bytes=VMEM_LIMIT),
        name="mixers",
    )(proj, jnp.asarray(_SEG_MATS, BF16), jnp.asarray(_LEVEL_MAP), lb_logits, hg_norm_g,
      wgu, bg, gla_norm_g)


def _outproj_body(x_ref, m_ref, wo_ref, g_ref, wq_ref, h_ref, hn_ref, q_ref):
    h = x_ref[...] + jnp.dot(m_ref[...], wo_ref[...], preferred_element_type=F32)
    h_ref[...] = h
    hn = h * lax.rsqrt(jnp.mean(h * h, axis=-1, keepdims=True) + EPS) * g_ref[...]
    hn_ref[...] = hn
    q = jnp.dot(hn.astype(BF16), wq_ref[...], preferred_element_type=F32)
    for j in range(q_ref.shape[0]):
        q_ref[j] = q[:, j * PEER_HALF:(j + 1) * PEER_HALF]


def _outproj(x2, row0, mixed, wo, g2, wq, tm=256):
    n, d = mixed.shape[0], x2.shape[1]
    nq = wq.shape[1] // PEER_HALF
    assert row0 % tm == 0 and n % tm == 0
    return pl.pallas_call(
        _outproj_body,
        grid=(n // tm,),
        in_specs=[pl.BlockSpec((tm, d), lambda i: (i + row0 // tm, 0)),
                  pl.BlockSpec((tm, mixed.shape[1]), lambda i: (i, 0)),
                  pl.BlockSpec(wo.shape, lambda i: (0, 0)),
                  pl.BlockSpec((1, d), lambda i: (0, 0)),
                  pl.BlockSpec(wq.shape, lambda i: (0, 0))],
        out_specs=[pl.BlockSpec((tm, d), lambda i: (i, 0)),
                   pl.BlockSpec((tm, d), lambda i: (i, 0)),
                   pl.BlockSpec((nq, tm, PEER_HALF), lambda i: (0, i, 0))],
        out_shape=[jax.ShapeDtypeStruct((n, d), F32),
                   jax.ShapeDtypeStruct((n, d), F32),
                   jax.ShapeDtypeStruct((nq, n, PEER_HALF), F32)],
        compiler_params=pltpu.CompilerParams(dimension_semantics=("parallel",),
                                             vmem_limit_bytes=VMEM_LIMIT),
        name="outproj",
    )(x2, mixed, wo, g2, wq)


def _candidate_constants():
    k = PEER_TOPK
    pos, valid = [], []
    for j in range(k):
        pos.append(j); valid.append(True)
    for i in range(1, 8):
        for j in range(8):
            pos.append(i * k + j); valid.append((i + 1) * (j + 1) <= k)
    for i in range(8, k):
        pos.append(i * k); valid.append(True)
    return np.asarray(pos, np.float32), np.asarray(valid, bool)


_CAND_POS, _CAND_VALID = _candidate_constants()
_N_CAND = _CAND_POS.shape[0]


def _topk_rows(ss, pos, extras, k):
    t = ss[0].shape[1]
    slot = lax.broadcasted_iota(jnp.int32, (k, t), 0)

    def body(it, carry):
        out = []
        for (s, vals, poss, exts), extra in zip(carry, extras):
            m = jnp.max(s, axis=0, keepdims=True)
            p = jnp.min(jnp.where(s == m, pos, 1e9), axis=0, keepdims=True)
            hit = pos == p
            vals = jnp.where(slot == it, m, vals)
            poss = jnp.where(slot == it, p, poss)
            if extra is not None:
                x = jnp.max(jnp.where(hit, extra, -1.0), axis=0, keepdims=True)
                exts = jnp.where(slot == it, x, exts)
            out.append((jnp.where(hit, -jnp.inf, s), vals, poss, exts))
        return tuple(out)

    z = jnp.zeros((k, t), F32)
    res = lax.fori_loop(0, k, body, tuple((s, z, z, z) for s in ss))
    return [(vals, poss, exts) for _, vals, poss, exts in res]


def _route_body(q_ref, keys_ref, cpos_ref, cvalid_ref, idx_ref, gate_ref):
    k = PEER_TOPK
    n_tiles = q_ref.shape[1] // LANES
    key_pos = lax.broadcasted_iota(jnp.int32, (PEER_N_KEYS, LANES), 0).astype(F32)
    cpos = cpos_ref[...]
    cvalid = cvalid_ref[...] > 0.5
    scores = [_nt(keys_ref[p], q_ref[p, pl.ds(lt * LANES, LANES), :].astype(BF16))
              for lt in range(n_tiles) for p in range(2)]
    tops = _topk_rows(scores, key_pos, [None] * len(scores), k)
    cands, experts = [], []
    for lt in range(n_tiles):
        (v0, i0, _), (v1, i1, _) = tops[2 * lt], tops[2 * lt + 1]
        vals = [v0[0:1] + v1]
        exps = [i0[0:1] * PEER_N_KEYS + i1]
        for i in range(1, 8):
            vals.append(v0[i:i + 1] + v1[0:8])
            exps.append(i0[i:i + 1] * PEER_N_KEYS + i1[0:8])
        vals.append(v0[8:k] + v1[0:1])
        exps.append(i0[8:k] * PEER_N_KEYS + i1[0:1])
        cands.append(jnp.where(cvalid, jnp.concatenate(vals, axis=0), -jnp.inf))
        experts.append(jnp.concatenate(exps, axis=0))
    picks = _topk_rows(cands, cpos, experts, k)
    for lt in range(n_tiles):
        best, _, expert = picks[lt]
        cols = pl.ds(lt * LANES, LANES)
        e = jnp.exp(best - jnp.max(best, axis=0, keepdims=True))
        gate_ref[:, cols] = e / jnp.sum(e, axis=0, keepdims=True)
        idx_ref[:, cols] = expert.astype(jnp.int32)


def _route(q3, keys, tok0, n):
    heads = q3.shape[0] // 2
    tb = 512 if tok0 % 512 == 0 and n % 512 == 0 else SHARE_GRANULE
    assert tok0 % tb == 0 and n % tb == 0
    cpos = jnp.broadcast_to(jnp.asarray(_CAND_POS)[:, None], (_N_CAND, LANES))
    cvalid = jnp.broadcast_to(jnp.asarray(_CAND_VALID, F32)[:, None], (_N_CAND, LANES))
    return pl.pallas_call(
        _route_body,
        grid=(n // tb, heads),
        in_specs=[pl.BlockSpec((2, tb, PEER_HALF), lambda i, h: (h, i + tok0 // tb, 0)),
                  pl.BlockSpec((2, PEER_N_KEYS, PEER_HALF), lambda i, h: (h, 0, 0)),
                  pl.BlockSpec((_N_CAND, LANES), lambda i, h: (0, 0)),
                  pl.BlockSpec((_N_CAND, LANES), lambda i, h: (0, 0))],
        out_specs=[pl.BlockSpec((PEER_TOPK, tb), lambda i, h: (h, i)),
                   pl.BlockSpec((PEER_TOPK, tb), lambda i, h: (h, i))],
        out_shape=[jax.ShapeDtypeStruct((heads * PEER_TOPK, n), jnp.int32),
                   jax.ShapeDtypeStruct((heads * PEER_TOPK, n), F32)],
        compiler_params=pltpu.CompilerParams(dimension_semantics=("parallel", "parallel"),
                                             vmem_limit_bytes=VMEM_LIMIT),
        name="route",
    )(q3, keys, cpos, cvalid)


def _sc_pipeline(idx_v, table_hbm, rows, sems, n_units, units_per_token, compute):
    parts = rows[0].shape[0] // SC_LANES
    assert n_units % 2 == 0

    def start(unit, slot):
        tl = unit // units_per_token
        g = unit % units_per_token
        for part in range(parts):
            iv = idx_v[tl, pl.ds((g * parts + part) * SC_LANES, SC_LANES)]
            pltpu.async_copy(table_hbm.at[iv], rows[slot].at[pl.ds(part * SC_LANES, SC_LANES)], sems[slot])

    def wait(slot):
        iv = idx_v[0, pl.ds(0, SC_LANES)]
        for part in range(parts):
            pltpu.make_async_copy(table_hbm.at[iv], rows[slot].at[pl.ds(part * SC_LANES, SC_LANES)],
                                  sems[slot]).wait()

    start(0, 0)

    def pair(i, carry):
        u0 = 2 * i
        start(u0 + 1, 1)
        wait(0)
        compute(u0 // units_per_token, u0 % units_per_token, rows[0])

        @pl.when(u0 + 2 < n_units)
        def _():
            start(u0 + 2, 0)

        wait(1)
        compute((u0 + 1) // units_per_token, (u0 + 1) % units_per_token, rows[1])
        return carry

    lax.fori_loop(0, n_units // 2, pair, 0)


def _sc_batch(tokens_per_worker):
    for tb in (40, 32, 24, 16, 8):
        if tokens_per_worker % tb == 0:
            return tb
    raise ValueError(f"tokens per SparseCore worker ({tokens_per_worker}) must be a multiple of 8")


def _peer_act(hn, idx, u, unit_rows=2 * SC_LANES):
    n, d = hn.shape
    kk = idx.shape[1]
    tpw = n // SC_WORKERS
    tb = _sc_batch(tpw)
    upt = kk // unit_rows
    parts = unit_rows // SC_LANES
    nchunk = d // SC_LANES
    mesh = plsc.VectorSubcoreMesh(core_axis_name="c", subcore_axis_name="s")

    def body(x_hbm, idx_hbm, u_hbm, out_hbm, x_v, idx_v, act_v, rows0, rows1, tmp_v, sem0, sem1):
        wid = lax.axis_index("s") * 2 + lax.axis_index("c")
        lane_row = lax.iota(jnp.int32, SC_LANES) * SC_LANES

        def compute(tl, g, r_ref):
            for part in range(parts):
                row0 = part * SC_LANES

                def cbody(c, accs, row0=row0):
                    xv = x_v[tl, pl.ds(c * SC_LANES, SC_LANES)]
                    return tuple(accs[r] + r_ref[row0 + r, pl.ds(c * SC_LANES, SC_LANES)] * xv
                                 for r in range(SC_LANES))
                accs = lax.fori_loop(0, nchunk, cbody,
                                     tuple(jnp.zeros((SC_LANES,), F32) for _ in range(SC_LANES)))
                for r in range(SC_LANES):
                    tmp_v[pl.ds(r * SC_LANES, SC_LANES)] = accs[r]
                tot = plsc.load_gather(tmp_v, [lane_row])
                for j in range(1, SC_LANES):
                    tot = tot + plsc.load_gather(tmp_v, [lane_row + j])
                act_v[tl, pl.ds((g * parts + part) * SC_LANES, SC_LANES)] = tot

        def batch(b, carry):
            t0 = wid * tpw + b * tb
            pltpu.sync_copy(x_hbm.at[pl.ds(t0, tb)], x_v)
            pltpu.sync_copy(idx_hbm.at[pl.ds(t0, tb)], idx_v)
            _sc_pipeline(idx_v, u_hbm, (rows0, rows1), (sem0, sem1), tb * upt, upt, compute)
            pltpu.sync_copy(act_v, out_hbm.at[pl.ds(t0, tb)])
            return carry

        lax.fori_loop(0, tpw // tb, batch, 0)

    return pl.kernel(
        body, mesh=mesh,
        compiler_params=pltpu.CompilerParams(needs_layout_passes=False),
        out_type=jax.ShapeDtypeStruct((n, kk), F32),
        scratch_types=[pltpu.VMEM((tb, d), F32), pltpu.VMEM((tb, kk), jnp.int32),
                       pltpu.VMEM((tb, kk), F32),
                       pltpu.VMEM((unit_rows, d), F32), pltpu.VMEM((unit_rows, d), F32),
                       pltpu.VMEM((SC_LANES * SC_LANES,), F32),
                       pltpu.SemaphoreType.DMA, pltpu.SemaphoreType.DMA],
        name="peer_act",
    )(hn, idx, u)


def _peer_out(w, idx, v, cb=16, unit_rows=2 * SC_LANES):
    n, kk = w.shape
    d = v.shape[1]
    tpw = n // SC_WORKERS
    tb = _sc_batch(tpw)
    upt = kk // unit_rows
    nchunk = d // SC_LANES
    mesh = plsc.VectorSubcoreMesh(core_axis_name="c", subcore_axis_name="s")

    def body(w_hbm, idx_hbm, v_hbm, out_hbm, w_v, idx_v, out_v, rows0, rows1, sem0, sem1):
        wid = lax.axis_index("s") * 2 + lax.axis_index("c")
        zero = jnp.zeros((SC_LANES,), F32)

        def compute(tl, g, r_ref):
            tls = jnp.full((SC_LANES,), tl, jnp.int32)

            def cb_body(ci, carry):
                c0 = ci * (cb * SC_LANES)
                accs = tuple(out_v[tl, pl.ds(c0 + j * SC_LANES, SC_LANES)] for j in range(cb))

                def rbody(r, accs):
                    wr = plsc.load_gather(w_v, [tls, jnp.full((SC_LANES,), g * unit_rows + r, jnp.int32)])
                    return tuple(accs[j] + wr * r_ref[r, pl.ds(c0 + j * SC_LANES, SC_LANES)]
                                 for j in range(cb))

                accs = lax.fori_loop(0, unit_rows, rbody, accs)
                for j in range(cb):
                    out_v[tl, pl.ds(c0 + j * SC_LANES, SC_LANES)] = accs[j]
                return carry

            lax.fori_loop(0, nchunk // cb, cb_body, 0)

        def batch(b, carry):
            t0 = wid * tpw + b * tb
            pltpu.sync_copy(w_hbm.at[pl.ds(t0, tb)], w_v)
            pltpu.sync_copy(idx_hbm.at[pl.ds(t0, tb)], idx_v)

            def zbody(i, carry):
                out_v[i // nchunk, pl.ds((i % nchunk) * SC_LANES, SC_LANES)] = zero
                return carry

            lax.fori_loop(0, tb * nchunk, zbody, 0)
            _sc_pipeline(idx_v, v_hbm, (rows0, rows1), (sem0, sem1), tb * upt, upt, compute)
            pltpu.sync_copy(out_v, out_hbm.at[pl.ds(t0, tb)])
            return carry

        lax.fori_loop(0, tpw // tb, batch, 0)

    return pl.kernel(
        body, mesh=mesh,
        compiler_params=pltpu.CompilerParams(needs_layout_passes=False),
        out_type=jax.ShapeDtypeStruct((n, d), F32),
        scratch_types=[pltpu.VMEM((tb, kk), F32), pltpu.VMEM((tb, kk), jnp.int32),
                       pltpu.VMEM((tb, d), F32),
                       pltpu.VMEM((unit_rows, d), F32), pltpu.VMEM((unit_rows, d), F32),
                       pltpu.SemaphoreType.DMA, pltpu.SemaphoreType.DMA],
        name="peer_out",
    )(w, idx, v)


SUBLANES = 8
TC_PEER_TOKENS = 8
ISSUE_UNROLL = 8


def _gelu(a):
    return a * (lax.erf(a / np.sqrt(2.0).astype(np.float32)) + 1.0) / 2.0


def _peer_tc_body(idx_ref, idx_next_ref, x_ref, g_ref, u_hbm, v_hbm, o_ref,
                  ubuf0, ubuf1, vbuf0, vbuf1, sem_u, sem_v):
    tokens = x_ref.shape[0]
    kk = g_ref.shape[1]
    n_rows = tokens * kk
    step = pl.program_id(0)
    last = pl.num_programs(0) - 1
    ubufs, vbufs = (ubuf0, ubuf1), (vbuf0, vbuf1)

    def row_copy(table, buf, sem, expert, j):
        src = table.at[pl.ds(pl.multiple_of(expert * SUBLANES, SUBLANES), SUBLANES)]
        return pltpu.make_async_copy(src, buf.at[pl.ds(j * SUBLANES, SUBLANES)], sem)

    def wait_all(table, buf, sem):
        pltpu.make_async_copy(table.at[pl.ds(0, n_rows * SUBLANES)], buf, sem).wait()

    @pl.when(step == 0)
    def _():
        def issue(i, carry):
            for r in range(ISSUE_UNROLL):
                j = i * ISSUE_UNROLL + r
                expert = idx_ref[j]
                row_copy(u_hbm, ubuf0, sem_u.at[0], expert, j).start()
                row_copy(v_hbm, vbuf0, sem_v.at[0], expert, j).start(priority=1)
            return carry

        lax.fori_loop(0, n_rows // ISSUE_UNROLL, issue, 0)

    def run(cur, nxt):
        ub, vb = ubufs[cur], vbufs[cur]

        def rows_of(buf, t, s):
            return buf[pl.ds(t * kk * SUBLANES + s, kk, stride=SUBLANES), :]

        def request(table, bufs, sem, t, priority):
            for r in range(kk):
                j = t * kk + r
                row_copy(table, bufs[nxt], sem.at[nxt], idx_next_ref[j], j).start(priority=priority)

        wait_all(u_hbm, ub, sem_u.at[cur])
        acts = []
        for t in range(tokens):
            request(u_hbm, ubufs, sem_u, t, 0)
            acc = rows_of(ub, t, 0) * x_ref[t:t + 1, 0:LANES]
            for s in range(1, SUBLANES):
                acc = acc + rows_of(ub, t, s) * x_ref[t:t + 1, s * LANES:(s + 1) * LANES]
            acts.append(jnp.sum(acc, axis=1, keepdims=True))
        w = g_ref[0] * _gelu(jnp.concatenate(acts, axis=1))

        wait_all(v_hbm, vb, sem_v.at[cur])
        for t in range(tokens):
            request(v_hbm, vbufs, sem_v, t, 1)
            wt = w[:, t:t + 1]
            for s in range(SUBLANES):
                o_ref[t:t + 1, s * LANES:(s + 1) * LANES] = jnp.sum(rows_of(vb, t, s) * wt, axis=0,
                                                                    keepdims=True)

        @pl.when(step == last)
        def _():
            wait_all(u_hbm, ubufs[nxt], sem_u.at[nxt])
            wait_all(v_hbm, vbufs[nxt], sem_v.at[nxt])

    @pl.when(step % 2 == 0)
    def _():
        run(0, 1)

    @pl.when(step % 2 == 1)
    def _():
        run(1, 0)


def _peer_tc(hn, idx, gate_t, u_tiles, v_tiles):
    m, d = hn.shape
    kk = idx.shape[1]
    tb = TC_PEER_TOKENS
    assert d == SUBLANES * LANES and m % tb == 0
    steps = m // tb
    g3 = gate_t.reshape(kk, steps, tb).transpose(1, 0, 2)
    rows = tb * kk * SUBLANES
    ids = idx.reshape(m * kk)
    return pl.pallas_call(
        _peer_tc_body,
        grid=(steps,),
        in_specs=[pl.BlockSpec((tb * kk,), lambda i: (i,), memory_space=pltpu.SMEM),
                  pl.BlockSpec((tb * kk,), lambda i: (jnp.minimum(i + 1, steps - 1),), memory_space=pltpu.SMEM),
                  pl.BlockSpec((tb, d), lambda i: (i, 0)),
                  pl.BlockSpec((1, kk, tb), lambda i: (i, 0, 0)),
                  pl.BlockSpec(memory_space=pl.ANY),
                  pl.BlockSpec(memory_space=pl.ANY)],
        out_specs=pl.BlockSpec((tb, d), lambda i: (i, 0)),
        out_shape=jax.ShapeDtypeStruct((m, d), F32),
        scratch_shapes=[pltpu.VMEM((rows, LANES), F32), pltpu.VMEM((rows, LANES), F32),
                        pltpu.VMEM((rows, LANES), F32), pltpu.VMEM((rows, LANES), F32),
                        pltpu.SemaphoreType.DMA((2,)), pltpu.SemaphoreType.DMA((2,))],
        compiler_params=pltpu.CompilerParams(dimension_semantics=("arbitrary",),
                                             vmem_limit_bytes=VMEM_LIMIT),
        name="peer_tc",
    )(ids, ids, hn, g3, u_tiles, v_tiles)


def _gelu_gate_body(a_ref, g_ref, o_ref):
    o_ref[...] = g_ref[...] * _gelu(a_ref[...])


def _gelu_gate(act, gate, tm=SHARE_GRANULE):
    n, kk = act.shape
    tm = min(tm, n)
    assert n % tm == 0
    spec = pl.BlockSpec((tm, kk), lambda i: (i, 0))
    return pl.pallas_call(
        _gelu_gate_body, grid=(n // tm,), in_specs=[spec, spec], out_specs=spec,
        out_shape=jax.ShapeDtypeStruct((n, kk), F32),
        compiler_params=pltpu.CompilerParams(dimension_semantics=("parallel",)),
        name="gelu_gate",
    )(act, gate)


def _final_body(h_ref, g_ref, *rest, starts, aliased):
    o_ref = rest[-1]
    parts = rest[:len(starts)]
    assert len(rest) == len(starts) + 1 + int(aliased)
    peer = parts[0][...]
    for start, p_ref in zip(starts[1:], parts[1:]):
        peer = jnp.where(pl.program_id(0) >= start, p_ref[...], peer)
    h = h_ref[...] + peer
    o_ref[...] = h * lax.rsqrt(jnp.mean(h * h, axis=-1, keepdims=True) + EPS) * g_ref[...]


def _final(h1, peer_parts, g, out, row0, n_total, tm=256):
    n, d = h1.shape
    sizes = [p.shape[0] for p in peer_parts]
    assert sum(sizes) == n and all(m > 0 and m % tm == 0 for m in sizes) and row0 % tm == 0
    starts = tuple(int(v) // tm for v in np.cumsum([0] + sizes[:-1]))

    def part_spec(start, size):
        return pl.BlockSpec((tm, d), lambda i: (jnp.clip(i - start, 0, size // tm - 1), 0))

    in_specs = [pl.BlockSpec((tm, d), lambda i: (i, 0)), pl.BlockSpec((1, d), lambda i: (0, 0))]
    in_specs += [part_spec(s, m) for s, m in zip(starts, sizes)]
    args = [h1, g, *peer_parts]
    aliases = {}
    if out is not None:
        in_specs.append(pl.BlockSpec(memory_space=pl.ANY))
        aliases = {len(args): 0}
        args.append(out)
    return pl.pallas_call(
        functools.partial(_final_body, starts=starts, aliased=out is not None), grid=(n // tm,),
        in_specs=in_specs,
        out_specs=pl.BlockSpec((tm, d), lambda i: (i + row0 // tm, 0)),
        out_shape=jax.ShapeDtypeStruct((n_total, d), F32),
        input_output_aliases=aliases,
        compiler_params=pltpu.CompilerParams(dimension_semantics=("parallel",)),
        name="final_norm",
    )(*args)


def _tc_share(n_tokens):
    m = int(n_tokens * TC_SHARE) // SHARE_GRANULE * SHARE_GRANULE
    return min(max(m, SHARE_GRANULE), n_tokens - SHARE_GRANULE)


def kernel(x, norm1_g, w_in, hg_lower_logits, hg_norm_g, gla_w_gate_up, gla_b_gate, gla_norm_g,
           w_out, norm2_g, peer_w_q, peer_sub_keys, peer_u, peer_v, norm_f_g):
    batch, seq, d = x.shape
    depth = w_in.shape[0]
    assert depth == 1, "single-layer block"
    layer = 0

    w = w_in[layer]
    pad = LANES - GLA_GATE_RANK
    assert w.shape == (d, IN_WIDTH_PADDED - pad) and d == SUBLANES * LANES
    w_r = jnp.concatenate(
        [w[:, :OFF_GLOW + GLA_GATE_RANK], jnp.zeros((d, pad), w.dtype),
         w[:, OFF_GLOW + GLA_GATE_RANK:]], axis=1).astype(BF16)
    wgu = jnp.concatenate([gla_w_gate_up[layer], jnp.zeros((pad, GLA_KEY), F32)], axis=0).astype(BF16)

    wo = w_out[layer].astype(BF16)
    wq = peer_w_q[layer].astype(BF16)
    keys = peer_sub_keys[layer].reshape(PEER_HEADS * 2, PEER_N_KEYS, PEER_HALF).astype(BF16)
    n_experts = peer_u.shape[1]
    u_tiles = peer_u[layer].reshape(n_experts * SUBLANES, LANES)
    v_tiles = peer_v[layer].reshape(n_experts * SUBLANES, LANES)

    groups = PIPELINE_GROUPS if batch % PIPELINE_GROUPS == 0 else 1
    bpg = batch // groups
    ng = bpg * seq
    m_tc = _tc_share(ng)
    gf = norm_f_g.reshape(1, d)
    x2 = x.reshape(batch * seq, d)

    def after(value, anchor):
        value, _ = lax.optimization_barrier((value, anchor))
        return value

    half = ng // 2
    h1s, tc_ins, sc_units = [], [], []
    for gi in range(groups):
        proj = _inproj(x2, gi * ng, ng, norm1_g[layer].reshape(1, d), w_r)
        mixed = _mixers(proj, bpg, seq, hg_lower_logits, hg_norm_g[layer].reshape(1, HEAD_V), wgu,
                        gla_b_gate[layer].reshape(1, GLA_KEY), gla_norm_g[layer].reshape(1, HEAD_V))
        h1, hn, q3 = _outproj(x2, gi * ng, mixed, wo, norm2_g[layer].reshape(1, d), wq)
        h1s.append(h1)
        extra = SHARE_GRANULE * LAST_GROUP_EXTRA_GRANULES if 0 < gi == groups - 1 else 0
        m_g = min(m_tc + extra, ng - SHARE_GRANULE)
        bounds = (m_g, half, ng) if gi == 0 and m_g < half else (m_g, ng)
        units = []
        for lo, hi in zip(bounds[:-1], bounds[1:]):
            idx_t, gate_t = _route(q3, keys, lo, hi - lo)
            idx = idx_t.T
            units.append(dict(idx=idx, gate=gate_t.T, act=_peer_act(hn[lo:hi], idx, peer_u[layer])))
        sc_units.append(units)
        idx_t, gate_t = _route(q3, keys, 0, m_g)
        tc_ins.append((hn[:m_g], idx_t.T, gate_t))

    def sc_out(gi, anchor):
        outs = []
        for unit in sc_units[gi]:
            wts = _gelu_gate(after(unit["act"], anchor), unit["gate"])
            outs.append(_peer_out(wts, unit["idx"], peer_v[layer]))
        return wts, outs

    rows, peer_sc = [None] * groups, [None] * groups
    rows[0] = _peer_tc(*after(tc_ins[0], sc_units[-1][-1]["idx"]), u_tiles, v_tiles)
    anchor = rows[0]
    for gi in range(groups - 1):
        anchor, peer_sc[gi] = sc_out(gi, rows[0])
    if groups > 1:
        rows[1] = _peer_tc(*after(tc_ins[1], anchor), u_tiles, v_tiles)
        anchor = rows[1]
    anchor, peer_sc[groups - 1] = sc_out(groups - 1, anchor)
    out, finished = None, 0

    def finish(gi, out):
        return _final(h1s[gi], [rows[gi], *peer_sc[gi]], gf, out, gi * ng, batch * seq)

    for gi in range(2, groups):
        out = finish(gi - 2, out)
        finished = gi - 1
        rows[gi] = _peer_tc(*after(tc_ins[gi], (anchor, out)), u_tiles, v_tiles)
        anchor = rows[gi]
    for gi in range(finished, groups):
        out = finish(gi, out)
    return out.reshape(batch, seq, d)
```

```python
import functools

import jax
import jax.numpy as jnp
import numpy as np
from jax import lax
from jax.experimental import pallas as pl
from jax.experimental.pallas import tpu as pltpu
from jax.experimental.pallas import tpu_sc as plsc

F32 = jnp.float32
BF16 = jnp.bfloat16
EPS = 1e-6

HG_HEADS = 4
GLA_HEADS = 4
HEAD_V = 128
HG_KEY = 512
GLA_KEY = 256
GLA_HEAD_K = 64
GLA_GATE_RANK = 16
GLA_GATE_NORMALIZER = 16.0
CHUNK = 64
LANES = 128
PEER_HEADS = 8
PEER_N_KEYS = 128
PEER_HALF = 128
PEER_TOPK = 16

OFF_HQ, OFF_HF, OFF_HI, OFF_HGATE = 0, 512, 1024, 1536
OFF_GQ, OFF_GK, OFF_GV, OFF_GLOW, OFF_GGATE = 2048, 2304, 2560, 3072, 3200
IN_WIDTH_PADDED = 3712

VMEM_LIMIT = 48 * 1024 * 1024

SC_LANES = 16
SC_WORKERS = 32
PIPELINE_GROUPS = 4
TC_SHARE = 0.3125
SHARE_GRANULE = 256
LAST_GROUP_EXTRA_GRANULES = 2


def _nt(a, b):
    return lax.dot_general(a, b, (((1,), (1,)), ((), ())), preferred_element_type=F32)


def _tn(a, b):
    return lax.dot_general(a, b, (((0,), (0,)), ((), ())), preferred_element_type=F32)


def _inproj_body(x_ref, g_ref, w_ref, o_ref):
    x = x_ref[...]
    xn = x * lax.rsqrt(jnp.mean(x * x, axis=-1, keepdims=True) + EPS) * g_ref[...]
    o_ref[...] = jnp.dot(xn.astype(BF16), w_ref[...], preferred_element_type=F32)


def _inproj(x2, row0, n, g, w, tm=256):
    d = x2.shape[1]
    wd = w.shape[1]
    assert row0 % tm == 0 and n % tm == 0
    return pl.pallas_call(
        _inproj_body,
        grid=(n // tm,),
        in_specs=[pl.BlockSpec((tm, d), lambda i: (i + row0 // tm, 0)),
                  pl.BlockSpec((1, d), lambda i: (0, 0)),
                  pl.BlockSpec((d, wd), lambda i: (0, 0))],
        out_specs=pl.BlockSpec((tm, wd), lambda i: (i, 0)),
        out_shape=jax.ShapeDtypeStruct((n, wd), F32),
        compiler_params=pltpu.CompilerParams(dimension_semantics=("parallel",),
                                             vmem_limit_bytes=VMEM_LIMIT),
        name="inproj",
    )(x2, g, w)


def _level_constants():
    c = CHUNK
    mats = [np.tril(np.ones((c, c), np.float32))]
    level = np.full((c, c), -1, np.int32)
    b, lvl = c // 2, 0
    while b >= 1:
        m = np.zeros((c, c), np.float32)
        for s in range(0, c, 2 * b):
            mid = s + b
            for i in range(mid, s + 2 * b):
                m[i, mid:i + 1] = 1.0
                level[i, s:mid] = lvl
            for j in range(s, mid):
                m[j, j + 1:mid] = 1.0
        mats.append(m)
        b //= 2
        lvl += 1
    level[np.arange(c), np.arange(c)] = lvl
    return np.concatenate(mats, axis=0), level, lvl


_SEG_MATS, _LEVEL_MAP, _N_LEVELS = _level_constants()


def _split3(a):
    hi = a.astype(BF16)
    r = a - hi.astype(F32)
    mid = r.astype(BF16)
    lo = (r - mid.astype(F32)).astype(BF16)
    return jnp.concatenate([hi, mid, lo], axis=1)


def _gla_chunk(q, k, g, v, st, seg, level):
    c = CHUNK
    ex3 = jnp.dot(seg, _split3(g), preferred_element_type=F32)
    ex = ex3[:, 0:LANES] + ex3[:, LANES:2 * LANES] + ex3[:, 2 * LANES:3 * LANES]
    cum = ex[0:c]
    scores = jnp.where(level == _N_LEVELS, _nt(q.astype(BF16), k.astype(BF16)), 0.0)
    for l in range(_N_LEVELS):
        e = jnp.exp(ex[c * (l + 1):c * (l + 2)])
        p = _nt((q * e).astype(BF16), (k * e).astype(BF16))
        scores = jnp.where(level == l, p, scores)
    last = cum[c - 1:c, :]
    qd = (q * jnp.exp(cum)).astype(BF16)
    kd = (k * jnp.exp(last - cum)).astype(BF16)
    vb = v.astype(BF16)
    o = jnp.dot(scores.astype(BF16), vb, preferred_element_type=F32) + _nt(qd, st.astype(BF16))
    st_new = st * jnp.exp(last) + _tn(vb, kd)
    return o, st_new


def _head_out(o, gain, gate):
    on = o * lax.rsqrt(jnp.mean(o * o, axis=-1, keepdims=True) + EPS) * gain
    return on * (gate * jax.nn.sigmoid(gate))


def _mixer_body(p_ref, seg_ref, level_ref, lbl_ref, hgn_ref, wgu_ref, bg_ref, ggn_ref,
                o_ref, st_ref, *, chunks):
    @pl.when(pl.program_id(1) == 0)
    def _():
        st_ref[...] = jnp.zeros_like(st_ref)

    seg = seg_ref[...]
    level = level_ref[...]
    logits = lbl_ref[...]
    ez = jnp.exp(logits - jnp.max(logits, axis=0, keepdims=True))
    lb = ez[0:1, :] / jnp.sum(ez, axis=0, keepdims=True)
    lane = lax.broadcasted_iota(jnp.int32, (CHUNK, LANES), 1)

    def chunk_body(ci, carry):
        rows = pl.ds(pl.multiple_of(ci * CHUNK, CHUNK), CHUNK)

        def col(off):
            return p_ref[rows, pl.ds(off, LANES)]

        for h in range(HG_HEADS):
            hq = col(OFF_HQ + h * LANES)
            q = hq * jax.nn.sigmoid(hq)
            lbh = lb[:, h * LANES:(h + 1) * LANES]
            forget = lbh + (1.0 - lbh) * jax.nn.sigmoid(col(OFF_HF + h * LANES))
            o, st = _gla_chunk(q, 1.0 - forget, jnp.log(forget), col(OFF_HI + h * LANES),
                               st_ref[h], seg, level)
            st_ref[h] = st
            o_ref[rows, pl.ds(h * HEAD_V, HEAD_V)] = _head_out(
                o, hgn_ref[...], col(OFF_HGATE + h * HEAD_V)).astype(o_ref.dtype)

        zg = jnp.dot(col(OFF_GLOW).astype(BF16), wgu_ref[...], preferred_element_type=F32) + bg_ref[...]
        log_g = (jnp.minimum(zg, 0.0) - jnp.log(1.0 + jnp.exp(-jnp.abs(zg)))) / GLA_GATE_NORMALIZER
        for h in range(GLA_HEADS):
            pair, half = h // 2, h % 2
            q = col(OFF_GQ + pair * LANES) * (GLA_HEAD_K ** -0.5)
            in_head = (lane >= half * GLA_HEAD_K) & (lane < (half + 1) * GLA_HEAD_K)
            k = jnp.where(in_head, col(OFF_GK + pair * LANES), 0.0)
            g = log_g[:, pair * LANES:(pair + 1) * LANES]
            o, st = _gla_chunk(q, k, g, col(OFF_GV + h * HEAD_V), st_ref[HG_HEADS + h], seg, level)
            st_ref[HG_HEADS + h] = st
            o_ref[rows, pl.ds((HG_HEADS + h) * HEAD_V, HEAD_V)] = _head_out(
                o, ggn_ref[...], col(OFF_GGATE + h * HEAD_V)).astype(o_ref.dtype)
        return carry

    lax.fori_loop(0, chunks, chunk_body, 0)


def _mixers(proj, batch, seq, lb_logits, hg_norm_g, wgu, bg, gla_norm_g, tt=512):
    n = batch * seq
    steps = seq // tt
    heads = HG_HEADS + GLA_HEADS
    const = lambda shape: pl.BlockSpec(shape, lambda b, t: (0,) * len(shape))
    return pl.pallas_call(
        functools.partial(_mixer_body, chunks=tt // CHUNK),
        grid=(batch, steps),
        in_specs=[pl.BlockSpec((tt, IN_WIDTH_PADDED), lambda b, t: (b * steps + t, 0)),
                  const(_SEG_MATS.shape), const(_LEVEL_MAP.shape),
                  const(lb_logits.shape), const((1, HEAD_V)),
                  const(wgu.shape), const(bg.shape), const((1, HEAD_V))],
        out_specs=pl.BlockSpec((tt, heads * HEAD_V), lambda b, t: (b * steps + t, 0)),
        out_shape=jax.ShapeDtypeStruct((n, heads * HEAD_V), BF16),
        scratch_shapes=[pltpu.VMEM((heads, HEAD_V, LANES), F32)],
        compiler_params=pltpu.CompilerParams(dimension_semantics=("parallel", "arbitrary"),
                                             vmem_limit_bytes=VMEM_LIMIT),
        name="mixers",
    )(proj, jnp.asarray(_SEG_MATS, BF16), jnp.asarray(_LEVEL_MAP), lb_logits, hg_norm_g,
      wgu, bg, gla_norm_g)


def _outproj_body(x_ref, m_ref, wo_ref, g_ref, wq_ref, h_ref, hn_ref, q_ref):
    h = x_ref[...] + jnp.dot(m_ref[...], wo_ref[...], preferred_element_type=F32)
    h_ref[...] = h
    hn = h * lax.rsqrt(jnp.mean(h * h, axis=-1, keepdims=True) + EPS) * g_ref[...]
    hn_ref[...] = hn
    q = jnp.dot(hn.astype(BF16), wq_ref[...], preferred_element_type=F32)
    for j in range(q_ref.shape[0]):
        q_ref[j] = q[:, j * PEER_HALF:(j + 1) * PEER_HALF]


def _outproj(x2, row0, mixed, wo, g2, wq, tm=256):
    n, d = mixed.shape[0], x2.shape[1]
    nq = wq.shape[1] // PEER_HALF
    assert row0 % tm == 0 and n % tm == 0
    return pl.pallas_call(
        _outproj_body,
        grid=(n // tm,),
        in_specs=[pl.BlockSpec((tm, d), lambda i: (i + row0 // tm, 0)),
                  pl.BlockSpec((tm, mixed.shape[1]), lambda i: (i, 0)),
                  pl.BlockSpec(wo.shape, lambda i: (0, 0)),
                  pl.BlockSpec((1, d), lambda i: (0, 0)),
                  pl.BlockSpec(wq.shape, lambda i: (0, 0))],
        out_specs=[pl.BlockSpec((tm, d), lambda i: (i, 0)),
                   pl.BlockSpec((tm, d), lambda i: (i, 0)),
                   pl.BlockSpec((nq, tm, PEER_HALF), lambda i: (0, i, 0))],
        out_shape=[jax.ShapeDtypeStruct((n, d), F32),
                   jax.ShapeDtypeStruct((n, d), F32),
                   jax.ShapeDtypeStruct((nq, n, PEER_HALF), F32)],
        compiler_params=pltpu.CompilerParams(dimension_semantics=("parallel",),
                                             vmem_limit_bytes=VMEM_LIMIT),
        name="outproj",
    )(x2, mixed, wo, g2, wq)


def _candidate_constants():
    k = PEER_TOPK
    pos, valid = [], []
    for j in range(k):
        pos.append(j); valid.append(True)
    for i in range(1, 8):
        for j in range(8):
            pos.append(i * k + j); valid.append((i + 1) * (j + 1) <= k)
    for i in range(8, k):
        pos.append(i * k); valid.append(True)
    return np.asarray(pos, np.float32), np.asarray(valid, bool)


_CAND_POS, _CAND_VALID = _candidate_constants()
_N_CAND = _CAND_POS.shape[0]


def _topk_rows(ss, pos, extras, k):
    r, t = ss[0].shape
    slot = lax.broadcasted_iota(jnp.int32, (k, t), 0)

    def scoped(s_ref):
        for c, s in enumerate(ss):
            s_ref[c] = s

        def body(it, carry):
            out = []
            for c, ((vals, poss, exts), extra) in enumerate(zip(carry, extras)):
                s = s_ref[c]
                m = jnp.max(s, axis=0, keepdims=True)
                p = jnp.min(jnp.where(s == m, pos, 1e9), axis=0, keepdims=True)
                hit = pos == p
                vals = jnp.where(slot == it, m, vals)
                poss = jnp.where(slot == it, p, poss)
                if extra is not None:
                    x = jnp.max(jnp.where(hit, extra, -1.0), axis=0, keepdims=True)
                    exts = jnp.where(slot == it, x, exts)
                s_ref[c] = jnp.where(hit, -jnp.inf, s)
                out.append((vals, poss, exts))
            return tuple(out)

        z = jnp.zeros((k, t), F32)
        return lax.fori_loop(0, k, body, tuple((z, z, z) for _ in ss))

    return list(pl.run_scoped(scoped, pltpu.VMEM((len(ss), r, t), F32)))


def _route_body(q_ref, keys_ref, cpos_ref, cvalid_ref, idx_ref, gate_ref):
    k = PEER_TOPK
    n_tiles = q_ref.shape[1] // LANES
    key_pos = lax.broadcasted_iota(jnp.int32, (PEER_N_KEYS, LANES), 0).astype(F32)
    cpos = cpos_ref[...]
    cvalid = cvalid_ref[...] > 0.5
    scores = [_nt(keys_ref[p], q_ref[p, pl.ds(lt * LANES, LANES), :].astype(BF16))
              for lt in range(n_tiles) for p in range(2)]
    tops = _topk_rows(scores, key_pos, [None] * len(scores), k)
    cands, experts = [], []
    for lt in range(n_tiles):
        (v0, i0, _), (v1, i1, _) = tops[2 * lt], tops[2 * lt + 1]
        vals = [v0[0:1] + v1]
        exps = [i0[0:1] * PEER_N_KEYS + i1]
        for i in range(1, 8):
            vals.append(v0[i:i + 1] + v1[0:8])
            exps.append(i0[i:i + 1] * PEER_N_KEYS + i1[0:8])
        vals.append(v0[8:k] + v1[0:1])
        exps.append(i0[8:k] * PEER_N_KEYS + i1[0:1])
        cands.append(jnp.where(cvalid, jnp.concatenate(vals, axis=0), -jnp.inf))
        experts.append(jnp.concatenate(exps, axis=0))
    picks = _topk_rows(cands, cpos, experts, k)
    for lt in range(n_tiles):
        best, _, expert = picks[lt]
        cols = pl.ds(lt * LANES, LANES)
        e = jnp.exp(best - jnp.max(best, axis=0, keepdims=True))
        gate_ref[:, cols] = e / jnp.sum(e, axis=0, keepdims=True)
        idx_ref[:, cols] = expert.astype(jnp.int32)


def _route(q3, keys, tok0, n):
    heads = q3.shape[0] // 2
    tb = 512 if tok0 % 512 == 0 and n % 512 == 0 else SHARE_GRANULE
    assert tok0 % tb == 0 and n % tb == 0
    cpos = jnp.broadcast_to(jnp.asarray(_CAND_POS)[:, None], (_N_CAND, LANES))
    cvalid = jnp.broadcast_to(jnp.asarray(_CAND_VALID, F32)[:, None], (_N_CAND, LANES))
    return pl.pallas_call(
        _route_body,
        grid=(n // tb, heads),
        in_specs=[pl.BlockSpec((2, tb, PEER_HALF), lambda i, h: (h, i + tok0 // tb, 0)),
                  pl.BlockSpec((2, PEER_N_KEYS, PEER_HALF), lambda i, h: (h, 0, 0)),
                  pl.BlockSpec((_N_CAND, LANES), lambda i, h: (0, 0)),
                  pl.BlockSpec((_N_CAND, LANES), lambda i, h: (0, 0))],
        out_specs=[pl.BlockSpec((PEER_TOPK, tb), lambda i, h: (h, i)),
                   pl.BlockSpec((PEER_TOPK, tb), lambda i, h: (h, i))],
        out_shape=[jax.ShapeDtypeStruct((heads * PEER_TOPK, n), jnp.int32),
                   jax.ShapeDtypeStruct((heads * PEER_TOPK, n), F32)],
        compiler_params=pltpu.CompilerParams(dimension_semantics=("parallel", "parallel"),
                                             vmem_limit_bytes=VMEM_LIMIT),
        name="route",
    )(q3, keys, cpos, cvalid)


def _sc_pipeline(idx_v, table_hbm, rows, sems, n_units, units_per_token, compute):
    parts = rows[0].shape[0] // SC_LANES
    assert n_units % 2 == 0

    def start(unit, slot):
        tl = unit // units_per_token
        g = unit % units_per_token
        for part in range(parts):
            iv = idx_v[tl, pl.ds((g * parts + part) * SC_LANES, SC_LANES)]
            pltpu.async_copy(table_hbm.at[iv], rows[slot].at[pl.ds(part * SC_LANES, SC_LANES)], sems[slot])

    def wait(slot):
        iv = idx_v[0, pl.ds(0, SC_LANES)]
        for part in range(parts):
            pltpu.make_async_copy(table_hbm.at[iv], rows[slot].at[pl.ds(part * SC_LANES, SC_LANES)],
                                  sems[slot]).wait()

    start(0, 0)

    def pair(i, carry):
        u0 = 2 * i
        start(u0 + 1, 1)
        wait(0)
        compute(u0 // units_per_token, u0 % units_per_token, rows[0])

        @pl.when(u0 + 2 < n_units)
        def _():
            start(u0 + 2, 0)

        wait(1)
        compute((u0 + 1) // units_per_token, (u0 + 1) % units_per_token, rows[1])
        return carry

    lax.fori_loop(0, n_units // 2, pair, 0)


def _sc_batch(tokens_per_worker):
    for tb in (40, 32, 24, 16, 8):
        if tokens_per_worker % tb == 0:
            return tb
    raise ValueError(f"tokens per SparseCore worker ({tokens_per_worker}) must be a multiple of 8")


def _peer_act(hn, idx, u, unit_rows=2 * SC_LANES):
    n, d = hn.shape
    kk = idx.shape[1]
    tpw = n // SC_WORKERS
    tb = _sc_batch(tpw)
    upt = kk // unit_rows
    parts = unit_rows // SC_LANES
    nchunk = d // SC_LANES
    mesh = plsc.VectorSubcoreMesh(core_axis_name="c", subcore_axis_name="s")

    def body(x_hbm, idx_hbm, u_hbm, out_hbm, x_v, idx_v, act_v, rows0, rows1, tmp_v, sem0, sem1):
        wid = lax.axis_index("s") * 2 + lax.axis_index("c")
        lane_row = lax.iota(jnp.int32, SC_LANES) * SC_LANES

        def compute(tl, g, r_ref):
            for part in range(parts):
                row0 = part * SC_LANES

                def cbody(c, accs, row0=row0):
                    xv = x_v[tl, pl.ds(c * SC_LANES, SC_LANES)]
                    return tuple(accs[r] + r_ref[row0 + r, pl.ds(c * SC_LANES, SC_LANES)] * xv
                                 for r in range(SC_LANES))
                accs = lax.fori_loop(0, nchunk, cbody,
                                     tuple(jnp.zeros((SC_LANES,), F32) for _ in range(SC_LANES)))
                for r in range(SC_LANES):
                    tmp_v[pl.ds(r * SC_LANES, SC_LANES)] = accs[r]
                tot = plsc.load_gather(tmp_v, [lane_row])
                for j in range(1, SC_LANES):
                    tot = tot + plsc.load_gather(tmp_v, [lane_row + j])
                act_v[tl, pl.ds((g * parts + part) * SC_LANES, SC_LANES)] = tot

        def batch(b, carry):
            t0 = wid * tpw + b * tb
            pltpu.sync_copy(x_hbm.at[pl.ds(t0, tb)], x_v)
            pltpu.sync_copy(idx_hbm.at[pl.ds(t0, tb)], idx_v)
            _sc_pipeline(idx_v, u_hbm, (rows0, rows1), (sem0, sem1), tb * upt, upt, compute)
            pltpu.sync_copy(act_v, out_hbm.at[pl.ds(t0, tb)])
            return carry

        lax.fori_loop(0, tpw // tb, batch, 0)

    return pl.kernel(
        body, mesh=mesh,
        compiler_params=pltpu.CompilerParams(needs_layout_passes=False),
        out_type=jax.ShapeDtypeStruct((n, kk), F32),
        scratch_types=[pltpu.VMEM((tb, d), F32), pltpu.VMEM((tb, kk), jnp.int32),
                       pltpu.VMEM((tb, kk), F32),
                       pltpu.VMEM((unit_rows, d), F32), pltpu.VMEM((unit_rows, d), F32),
                       pltpu.VMEM((SC_LANES * SC_LANES,), F32),
                       pltpu.SemaphoreType.DMA, pltpu.SemaphoreType.DMA],
        name="peer_act",
    )(hn, idx, u)


def _peer_out(w, idx, v, cb=16, unit_rows=2 * SC_LANES):
    n, kk = w.shape
    d = v.shape[1]
    tpw = n // SC_WORKERS
    tb = _sc_batch(tpw)
    upt = kk // unit_rows
    nchunk = d // SC_LANES
    mesh = plsc.VectorSubcoreMesh(core_axis_name="c", subcore_axis_name="s")

    def body(w_hbm, idx_hbm, v_hbm, out_hbm, w_v, idx_v, out_v, rows0, rows1, sem0, sem1):
        wid = lax.axis_index("s") * 2 + lax.axis_index("c")
        zero = jnp.zeros((SC_LANES,), F32)

        def compute(tl, g, r_ref):
            tls = jnp.full((SC_LANES,), tl, jnp.int32)

            def cb_body(ci, carry):
                c0 = ci * (cb * SC_LANES)
                accs = tuple(out_v[tl, pl.ds(c0 + j * SC_LANES, SC_LANES)] for j in range(cb))

                def rbody(r, accs):
                    wr = plsc.load_gather(w_v, [tls, jnp.full((SC_LANES,), g * unit_rows + r, jnp.int32)])
                    return tuple(accs[j] + wr * r_ref[r, pl.ds(c0 + j * SC_LANES, SC_LANES)]
                                 for j in range(cb))

                accs = lax.fori_loop(0, unit_rows, rbody, accs)
                for j in range(cb):
                    out_v[tl, pl.ds(c0 + j * SC_LANES, SC_LANES)] = accs[j]
                return carry

            lax.fori_loop(0, nchunk // cb, cb_body, 0)

        def batch(b, carry):
            t0 = wid * tpw + b * tb
            pltpu.sync_copy(w_hbm.at[pl.ds(t0, tb)], w_v)
            pltpu.sync_copy(idx_hbm.at[pl.ds(t0, tb)], idx_v)

            def zbody(i, carry):
                out_v[i // nchunk, pl.ds((i % nchunk) * SC_LANES, SC_LANES)] = zero
                return carry

            lax.fori_loop(0, tb * nchunk, zbody, 0)
            _sc_pipeline(idx_v, v_hbm, (rows0, rows1), (sem0, sem1), tb * upt, upt, compute)
            pltpu.sync_copy(out_v, out_hbm.at[pl.ds(t0, tb)])
            return carry

        lax.fori_loop(0, tpw // tb, batch, 0)

    return pl.kernel(
        body, mesh=mesh,
        compiler_params=pltpu.CompilerParams(needs_layout_passes=False),
        out_type=jax.ShapeDtypeStruct((n, d), F32),
        scratch_types=[pltpu.VMEM((tb, kk), F32), pltpu.VMEM((tb, kk), jnp.int32),
                       pltpu.VMEM((tb, d), F32),
                       pltpu.VMEM((unit_rows, d), F32), pltpu.VMEM((unit_rows, d), F32),
                       pltpu.SemaphoreType.DMA, pltpu.SemaphoreType.DMA],
        name="peer_out",
    )(w, idx, v)


SUBLANES = 8
TC_PEER_TOKENS = 8
ISSUE_UNROLL = 8


def _gelu(a):
    return a * (lax.erf(a / np.sqrt(2.0).astype(np.float32)) + 1.0) / 2.0


def _peer_tc_body(idx_ref, idx_next_ref, x_ref, g_ref, u_hbm, v_hbm, o_ref, ubuf, vbuf, sem_u, sem_v):
    tokens = x_ref.shape[0]
    kk = g_ref.shape[1]
    n_rows = tokens * kk
    step = pl.program_id(0)
    slot = step % 2

    def row_copy(table, buf, sem, expert, j):
        src = table.at[pl.ds(pl.multiple_of(expert * SUBLANES, SUBLANES), SUBLANES)]
        return pltpu.make_async_copy(src, buf.at[pl.ds(j * SUBLANES, SUBLANES)], sem)

    def issue_block(ids_ref, into):
        def issue(i, carry):
            for r in range(ISSUE_UNROLL):
                j = i * ISSUE_UNROLL + r
                expert = ids_ref[j]
                row_copy(u_hbm, ubuf.at[into], sem_u.at[into], expert, j).start()
                row_copy(v_hbm, vbuf.at[into], sem_v.at[into], expert, j).start(priority=1)
            return carry

        lax.fori_loop(0, n_rows // ISSUE_UNROLL, issue, 0)

    @pl.when(step == 0)
    def _():
        issue_block(idx_ref, 0)

    @pl.when(step + 1 < pl.num_programs(0))
    def _():
        issue_block(idx_next_ref, 1 - slot)

    def wait_all(table, buf, sem):
        pltpu.make_async_copy(table.at[pl.ds(0, n_rows * SUBLANES)], buf.at[slot], sem.at[slot]).wait()

    def rows_of(buf, t, s):
        return buf[slot, pl.ds(t * kk * SUBLANES + s, kk, stride=SUBLANES), :]

    wait_all(u_hbm, ubuf, sem_u)
    acts = []
    for t in range(tokens):
        acc = rows_of(ubuf, t, 0) * x_ref[t:t + 1, 0:LANES]
        for s in range(1, SUBLANES):
            acc = acc + rows_of(ubuf, t, s) * x_ref[t:t + 1, s * LANES:(s + 1) * LANES]
        acts.append(jnp.sum(acc, axis=1, keepdims=True))
    w = g_ref[0] * _gelu(jnp.concatenate(acts, axis=1))

    wait_all(v_hbm, vbuf, sem_v)
    for t in range(tokens):
        wt = w[:, t:t + 1]
        for s in range(SUBLANES):
            o_ref[t:t + 1, s * LANES:(s + 1) * LANES] = jnp.sum(rows_of(vbuf, t, s) * wt, axis=0, keepdims=True)


def _peer_tc(hn, idx, gate_t, u_tiles, v_tiles):
    m, d = hn.shape
    kk = idx.shape[1]
    tb = TC_PEER_TOKENS
    assert d == SUBLANES * LANES and m % tb == 0
    steps = m // tb
    g3 = gate_t.reshape(kk, steps, tb).transpose(1, 0, 2)
    rows = tb * kk * SUBLANES
    ids = idx.reshape(m * kk)
    return pl.pallas_call(
        _peer_tc_body,
        grid=(steps,),
        in_specs=[pl.BlockSpec((tb * kk,), lambda i: (i,), memory_space=pltpu.SMEM),
                  pl.BlockSpec((tb * kk,), lambda i: (jnp.minimum(i + 1, steps - 1),), memory_space=pltpu.SMEM),
                  pl.BlockSpec((tb, d), lambda i: (i, 0)),
                  pl.BlockSpec((1, kk, tb), lambda i: (i, 0, 0)),
                  pl.BlockSpec(memory_space=pl.ANY),
                  pl.BlockSpec(memory_space=pl.ANY)],
        out_specs=pl.BlockSpec((tb, d), lambda i: (i, 0)),
        out_shape=jax.ShapeDtypeStruct((m, d), F32),
        scratch_shapes=[pltpu.VMEM((2, rows, LANES), F32), pltpu.VMEM((2, rows, LANES), F32),
                        pltpu.SemaphoreType.DMA((2,)), pltpu.SemaphoreType.DMA((2,))],
        compiler_params=pltpu.CompilerParams(dimension_semantics=("arbitrary",),
                                             vmem_limit_bytes=VMEM_LIMIT),
        name="peer_tc",
    )(ids, ids, hn, g3, u_tiles, v_tiles)


def _gelu_gate_body(a_ref, g_ref, o_ref):
    o_ref[...] = g_ref[...] * _gelu(a_ref[...])


def _gelu_gate(act, gate, tm=SHARE_GRANULE):
    n, kk = act.shape
    tm = min(tm, n)
    assert n % tm == 0
    spec = pl.BlockSpec((tm, kk), lambda i: (i, 0))
    return pl.pallas_call(
        _gelu_gate_body, grid=(n // tm,), in_specs=[spec, spec], out_specs=spec,
        out_shape=jax.ShapeDtypeStruct((n, kk), F32),
        compiler_params=pltpu.CompilerParams(dimension_semantics=("parallel",)),
        name="gelu_gate",
    )(act, gate)


def _final_body(h_ref, g_ref, *rest, starts, aliased):
    o_ref = rest[-1]
    parts = rest[:len(starts)]
    assert len(rest) == len(starts) + 1 + int(aliased)
    peer = parts[0][...]
    for start, p_ref in zip(starts[1:], parts[1:]):
        peer = jnp.where(pl.program_id(0) >= start, p_ref[...], peer)
    h = h_ref[...] + peer
    o_ref[...] = h * lax.rsqrt(jnp.mean(h * h, axis=-1, keepdims=True) + EPS) * g_ref[...]


def _final(h1, peer_parts, g, out, row0, n_total, tm=256):
    n, d = h1.shape
    sizes = [p.shape[0] for p in peer_parts]
    assert sum(sizes) == n and all(m > 0 and m % tm == 0 for m in sizes) and row0 % tm == 0
    starts = tuple(int(v) // tm for v in np.cumsum([0] + sizes[:-1]))

    def part_spec(start, size):
        return pl.BlockSpec((tm, d), lambda i: (jnp.clip(i - start, 0, size // tm - 1), 0))

    in_specs = [pl.BlockSpec((tm, d), lambda i: (i, 0)), pl.BlockSpec((1, d), lambda i: (0, 0))]
    in_specs += [part_spec(s, m) for s, m in zip(starts, sizes)]
    args = [h1, g, *peer_parts]
    aliases = {}
    if out is not None:
        in_specs.append(pl.BlockSpec(memory_space=pl.ANY))
        aliases = {len(args): 0}
        args.append(out)
    return pl.pallas_call(
        functools.partial(_final_body, starts=starts, aliased=out is not None), grid=(n // tm,),
        in_specs=in_specs,
        out_specs=pl.BlockSpec((tm, d), lambda i: (i + row0 // tm, 0)),
        out_shape=jax.ShapeDtypeStruct((n_total, d), F32),
        input_output_aliases=aliases,
        compiler_params=pltpu.CompilerParams(dimension_semantics=("parallel",)),
        name="final_norm",
    )(*args)


def _tc_share(n_tokens):
    m = int(n_tokens * TC_SHARE) // SHARE_GRANULE * SHARE_GRANULE
    return min(max(m, SHARE_GRANULE), n_tokens - SHARE_GRANULE)


def kernel(x, norm1_g, w_in, hg_lower_logits, hg_norm_g, gla_w_gate_up, gla_b_gate, gla_norm_g,
           w_out, norm2_g, peer_w_q, peer_sub_keys, peer_u, peer_v, norm_f_g):
    batch, seq, d = x.shape
    depth = w_in.shape[0]
    assert depth == 1, "single-layer block"
    layer = 0

    w = w_in[layer]
    pad = LANES - GLA_GATE_RANK
    assert w.shape == (d, IN_WIDTH_PADDED - pad) and d == SUBLANES * LANES
    w_r = jnp.concatenate(
        [w[:, :OFF_GLOW + GLA_GATE_RANK], jnp.zeros((d, pad), w.dtype),
         w[:, OFF_GLOW + GLA_GATE_RANK:]], axis=1).astype(BF16)
    wgu = jnp.concatenate([gla_w_gate_up[layer], jnp.zeros((pad, GLA_KEY), F32)], axis=0).astype(BF16)

    wo = w_out[layer].astype(BF16)
    wq = peer_w_q[layer].astype(BF16)
    keys = peer_sub_keys[layer].reshape(PEER_HEADS * 2, PEER_N_KEYS, PEER_HALF).astype(BF16)
    n_experts = peer_u.shape[1]
    u_tiles = peer_u[layer].reshape(n_experts * SUBLANES, LANES)
    v_tiles = peer_v[layer].reshape(n_experts * SUBLANES, LANES)

    groups = PIPELINE_GROUPS if batch % PIPELINE_GROUPS == 0 else 1
    bpg = batch // groups
    ng = bpg * seq
    m_tc = _tc_share(ng)
    gf = norm_f_g.reshape(1, d)
    x2 = x.reshape(batch * seq, d)

    def after(value, anchor):
        value, _ = lax.optimization_barrier((value, anchor))
        return value

    half = ng // 2
    h1s, tc_ins, sc_units = [], [], []
    for gi in range(groups):
        proj = _inproj(x2, gi * ng, ng, norm1_g[layer].reshape(1, d), w_r)
        mixed = _mixers(proj, bpg, seq, hg_lower_logits, hg_norm_g[layer].reshape(1, HEAD_V), wgu,
                        gla_b_gate[layer].reshape(1, GLA_KEY), gla_norm_g[layer].reshape(1, HEAD_V))
        h1, hn, q3 = _outproj(x2, gi * ng, mixed, wo, norm2_g[layer].reshape(1, d), wq)
        h1s.append(h1)
        extra = SHARE_GRANULE * LAST_GROUP_EXTRA_GRANULES if 0 < gi == groups - 1 else 0
        m_g = min(m_tc + extra, ng - SHARE_GRANULE)
        bounds = (m_g, half, ng) if gi == 0 and m_g < half else (m_g, ng)
        units = []
        for lo, hi in zip(bounds[:-1], bounds[1:]):
            idx_t, gate_t = _route(q3, keys, lo, hi - lo)
            idx = idx_t.T
            units.append(dict(idx=idx, gate=gate_t.T, act=_peer_act(hn[lo:hi], idx, peer_u[layer])))
        sc_units.append(units)
        idx_t, gate_t = _route(q3, keys, 0, m_g)
        tc_ins.append((hn[:m_g], idx_t.T, gate_t))

    def sc_out(gi, anchor):
        outs = []
        for unit in sc_units[gi]:
            wts = _gelu_gate(after(unit["act"], anchor), unit["gate"])
            outs.append(_peer_out(wts, unit["idx"], peer_v[layer]))
        return wts, outs

    rows, peer_sc = [None] * groups, [None] * groups
    rows[0] = _peer_tc(*after(tc_ins[0], sc_units[-1][-1]["idx"]), u_tiles, v_tiles)
    anchor = rows[0]
    for gi in range(groups - 1):
        anchor, peer_sc[gi] = sc_out(gi, rows[0])
    if groups > 1:
        rows[1] = _peer_tc(*after(tc_ins[1], anchor), u_tiles, v_tiles)
        anchor = rows[1]
    anchor, peer_sc[groups - 1] = sc_out(groups - 1, anchor)
    out, finished = None, 0

    def finish(gi, out):
        return _final(h1s[gi], [rows[gi], *peer_sc[gi]], gf, out, gi * ng, batch * seq)

    for gi in range(2, groups):
        out = finish(gi - 2, out)
        finished = gi - 1
        rows[gi] = _peer_tc(*after(tc_ins[gi], (anchor, out)), u_tiles, v_tiles)
        anchor = rows[gi]
    for gi in range(finished, groups):
        out = finish(gi, out)
    return out.reshape(batch, seq, d)
```
